```python
import jax, jax.numpy as jnp
from jax import lax
import numpy as np

D_MODEL = 1024
BATCH = 2
SEQ = 16384
DEPTH = 4

N_MIXERS = 4
HEAD_DIM = 64
N_Q_HEADS = D_MODEL // HEAD_DIM
N_KV_HEADS = 4
GQA_GROUP = N_Q_HEADS // N_KV_HEADS
ATTN_Q_DIM = N_Q_HEADS * HEAD_DIM
ATTN_KV_DIM = N_KV_HEADS * HEAD_DIM
ATTN_IN_DIM = ATTN_Q_DIM + 2 * ATTN_KV_DIM
SWA_WINDOW = 128
SWA_BLOCK = 128
MOBA_BLOCK = 256
MOBA_TOP_K = 3
MOBA_Q_CHUNK = 64
GLA_HEADS = 4
GLA_KEY_DIM = D_MODEL // (2 * GLA_HEADS)
GLA_VAL_DIM = D_MODEL // GLA_HEADS
GLA_GATE_RANK = 16
GLA_GATE_TEMP = 16.0
GLA_IN_DIM = 2 * GLA_HEADS * GLA_KEY_DIM + 2 * GLA_HEADS * GLA_VAL_DIM + GLA_GATE_RANK
HGRN_EXPAND = 128
HGRN_HEADS = D_MODEL // HGRN_EXPAND
HGRN_KEY_DIM = HGRN_EXPAND
HGRN_VAL_DIM = D_MODEL // HGRN_HEADS
HGRN_IN_DIM = 4 * D_MODEL
LIN_CHUNK = 64
D_FF = ((8 * D_MODEL + 3 * 256 - 1) // (3 * 256)) * 256
RMS_EPS = 1e-6

kernel_name = 'hybrid_swa_moba_gla_hgrn2_trunk'


def rms_norm(x, w):
    xf = x.astype(jnp.float32)
    y = xf * lax.rsqrt(jnp.mean(xf * xf, axis=-1, keepdims=True) + RMS_EPS)
    return (y * w.astype(jnp.float32)).astype(x.dtype)


def head_rms_norm(o, w):
    return o * lax.rsqrt(jnp.mean(o * o, axis=-1, keepdims=True) + RMS_EPS) * w.astype(jnp.float32)


def alibi_slopes(n_heads):
    return jnp.exp2(-8.0 * jnp.arange(1, n_heads + 1, dtype=jnp.float32) / n_heads)


def split_qkv(h, w_in):
    B, T, _ = h.shape
    proj = h @ w_in
    q = proj[..., :ATTN_Q_DIM].reshape(B, T, N_KV_HEADS, GQA_GROUP, HEAD_DIM)
    k = proj[..., ATTN_Q_DIM:ATTN_Q_DIM + ATTN_KV_DIM].reshape(B, T, N_KV_HEADS, HEAD_DIM)
    v = proj[..., ATTN_Q_DIM + ATTN_KV_DIM:].reshape(B, T, N_KV_HEADS, HEAD_DIM)
    return q, k, v


def sliding_window_sink_attention(h, w_in, sinks, w_out):
    B, T, _ = h.shape
    L = SWA_BLOCK
    nb = T // L
    q, k, v = split_qkv(h, w_in)
    qb = q.reshape(B, nb, L, N_KV_HEADS, GQA_GROUP, HEAD_DIM)

    def with_prev(a):
        a = a.reshape(B, nb, L, N_KV_HEADS, HEAD_DIM)
        prev = jnp.concatenate([jnp.zeros_like(a[:, :1]), a[:, :-1]], axis=1)
        return jnp.concatenate([prev, a], axis=2)

    kw, vw = with_prev(k), with_prev(v)
    s = jnp.einsum('bnqhgd,bnkhd->bnhgqk', qb, kw, preferred_element_type=jnp.float32) * (HEAD_DIM ** -0.5)
    dist = (L + jnp.arange(L))[:, None] - jnp.arange(2 * L)[None, :]
    band = (dist >= 0) & (dist < SWA_WINDOW)
    has_prev = (jnp.arange(nb)[:, None, None] > 0) | (jnp.arange(2 * L)[None, None, :] >= L)
    mask = band[None] & has_prev
    slopes = alibi_slopes(N_Q_HEADS).reshape(N_KV_HEADS, GQA_GROUP)
    s = s - slopes[:, :, None, None] * dist.astype(jnp.float32)
    s = jnp.where(mask[None, :, None, None], s, -jnp.inf)
    sink = sinks.astype(jnp.float32).reshape(N_KV_HEADS, GQA_GROUP)[:, :, None, None]
    m = jnp.maximum(jnp.max(s, axis=-1, keepdims=True), sink)
    p = jnp.exp(s - m)
    denom = jnp.sum(p, axis=-1, keepdims=True) + jnp.exp(sink - m)
    o = jnp.einsum('bnhgqk,bnkhd->bnqhgd', p / denom, vw.astype(jnp.float32))
    return o.reshape(B, T, ATTN_Q_DIM).astype(h.dtype) @ w_out


def moba_attention(h, w_in, w_out):
    B, T, _ = h.shape
    Lb, Qc = MOBA_BLOCK, MOBA_Q_CHUNK
    q, k, v = split_qkv(h, w_in)
    Tp = ((T + Lb - 1) // Lb) * Lb
    pad = Tp - T
    q = jnp.pad(q.astype(jnp.float32), ((0, 0), (0, pad), (0, 0), (0, 0), (0, 0))) * (HEAD_DIM ** -0.5)
    k = jnp.pad(k.astype(jnp.float32), ((0, 0), (0, pad), (0, 0), (0, 0)))
    v = jnp.pad(v.astype(jnp.float32), ((0, 0), (0, pad), (0, 0), (0, 0)))
    nblk, nq = Tp // Lb, Tp // Qc
    n_sel = min(MOBA_TOP_K, nblk)
    kb = k.reshape(B, nblk, Lb, N_KV_HEADS, HEAD_DIM).transpose(0, 3, 1, 2, 4)
    vb = v.reshape(B, nblk, Lb, N_KV_HEADS, HEAD_DIM).transpose(0, 3, 1, 2, 4)
    k_mean = jnp.mean(kb, axis=3)
    gate = jnp.einsum('bthgd,bhnd->bhtn', q, k_mean)
    t_blk = jnp.arange(Tp) // Lb
    past = jnp.arange(nblk)[None, :] < t_blk[:, None]
    gate = jnp.where(past, gate, -jnp.inf)
    _, sel_idx = lax.top_k(gate, n_sel)
    sel_valid = jnp.arange(n_sel)[None, :] < t_blk[:, None]
    q_chunks = q.reshape(B, nq, Qc, N_KV_HEADS, GQA_GROUP, HEAD_DIM).transpose(1, 0, 2, 3, 4, 5)
    idx_chunks = sel_idx.reshape(B, N_KV_HEADS, nq, Qc, n_sel).transpose(2, 0, 1, 3, 4)
    valid_chunks = sel_valid.reshape(nq, Qc, n_sel)
    slopes = alibi_slopes(N_Q_HEADS).reshape(N_KV_HEADS, GQA_GROUP)
    gather_blocks = jax.vmap(jax.vmap(lambda blocks, ix: blocks[ix]))

    def attend(args):
        c, qc, idx, valid = args
        t_q = c * Qc + jnp.arange(Qc)
        own = (c * Qc) // Lb
        k_own = lax.dynamic_index_in_dim(kb, own, axis=2, keepdims=False)
        v_own = lax.dynamic_index_in_dim(vb, own, axis=2, keepdims=False)
        k_sel = gather_blocks(kb, idx)
        v_sel = gather_blocks(vb, idx)
        s_sel = jnp.einsum('bqhgd,bhqnkd->bhgqnk', qc, k_sel)
        dist_sel = (t_q[:, None, None] - (idx[..., None] * Lb + jnp.arange(Lb))).astype(jnp.float32)
        s_sel = s_sel - slopes[None, :, :, None, None, None] * dist_sel[:, :, None]
        s_sel = jnp.where(valid[:, :, None], s_sel, -jnp.inf)
        s_own = jnp.einsum('bqhgd,bhkd->bhgqk', qc, k_own)
        dist_own = t_q[:, None] - (own * Lb + jnp.arange(Lb))[None, :]
        s_own = s_own - slopes[:, :, None, None] * dist_own.astype(jnp.float32)
        s_own = jnp.where(dist_own >= 0, s_own, -jnp.inf)
        s_all = jnp.concatenate([s_sel.reshape(B, N_KV_HEADS, GQA_GROUP, Qc, n_sel * Lb), s_own], axis=-1)
        p = jax.nn.softmax(s_all, axis=-1)
        p_sel = p[..., :n_sel * Lb].reshape(B, N_KV_HEADS, GQA_GROUP, Qc, n_sel, Lb)
        p_own = p[..., n_sel * Lb:]
        return (jnp.einsum('bhgqnk,bhqnkd->bqhgd', p_sel, v_sel)
                + jnp.einsum('bhgqk,bhkd->bqhgd', p_own, v_own))

    o = lax.map(attend, (jnp.arange(nq), q_chunks, idx_chunks, valid_chunks))
    o = o.transpose(1, 0, 2, 3, 4, 5).reshape(B, Tp, ATTN_Q_DIM)[:, :T]
    return o.astype(h.dtype) @ w_out


def chunked_gated_linear_attention(q, k, v, log_g):
    B, T, H, K = q.shape
    V = v.shape[-1]
    L = LIN_CHUNK
    nc = T // L

    def to_chunks(a):
        return a.astype(jnp.float32).reshape(B, nc, L, H, a.shape[-1]).transpose(1, 0, 3, 2, 4)

    causal = jnp.tril(jnp.ones((L, L), dtype=bool))

    def step(S, inp):
        qc, kc, vc, gc = inp
        b = jnp.cumsum(gc, axis=2)
        b_last = b[:, :, -1:, :]
        q_dec = qc * jnp.exp(b)
        attn = jnp.einsum('bhtk,bhsk->bhts', q_dec, kc * jnp.exp(-b))
        attn = jnp.where(causal, attn, 0.0)
        o = jnp.einsum('bhts,bhsv->bhtv', attn, vc) + jnp.einsum('bhtk,bhkv->bhtv', q_dec, S)
        S = S * jnp.exp(b_last).swapaxes(-1, -2) + jnp.einsum('bhsk,bhsv->bhkv', kc * jnp.exp(b_last - b), vc)
        return S, o

    S0 = jnp.zeros((B, H, K, V), jnp.float32)
    _, o = lax.scan(step, S0, (to_chunks(q), to_chunks(k), to_chunks(v), to_chunks(log_g)))
    return o.transpose(1, 0, 3, 2, 4).reshape(B, T, H, V)


def gla_mixer(h, w_in, w_decay_up, b_decay, out_norm, w_out):
    B, T, _ = h.shape
    proj = h @ w_in
    dk, dv = GLA_HEADS * GLA_KEY_DIM, GLA_HEADS * GLA_VAL_DIM
    q = proj[..., :dk].reshape(B, T, GLA_HEADS, GLA_KEY_DIM).astype(jnp.float32) * (GLA_KEY_DIM ** -0.5)
    k = proj[..., dk:2 * dk].reshape(B, T, GLA_HEADS, GLA_KEY_DIM)
    v = proj[..., 2 * dk:2 * dk + dv].reshape(B, T, GLA_HEADS, GLA_VAL_DIM)
    g = proj[..., 2 * dk + dv:2 * dk + 2 * dv].reshape(B, T, GLA_HEADS, GLA_VAL_DIM)
    a = proj[..., 2 * dk + 2 * dv:]
    log_alpha = jax.nn.log_sigmoid((a @ w_decay_up + b_decay).astype(jnp.float32)) / GLA_GATE_TEMP
    log_alpha = log_alpha.reshape(B, T, GLA_HEADS, GLA_KEY_DIM)
    o = chunked_gated_linear_attention(q, k, v, log_alpha)
    o = head_rms_norm(o, out_norm) * jax.nn.silu(g.astype(jnp.float32))
    return o.reshape(B, T, dv).astype(h.dtype) @ w_out


def hgrn2_mixer(h, w_in, lower_bound, out_norm, w_out):
    B, T, _ = h.shape
    proj = h @ w_in
    d = D_MODEL
    q = jax.nn.silu(proj[..., :d].astype(jnp.float32)) * (HGRN_KEY_DIM ** -0.5)
    f = lower_bound + (1.0 - lower_bound) * jax.nn.sigmoid(proj[..., d:2 * d].astype(jnp.float32))
    i = proj[..., 2 * d:3 * d]
    g = proj[..., 3 * d:]
    shp_k = (B, T, HGRN_HEADS, HGRN_KEY_DIM)
    o = chunked_gated_linear_attention(q.reshape(shp_k), (1.0 - f).reshape(shp_k),
                                       i.reshape(B, T, HGRN_HEADS, HGRN_VAL_DIM), jnp.log(f).reshape(shp_k))
    o = head_rms_norm(o, out_norm) * jax.nn.silu(g.astype(jnp.float32)).reshape(B, T, HGRN_HEADS, HGRN_VAL_DIM)
    return o.reshape(B, T, HGRN_HEADS * HGRN_VAL_DIM).astype(h.dtype) @ w_out


def swiglu_ffn(h, w_gate_up, w_down):
    gu = h @ w_gate_up
    return (jax.nn.silu(gu[..., :D_FF]) * gu[..., D_FF:]) @ w_down


def setup_inputs(seed: int = 0) -> dict:
    key = jax.random.key(seed)
    ks = jax.random.split(key, 20)
    n_a, n_b, n_c, n_d = [(DEPTH - kind + N_MIXERS - 1) // N_MIXERS for kind in range(N_MIXERS)]

    def normal(k, shape):
        return jax.random.normal(k, shape, jnp.float32)

    def dense(k, shape, fan_in):
        return normal(k, shape) * (fan_in ** -0.5)

    def gain(k, shape):
        return 1.0 + 0.02 * normal(k, shape)

    return {
        'x': normal(ks[0], (BATCH, SEQ, D_MODEL)),
        'norm_mix': gain(ks[1], (DEPTH, D_MODEL)),
        'norm_ffn': gain(ks[2], (DEPTH, D_MODEL)),
        'swa_w_in': dense(ks[3], (n_a, D_MODEL, ATTN_IN_DIM), D_MODEL),
        'swa_sinks': 0.5 * normal(ks[4], (n_a, N_Q_HEADS)),
        'swa_w_out': dense(ks[5], (n_a, ATTN_Q_DIM, D_MODEL), ATTN_Q_DIM),
        'moba_w_in': dense(ks[6], (n_b, D_MODEL, ATTN_IN_DIM), D_MODEL),
        'moba_w_out': dense(ks[7], (n_b, ATTN_Q_DIM, D_MODEL), ATTN_Q_DIM),
        'gla_w_in': dense(ks[8], (n_c, D_MODEL, GLA_IN_DIM), D_MODEL),
        'gla_w_decay_up': dense(ks[9], (n_c, GLA_GATE_RANK, GLA_HEADS * GLA_KEY_DIM), GLA_GATE_RANK),
        'gla_b_decay': 0.1 * normal(ks[10], (n_c, GLA_HEADS * GLA_KEY_DIM)),
        'gla_out_norm': gain(ks[11], (n_c, GLA_VAL_DIM)),
        'gla_w_out': dense(ks[12], (n_c, GLA_HEADS * GLA_VAL_DIM, D_MODEL), GLA_HEADS * GLA_VAL_DIM),
        'hgrn_w_in': dense(ks[13], (n_d, D_MODEL, HGRN_IN_DIM), D_MODEL),
        'hgrn_lb_logits': 0.5 * normal(ks[14], (DEPTH, HGRN_HEADS * HGRN_KEY_DIM)),
        'hgrn_out_norm': gain(ks[15], (n_d, HGRN_VAL_DIM)),
        'hgrn_w_out': dense(ks[16], (n_d, HGRN_HEADS * HGRN_VAL_DIM, D_MODEL), HGRN_HEADS * HGRN_VAL_DIM),
        'ffn_w_gate_up': dense(ks[17], (DEPTH, D_MODEL, 2 * D_FF), D_MODEL),
        'ffn_w_down': dense(ks[18], (DEPTH, D_FF, D_MODEL), D_FF),
        'final_norm': gain(ks[19], (D_MODEL,)),
    }


def reference(x, norm_mix, norm_ffn, swa_w_in, swa_sinks, swa_w_out, moba_w_in, moba_w_out,
              gla_w_in, gla_w_decay_up, gla_b_decay, gla_out_norm, gla_w_out,
              hgrn_w_in, hgrn_lb_logits, hgrn_out_norm, hgrn_w_out,
              ffn_w_gate_up, ffn_w_down, final_norm):
    lb_p = jax.nn.softmax(hgrn_lb_logits.astype(jnp.float32), axis=0)
    lb_table = jnp.cumsum(lb_p, axis=0) - lb_p[0]
    for i in range(DEPTH):
        kind, j = i % N_MIXERS, i // N_MIXERS
        h = rms_norm(x, norm_mix[i])
        if kind == 0:
            mix = sliding_window_sink_attention(h, swa_w_in[j], swa_sinks[j], swa_w_out[j])
        elif kind == 1:
            mix = moba_attention(h, moba_w_in[j], moba_w_out[j])
        elif kind == 2:
            mix = gla_mixer(h, gla_w_in[j], gla_w_decay_up[j], gla_b_decay[j], gla_out_norm[j], gla_w_out[j])
        else:
            mix = hgrn2_mixer(h, hgrn_w_in[j], lb_table[i], hgrn_out_norm[j], hgrn_w_out[j])
        x = x + mix.astype(x.dtype)
        x = x + swiglu_ffn(rms_norm(x, norm_ffn[i]), ffn_w_gate_up[i], ffn_w_down[i]).astype(x.dtype)
    return rms_norm(x, final_norm)
```

```python
import functools
import math

import jax
import jax.numpy as jnp
from jax import lax
from jax.experimental import pallas as pl
from jax.experimental.pallas import tpu as pltpu

F32 = jnp.float32
BF16 = jnp.bfloat16

D_MODEL = 1024
HEAD_DIM = 64
N_Q_HEADS = 16
N_KV_HEADS = 4
GQA_GROUP = 4
ATTN_Q_DIM = 1024
ATTN_KV_DIM = 256
SWA_BLOCK = 128
MOBA_BLOCK = 256
MOBA_TOP_K = 3
GLA_HEADS = 4
GLA_KEY_DIM = 128
GLA_VAL_DIM = 256
GLA_GATE_RANK = 16
GLA_GATE_TEMP = 16.0
HGRN_HEADS = 8
HGRN_KEY_DIM = 128
HGRN_VAL_DIM = 128
LIN_CHUNK = 64
D_FF = 2816
RMS_EPS = 1e-6
N_MIXERS = 4

LANES = 128
VMEM_LIMIT = 56 * 1024 * 1024
FFN_CHUNK = 256
ROW_TILE = 512
LIN_ROWS = 256

ALIBI_SLOPES = tuple(2.0 ** (-8.0 * (i + 1) / N_Q_HEADS) for i in range(N_Q_HEADS))
NEG_INF = float("-inf")


def _params(sem):
    return pltpu.CompilerParams(dimension_semantics=sem, vmem_limit_bytes=VMEM_LIMIT)


def _dot(a, b):
    return jnp.dot(a, b, preferred_element_type=F32)


def _dot_nt(a, b):
    return lax.dot_general(a, b, (((1,), (1,)), ((), ())), preferred_element_type=F32)


def _dot_tn(a, b):
    return lax.dot_general(a, b, (((0,), (0,)), ((), ())), preferred_element_type=F32)


def _split(x):
    hi = x.astype(BF16)
    lo = (x - hi.astype(F32)).astype(BF16)
    return hi, lo


def _dot_split(a, b):
    ah, al = _split(a)
    bh, bl = _split(b)
    return _dot(ah, bh) + (_dot(ah, bl) + _dot(al, bh))


def _rms(x, gain):
    return x * lax.rsqrt(jnp.mean(x * x, axis=-1, keepdims=True) + RMS_EPS) * gain


def _sigmoid(x):
    return 1.0 / (1.0 + jnp.exp(-x))


def _silu(x):
    return x * _sigmoid(x)


def _norm_proj_kernel(x_ref, g_ref, w_ref, o_ref, h_ref):
    @pl.when(pl.program_id(1) == 0)
    def _():
        h_ref[...] = _rms(x_ref[...], g_ref[...]).astype(BF16)

    o_ref[...] = _dot(h_ref[...], w_ref[...])


def _norm_proj(x, gain, w, tn):
    n, d = x.shape
    dout = w.shape[1]
    return pl.pallas_call(
        _norm_proj_kernel,
        out_shape=jax.ShapeDtypeStruct((n, dout), F32),
        grid=(n // ROW_TILE, dout // tn),
        in_specs=[
            pl.BlockSpec((ROW_TILE, d), lambda i, j: (i, 0)),
            pl.BlockSpec((1, d), lambda i, j: (0, 0)),
            pl.BlockSpec((d, tn), lambda i, j: (0, j)),
        ],
        out_specs=pl.BlockSpec((ROW_TILE, tn), lambda i, j: (i, j)),
        scratch_shapes=[pltpu.VMEM((ROW_TILE, d), BF16)],
        compiler_params=_params(("parallel", "arbitrary")),
        name="norm_proj",
    )(x, gain.reshape(1, d), w)


def _ffn_kernel(x_ref, o_ref, wo_ref, g_ref, wgu_ref, wd_ref, fg_ref, out_ref,
                x1_ref, h_ref, acc_ref, *, final_norm):
    x1 = x_ref[...] + _dot(o_ref[...].astype(BF16), wo_ref[...])
    x1_ref[...] = x1
    h_ref[...] = _rms(x1, g_ref[...]).astype(BF16)
    for c in range(D_FF // FFN_CHUNK):
        h = h_ref[...]
        gate = _dot(h, wgu_ref[:, c * FFN_CHUNK:(c + 1) * FFN_CHUNK])
        up = _dot(h, wgu_ref[:, D_FF + c * FFN_CHUNK:D_FF + (c + 1) * FFN_CHUNK])
        act = (_silu(gate) * up).astype(BF16)
        part = _dot(act, wd_ref[c * FFN_CHUNK:(c + 1) * FFN_CHUNK, :])
        if c == 0:
            acc_ref[...] = part
        else:
            acc_ref[...] += part
    y = x1_ref[...] + acc_ref[...]
    if final_norm:
        y = _rms(y, fg_ref[...])
    out_ref[...] = y


def _ffn(x, o, wo, gain, wgu, wd, final_gain, final_norm):
    n, d = x.shape
    const = lambda i: (0, 0)
    row = lambda i: (i, 0)
    return pl.pallas_call(
        functools.partial(_ffn_kernel, final_norm=final_norm),
        out_shape=jax.ShapeDtypeStruct((n, d), F32),
        grid=(n // ROW_TILE,),
        in_specs=[
            pl.BlockSpec((ROW_TILE, d), row),
            pl.BlockSpec((ROW_TILE, d), row),
            pl.BlockSpec((d, d), const, pipeline_mode=pl.Buffered(1)),
            pl.BlockSpec((1, d), const),
            pl.BlockSpec((d, 2 * D_FF), const, pipeline_mode=pl.Buffered(1)),
            pl.BlockSpec((D_FF, d), const, pipeline_mode=pl.Buffered(1)),
            pl.BlockSpec((1, d), const),
        ],
        out_specs=pl.BlockSpec((ROW_TILE, d), row),
        scratch_shapes=[
            pltpu.VMEM((ROW_TILE, d), F32),
            pltpu.VMEM((ROW_TILE, d), BF16),
            pltpu.VMEM((ROW_TILE, d), F32),
        ],
        compiler_params=_params(("parallel",)),
        name="outproj_ffn",
    )(x, o, wo, gain.reshape(1, d), wgu, wd, final_gain.reshape(1, d))


def _swa_kernel(q_ref, kp_ref, ko_ref, vp_ref, vo_ref, sink_ref, o_ref):
    L = SWA_BLOCK
    n = pl.program_id(1)
    k = jnp.concatenate([kp_ref[...], ko_ref[...]], axis=0).astype(BF16)
    v = jnp.concatenate([vp_ref[...], vo_ref[...]], axis=0).astype(BF16)
    qi = lax.broadcasted_iota(jnp.int32, (L, 2 * L), 0)
    kj = lax.broadcasted_iota(jnp.int32, (L, 2 * L), 1)
    dist_i = (L + qi) - kj
    dist = dist_i.astype(F32)
    first_key = jnp.where(n > 0, 0, L)
    mask = (dist_i >= 0) & (dist_i < L) & (kj >= first_key)
    for hd in range(N_Q_HEADS):
        kv = hd // GQA_GROUP
        ks = k[:, kv * HEAD_DIM:(kv + 1) * HEAD_DIM]
        vs = v[:, kv * HEAD_DIM:(kv + 1) * HEAD_DIM]
        q = (q_ref[:, hd * HEAD_DIM:(hd + 1) * HEAD_DIM] * (HEAD_DIM ** -0.5)).astype(BF16)
        s = _dot_nt(q, ks) - ALIBI_SLOPES[hd] * dist
        s = jnp.where(mask, s, NEG_INF)
        sink = sink_ref[:, hd:hd + 1]
        m = jnp.maximum(jnp.max(s, axis=-1, keepdims=True), sink)
        p = jnp.exp(s - m)
        denom = jnp.sum(p, axis=-1, keepdims=True) + jnp.exp(sink - m)
        o_ref[:, hd * HEAD_DIM:(hd + 1) * HEAD_DIM] = _dot(p.astype(BF16), vs) / denom


def _swa_core(proj, sinks, batch, seq):
    n = proj.shape[0]
    nb = seq // SWA_BLOCK
    kcol = ATTN_Q_DIM // ATTN_KV_DIM
    own = lambda c: (lambda b, i: (b * nb + i, c))
    prev = lambda c: (lambda b, i: (b * nb + jnp.maximum(i - 1, 0), c))
    return pl.pallas_call(
        _swa_kernel,
        out_shape=jax.ShapeDtypeStruct((n, ATTN_Q_DIM), F32),
        grid=(batch, nb),
        in_specs=[
            pl.BlockSpec((SWA_BLOCK, ATTN_Q_DIM), own(0)),
            pl.BlockSpec((SWA_BLOCK, ATTN_KV_DIM), prev(kcol)),
            pl.BlockSpec((SWA_BLOCK, ATTN_KV_DIM), own(kcol)),
            pl.BlockSpec((SWA_BLOCK, ATTN_KV_DIM), prev(kcol + 1)),
            pl.BlockSpec((SWA_BLOCK, ATTN_KV_DIM), own(kcol + 1)),
            pl.BlockSpec((1, N_Q_HEADS), lambda b, i: (0, 0)),
        ],
        out_specs=pl.BlockSpec((SWA_BLOCK, ATTN_Q_DIM), own(0)),
        compiler_params=_params(("parallel", "parallel")),
        name="swa_core",
    )(proj, proj, proj, proj, proj, sinks.reshape(1, N_Q_HEADS))


_KMEAN_BLOCKS = 8


def _kmean_kernel(k_ref, o_ref):
    k = k_ref[...].reshape(_KMEAN_BLOCKS, MOBA_BLOCK, ATTN_KV_DIM)
    o_ref[...] = jnp.sum(k, axis=1) * (1.0 / MOBA_BLOCK)


def _moba_kmean(proj):
    n = proj.shape[0]
    rows = _KMEAN_BLOCKS * MOBA_BLOCK
    return pl.pallas_call(
        _kmean_kernel,
        out_shape=jax.ShapeDtypeStruct((n // MOBA_BLOCK, ATTN_KV_DIM), F32),
        grid=(n // rows,),
        in_specs=[pl.BlockSpec((rows, ATTN_KV_DIM), lambda i: (i, ATTN_Q_DIM // ATTN_KV_DIM))],
        out_specs=pl.BlockSpec((_KMEAN_BLOCKS, ATTN_KV_DIM), lambda i: (i, 0)),
        compiler_params=_params(("parallel",)),
        name="moba_kmean",
    )(proj)


def _moba_select_kernel(q_ref, km_ref, sel_ref, *, nblk):
    n = pl.program_id(1)
    q = q_ref[...] * (HEAD_DIM ** -0.5)
    gate = _dot_split(q, km_ref[0])
    lane_i = lax.broadcasted_iota(jnp.int32, gate.shape, 1)
    blk = jnp.bitwise_and(lane_i, nblk - 1)
    head = jnp.right_shift(lane_i, int(math.log2(nblk)))
    lane = lane_i.astype(F32)
    gate = jnp.where(blk < n, gate, NEG_INF)
    sel = jnp.zeros(gate.shape, F32)
    for h in range(N_KV_HEADS):
        g = jnp.where(head == h, gate, NEG_INF)
        for _ in range(MOBA_TOP_K):
            mx = jnp.max(g, axis=-1, keepdims=True)
            is_max = (g == mx) & (mx > NEG_INF)
            first = jnp.min(jnp.where(is_max, lane, float(4 * LANES)), axis=-1, keepdims=True)
            pick = lane == first
            sel = jnp.where(pick, 1.0, sel)
            g = jnp.where(pick, NEG_INF, g)
    sel_ref[...] = sel


def _moba_select(proj, km_mat, batch, seq):
    n = proj.shape[0]
    nblk = seq // MOBA_BLOCK
    width = N_KV_HEADS * nblk
    return pl.pallas_call(
        functools.partial(_moba_select_kernel, nblk=nblk),
        out_shape=jax.ShapeDtypeStruct((n, width), F32),
        grid=(batch, nblk),
        in_specs=[
            pl.BlockSpec((MOBA_BLOCK, ATTN_Q_DIM), lambda b, i: (b * nblk + i, 0)),
            pl.BlockSpec((1, ATTN_Q_DIM, width), lambda b, i: (b, 0, 0)),
        ],
        out_specs=pl.BlockSpec((MOBA_BLOCK, width), lambda b, i: (b * nblk + i, 0)),
        compiler_params=_params(("parallel", "parallel")),
        name="moba_select",
    )(proj, km_mat)


def _moba_attn_kernel(nq_ref, jk_ref, q_ref, k_ref, v_ref, sel_ref, o_ref,
                      m_ref, l_ref, acc_ref, *, nblk):
    Lb = MOBA_BLOCK
    step = pl.program_id(1)
    n = nq_ref[step]
    j = jk_ref[step]

    @pl.when(j == 0)
    def _():
        m_ref[...] = jnp.full(m_ref.shape, NEG_INF, F32)
        l_ref[...] = jnp.zeros(l_ref.shape, F32)
        acc_ref[...] = jnp.zeros(acc_ref.shape, F32)

    k = k_ref[...].astype(BF16)
    v = v_ref[...].astype(BF16)
    qi = lax.broadcasted_iota(jnp.int32, (Lb, Lb), 0)
    kj = lax.broadcasted_iota(jnp.int32, (Lb, Lb), 1)
    rel = qi - kj
    dist = (rel + (n - j) * Lb).astype(F32)
    own = j == n
    min_rel = jnp.where(own, 0, -Lb)
    sel = sel_ref[...]
    lane = lax.broadcasted_iota(jnp.int32, sel.shape, 1)
    for kv in range(N_KV_HEADS):
        picked = jnp.sum(jnp.where(lane == kv * nblk + j, sel, 0.0), axis=-1, keepdims=True)
        picked = jnp.where(own, 1.0, picked)
        valid = (rel >= min_rel) & (picked > 0.5)
        ks = k[:, kv * HEAD_DIM:(kv + 1) * HEAD_DIM]
        vs = v[:, kv * HEAD_DIM:(kv + 1) * HEAD_DIM]
        for g in range(GQA_GROUP):
            hd = kv * GQA_GROUP + g
            cols = slice(hd * HEAD_DIM, (hd + 1) * HEAD_DIM)
            q = (q_ref[:, cols] * (HEAD_DIM ** -0.5)).astype(BF16)
            s = _dot_nt(q, ks) - ALIBI_SLOPES[hd] * dist
            s = jnp.where(valid, s, NEG_INF)
            m_old = m_ref[hd]
            m_new = jnp.maximum(m_old, jnp.max(s, axis=-1, keepdims=True))
            m_safe = jnp.where(m_new == NEG_INF, 0.0, m_new)
            alpha = jnp.exp(m_old - m_safe)
            p = jnp.exp(s - m_safe)
            l_ref[hd] = alpha * l_ref[hd] + jnp.sum(p, axis=-1, keepdims=True)
            acc_ref[:, cols] = alpha * acc_ref[:, cols] + _dot(p.astype(BF16), vs)
            m_ref[hd] = m_new

    @pl.when(own)
    def _():
        for hd in range(N_Q_HEADS):
            cols = slice(hd * HEAD_DIM, (hd + 1) * HEAD_DIM)
            o_ref[:, cols] = acc_ref[:, cols] / l_ref[hd]


def _moba_attn(proj, sel, batch, seq):
    n = proj.shape[0]
    nblk = seq // MOBA_BLOCK
    pairs = [(i, j) for i in range(nblk) for j in range(i + 1)]
    nq = jnp.asarray([p[0] for p in pairs], jnp.int32)
    jk = jnp.asarray([p[1] for p in pairs], jnp.int32)
    kcol = ATTN_Q_DIM // ATTN_KV_DIM
    qmap = lambda b, s, nq_r, jk_r: (b * nblk + nq_r[s], 0)
    grid_spec = pltpu.PrefetchScalarGridSpec(
        num_scalar_prefetch=2,
        grid=(batch, len(pairs)),
        in_specs=[
            pl.BlockSpec((MOBA_BLOCK, ATTN_Q_DIM), qmap),
            pl.BlockSpec((MOBA_BLOCK, ATTN_KV_DIM), lambda b, s, nq_r, jk_r: (b * nblk + jk_r[s], kcol)),
            pl.BlockSpec((MOBA_BLOCK, ATTN_KV_DIM), lambda b, s, nq_r, jk_r: (b * nblk + jk_r[s], kcol + 1)),
            pl.BlockSpec((MOBA_BLOCK, N_KV_HEADS * nblk), qmap),
        ],
        out_specs=pl.BlockSpec((MOBA_BLOCK, ATTN_Q_DIM), qmap),
        scratch_shapes=[
            pltpu.VMEM((N_Q_HEADS, MOBA_BLOCK, 1), F32),
            pltpu.VMEM((N_Q_HEADS, MOBA_BLOCK, 1), F32),
            pltpu.VMEM((MOBA_BLOCK, ATTN_Q_DIM), F32),
        ],
    )
    return pl.pallas_call(
        functools.partial(_moba_attn_kernel, nblk=nblk),
        out_shape=jax.ShapeDtypeStruct((n, ATTN_Q_DIM), F32),
        grid_spec=grid_spec,
        compiler_params=_params(("parallel", "arbitrary")),
        name="moba_attn",
    )(nq, jk, proj, proj, proj, sel)


def _moba_core(proj, batch, seq):
    nblk = seq // MOBA_BLOCK
    kmean = _moba_kmean(proj)
    km = kmean.reshape(batch, nblk, N_KV_HEADS, HEAD_DIM).transpose(0, 2, 3, 1)
    eye = jnp.eye(N_KV_HEADS, dtype=F32)
    km_mat = jnp.einsum('bhdj,hH->bhdHj', km, eye)
    km_mat = jnp.broadcast_to(km_mat[:, :, None], (batch, N_KV_HEADS, GQA_GROUP, HEAD_DIM, N_KV_HEADS, nblk))
    km_mat = km_mat.reshape(batch, ATTN_Q_DIM, N_KV_HEADS * nblk)
    sel = _moba_select(proj, km_mat, batch, seq)
    return _moba_attn(proj, sel, batch, seq)


def _cumsum_rows(tri, x):
    hi = x.astype(BF16)
    r1 = x - hi.astype(F32)
    mid = r1.astype(BF16)
    lo = (r1 - mid.astype(F32)).astype(BF16)
    return _dot(tri, hi) + (_dot(tri, mid) + _dot(tri, lo))


def _lin_chunk(q, k, v, log_g, st_ref, h, tri, causal):
    b = _cumsum_rows(tri, log_g)
    b_last = b[LIN_CHUNK - 1:LIN_CHUNK, :]
    q_dec = (q * jnp.exp(b)).astype(BF16)
    k_dec = (k * jnp.exp(-b)).astype(BF16)
    attn = jnp.where(causal, _dot_nt(q_dec, k_dec), 0.0).astype(BF16)
    vb = v.astype(BF16)
    st = st_ref[h]
    o = _dot(attn, vb) + _dot_nt(q_dec, st.astype(BF16))
    k_tail = (k * jnp.exp(b_last - b)).astype(BF16)
    st_ref[h] = st * jnp.exp(b_last) + _dot_tn(vb, k_tail)
    return o


def _head_norm_gate(o, gain, g):
    o = o * lax.rsqrt(jnp.mean(o * o, axis=-1, keepdims=True) + RMS_EPS) * gain
    return o * _silu(g)


def _tri_causal():
    r = lax.broadcasted_iota(jnp.int32, (LIN_CHUNK, LIN_CHUNK), 0)
    c = lax.broadcasted_iota(jnp.int32, (LIN_CHUNK, LIN_CHUNK), 1)
    causal = r >= c
    return causal.astype(BF16), causal


def _gla_kernel(q_ref, k_ref, v_ref, g_ref, a_ref, wd_ref, bd_ref, gain_ref, o_ref, st_ref):
    @pl.when(pl.program_id(1) == 0)
    def _():
        st_ref[...] = jnp.zeros(st_ref.shape, F32)

    tri, causal = _tri_causal()
    K, V = GLA_KEY_DIM, GLA_VAL_DIM

    def body(c, carry):
        rows = pl.ds(pl.multiple_of(c * LIN_CHUNK, LIN_CHUNK), LIN_CHUNK)
        z = _dot_split(a_ref[rows, :], wd_ref[...]) + bd_ref[...]
        log_alpha = (jnp.minimum(z, 0.0) - jnp.log(1.0 + jnp.exp(-jnp.abs(z)))) * (1.0 / GLA_GATE_TEMP)
        for h in range(GLA_HEADS):
            q = q_ref[rows, h * K:(h + 1) * K] * (K ** -0.5)
            k = k_ref[rows, h * K:(h + 1) * K]
            v = v_ref[rows, h * V:(h + 1) * V]
            o = _lin_chunk(q, k, v, log_alpha[:, h * K:(h + 1) * K], st_ref, h, tri, causal)
            o_ref[rows, h * V:(h + 1) * V] = _head_norm_gate(o, gain_ref[...], g_ref[rows, h * V:(h + 1) * V])
        return carry

    lax.fori_loop(0, LIN_ROWS // LIN_CHUNK, body, 0)


def _gla_core(proj, wd_pad, bd, gain, batch, seq):
    n = proj.shape[0]
    nt = seq // LIN_ROWS
    dk = GLA_HEADS * GLA_KEY_DIM
    dv = GLA_HEADS * GLA_VAL_DIM
    rows = lambda c: (lambda b, t: (b * nt + t, c))
    const = lambda b, t: (0, 0)
    return pl.pallas_call(
        _gla_kernel,
        out_shape=jax.ShapeDtypeStruct((n, dv), F32),
        grid=(batch, nt),
        in_specs=[
            pl.BlockSpec((LIN_ROWS, dk), rows(0)),
            pl.BlockSpec((LIN_ROWS, dk), rows(1)),
            pl.BlockSpec((LIN_ROWS, dv), rows(1)),
            pl.BlockSpec((LIN_ROWS, dv), rows(2)),
            pl.BlockSpec((LIN_ROWS, LANES), rows((2 * dk + 2 * dv) // LANES)),
            pl.BlockSpec((LANES, dk), const),
            pl.BlockSpec((1, dk), const),
            pl.BlockSpec((1, GLA_VAL_DIM), const),
        ],
        out_specs=pl.BlockSpec((LIN_ROWS, dv), rows(0)),
        scratch_shapes=[pltpu.VMEM((GLA_HEADS, GLA_VAL_DIM, GLA_KEY_DIM), F32)],
        compiler_params=_params(("parallel", "arbitrary")),
        name="gla_core",
    )(proj, proj, proj, proj, proj, wd_pad, bd.reshape(1, dk), gain.reshape(1, GLA_VAL_DIM))


def _hgrn_kernel(q_ref, f_ref, i_ref, g_ref, lbl_ref, gain_ref, o_ref, st_ref, *, layer):
    @pl.when(pl.program_id(1) == 0)
    def _():
        st_ref[...] = jnp.zeros(st_ref.shape, F32)

    tri, causal = _tri_causal()
    K, V = HGRN_KEY_DIM, HGRN_VAL_DIM
    logits = lbl_ref[...]
    e = jnp.exp(logits - jnp.max(logits, axis=0, keepdims=True))
    p = e / jnp.sum(e, axis=0, keepdims=True)
    lb = jnp.zeros((1, logits.shape[1]), F32)
    for l in range(1, layer + 1):
        lb = lb + p[l:l + 1, :]

    def body(c, carry):
        rows = pl.ds(pl.multiple_of(c * LIN_CHUNK, LIN_CHUNK), LIN_CHUNK)
        for h in range(HGRN_HEADS):
            cols = slice(h * K, (h + 1) * K)
            q = _silu(q_ref[rows, cols]) * (K ** -0.5)
            lbh = lb[:, cols]
            f = lbh + (1.0 - lbh) * _sigmoid(f_ref[rows, cols])
            o = _lin_chunk(q, 1.0 - f, i_ref[rows, cols], jnp.log(f), st_ref, h, tri, causal)
            o_ref[rows, cols] = _head_norm_gate(o, gain_ref[...], g_ref[rows, cols])
        return carry

    lax.fori_loop(0, LIN_ROWS // LIN_CHUNK, body, 0)


def _hgrn_core(proj, lb_logits, gain, layer, batch, seq):
    n = proj.shape[0]
    nt = seq // LIN_ROWS
    d = D_MODEL
    rows = lambda c: (lambda b, t: (b * nt + t, c))
    const = lambda b, t: (0, 0)
    return pl.pallas_call(
        functools.partial(_hgrn_kernel, layer=layer),
        out_shape=jax.ShapeDtypeStruct((n, d), F32),
        grid=(batch, nt),
        in_specs=[
            pl.BlockSpec((LIN_ROWS, d), rows(0)),
            pl.BlockSpec((LIN_ROWS, d), rows(1)),
            pl.BlockSpec((LIN_ROWS, d), rows(2)),
            pl.BlockSpec((LIN_ROWS, d), rows(3)),
            pl.BlockSpec(lb_logits.shape, const),
            pl.BlockSpec((1, HGRN_VAL_DIM), const),
        ],
        out_specs=pl.BlockSpec((LIN_ROWS, d), rows(0)),
        scratch_shapes=[pltpu.VMEM((HGRN_HEADS, HGRN_VAL_DIM, HGRN_KEY_DIM), F32)],
        compiler_params=_params(("parallel", "arbitrary")),
        name="hgrn_core",
    )(proj, proj, proj, proj, lb_logits, gain.reshape(1, HGRN_VAL_DIM))


def kernel(x, norm_mix, norm_ffn, swa_w_in, swa_sinks, swa_w_out, moba_w_in, moba_w_out, gla_w_in, gla_w_decay_up, gla_b_decay, gla_out_norm, gla_w_out, hgrn_w_in, hgrn_lb_logits, hgrn_out_norm, hgrn_w_out, ffn_w_gate_up, ffn_w_down, final_norm):
    batch, seq, d = x.shape
    depth = norm_mix.shape[0]
    xf = x.reshape(batch * seq, d)
    for i in range(depth):
        kind, j = i % N_MIXERS, i // N_MIXERS
        if kind == 0:
            proj = _norm_proj(xf, norm_mix[i], swa_w_in[j].astype(BF16), 512)
            o = _swa_core(proj, swa_sinks[j], batch, seq)
            w_out = swa_w_out[j]
        elif kind == 1:
            proj = _norm_proj(xf, norm_mix[i], moba_w_in[j].astype(BF16), 512)
            o = _moba_core(proj, batch, seq)
            w_out = moba_w_out[j]
        elif kind == 2:
            pad = LANES - GLA_GATE_RANK
            w_in = jnp.pad(gla_w_in[j], ((0, 0), (0, pad))).astype(BF16)
            wd_pad = jnp.pad(gla_w_decay_up[j], ((0, pad), (0, 0)))
            proj = _norm_proj(xf, norm_mix[i], w_in, 640)
            o = _gla_core(proj, wd_pad, gla_b_decay[j], gla_out_norm[j], batch, seq)
            w_out = gla_w_out[j]
        else:
            proj = _norm_proj(xf, norm_mix[i], hgrn_w_in[j].astype(BF16), 1024)
            o = _hgrn_core(proj, hgrn_lb_logits, hgrn_out_norm[j], i, batch, seq)
            w_out = hgrn_w_out[j]
        last = i == depth - 1
        xf = _ffn(xf, o, w_out.astype(BF16), norm_ffn[i], ffn_w_gate_up[i].astype(BF16),
                  ffn_w_down[i].astype(BF16), final_norm, last)
    return xf.reshape(batch, seq, d)
```

```python
import functools
import math

import jax
import jax.numpy as jnp
from jax import lax
from jax.experimental import pallas as pl
from jax.experimental.pallas import tpu as pltpu

F32 = jnp.float32
BF16 = jnp.bfloat16

D_MODEL = 1024
HEAD_DIM = 64
N_Q_HEADS = 16
N_KV_HEADS = 4
GQA_GROUP = 4
ATTN_Q_DIM = 1024
ATTN_KV_DIM = 256
SWA_BLOCK = 128
MOBA_BLOCK = 256
MOBA_TOP_K = 3
GLA_HEADS = 4
GLA_KEY_DIM = 128
GLA_VAL_DIM = 256
GLA_GATE_RANK = 16
GLA_GATE_TEMP = 16.0
HGRN_HEADS = 8
HGRN_KEY_DIM = 128
HGRN_VAL_DIM = 128
LIN_CHUNK = 64
D_FF = 2816
RMS_EPS = 1e-6
N_MIXERS = 4

LANES = 128
VMEM_LIMIT = 56 * 1024 * 1024
FFN_CHUNK = 256
ROW_TILE = 512
LIN_ROWS = 256

ALIBI_SLOPES = tuple(2.0 ** (-8.0 * (i + 1) / N_Q_HEADS) for i in range(N_Q_HEADS))
NEG_INF = float("-inf")


def _params(sem):
    return pltpu.CompilerParams(dimension_semantics=sem, vmem_limit_bytes=VMEM_LIMIT)


def _dot(a, b):
    return jnp.dot(a, b, preferred_element_type=F32)


def _dot_nt(a, b):
    return lax.dot_general(a, b, (((1,), (1,)), ((), ())), preferred_element_type=F32)


def _dot_tn(a, b):
    return lax.dot_general(a, b, (((0,), (0,)), ((), ())), preferred_element_type=F32)


def _split(x):
    hi = x.astype(BF16)
    lo = (x - hi.astype(F32)).astype(BF16)
    return hi, lo


def _dot_split(a, b):
    ah, al = _split(a)
    bh, bl = _split(b)
    return _dot(ah, bh) + (_dot(ah, bl) + _dot(al, bh))


def _rms(x, gain):
    return x * lax.rsqrt(jnp.mean(x * x, axis=-1, keepdims=True) + RMS_EPS) * gain


def _sigmoid(x):
    return 1.0 / (1.0 + jnp.exp(-x))


def _silu(x):
    return x * _sigmoid(x)


def _norm_proj_kernel(x_ref, g_ref, w_ref, o_ref, h_ref):
    @pl.when(pl.program_id(1) == 0)
    def _():
        h_ref[...] = _rms(x_ref[...], g_ref[...]).astype(BF16)

    o_ref[...] = _dot(h_ref[...], w_ref[...])


def _norm_proj(x, gain, w, tn):
    n, d = x.shape
    dout = w.shape[1]
    return pl.pallas_call(
        _norm_proj_kernel,
        out_shape=jax.ShapeDtypeStruct((n, dout), F32),
        grid=(n // ROW_TILE, dout // tn),
        in_specs=[
            pl.BlockSpec((ROW_TILE, d), lambda i, j: (i, 0)),
            pl.BlockSpec((1, d), lambda i, j: (0, 0)),
            pl.BlockSpec((d, tn), lambda i, j: (0, j)),
        ],
        out_specs=pl.BlockSpec((ROW_TILE, tn), lambda i, j: (i, j)),
        scratch_shapes=[pltpu.VMEM((ROW_TILE, d), BF16)],
        compiler_params=_params(("parallel", "arbitrary")),
        name="norm_proj",
    )(x, gain.reshape(1, d), w)


def _ffn_kernel(x_ref, o_ref, wo_ref, g_ref, wgu_ref, wd_ref, fg_ref, out_ref,
                x1_ref, h_ref, acc_ref, *, final_norm):
    x1 = x_ref[...] + _dot(o_ref[...].astype(BF16), wo_ref[...])
    x1_ref[...] = x1
    h_ref[...] = _rms(x1, g_ref[...]).astype(BF16)
    for c in range(D_FF // FFN_CHUNK):
        h = h_ref[...]
        gate = _dot(h, wgu_ref[:, c * FFN_CHUNK:(c + 1) * FFN_CHUNK])
        up = _dot(h, wgu_ref[:, D_FF + c * FFN_CHUNK:D_FF + (c + 1) * FFN_CHUNK])
        act = (_silu(gate) * up).astype(BF16)
        part = _dot(act, wd_ref[c * FFN_CHUNK:(c + 1) * FFN_CHUNK, :])
        if c == 0:
            acc_ref[...] = part
        else:
            acc_ref[...] += part
    y = x1_ref[...] + acc_ref[...]
    if final_norm:
        y = _rms(y, fg_ref[...])
    out_ref[...] = y


def _ffn(x, o, wo, gain, wgu, wd, final_gain, final_norm):
    n, d = x.shape
    const = lambda i: (0, 0)
    row = lambda i: (i, 0)
    return pl.pallas_call(
        functools.partial(_ffn_kernel, final_norm=final_norm),
        out_shape=jax.ShapeDtypeStruct((n, d), F32),
        grid=(n // ROW_TILE,),
        in_specs=[
            pl.BlockSpec((ROW_TILE, d), row),
            pl.BlockSpec((ROW_TILE, d), row),
            pl.BlockSpec((d, d), const, pipeline_mode=pl.Buffered(1)),
            pl.BlockSpec((1, d), const),
            pl.BlockSpec((d, 2 * D_FF), const, pipeline_mode=pl.Buffered(1)),
            pl.BlockSpec((D_FF, d), const, pipeline_mode=pl.Buffered(1)),
            pl.BlockSpec((1, d), const),
        ],
        out_specs=pl.BlockSpec((ROW_TILE, d), row),
        scratch_shapes=[
            pltpu.VMEM((ROW_TILE, d), F32),
            pltpu.VMEM((ROW_TILE, d), BF16),
            pltpu.VMEM((ROW_TILE, d), F32),
        ],
        compiler_params=_params(("parallel",)),
        name="outproj_ffn",
    )(x, o, wo, gain.reshape(1, d), wgu, wd, final_gain.reshape(1, d))


def _swa_kernel(q_ref, kp_ref, ko_ref, vp_ref, vo_ref, sink_ref, o_ref):
    L = SWA_BLOCK
    n = pl.program_id(1)
    k = jnp.concatenate([kp_ref[...], ko_ref[...]], axis=0).astype(BF16)
    v = jnp.concatenate([vp_ref[...], vo_ref[...]], axis=0).astype(BF16)
    qi = lax.broadcasted_iota(jnp.int32, (L, 2 * L), 0)
    kj = lax.broadcasted_iota(jnp.int32, (L, 2 * L), 1)
    dist_i = (L + qi) - kj
    dist = dist_i.astype(F32)
    first_key = jnp.where(n > 0, 0, L)
    mask = (dist_i >= 0) & (dist_i < L) & (kj >= first_key)
    for hd in range(N_Q_HEADS):
        kv = hd // GQA_GROUP
        ks = k[:, kv * HEAD_DIM:(kv + 1) * HEAD_DIM]
        vs = v[:, kv * HEAD_DIM:(kv + 1) * HEAD_DIM]
        q = (q_ref[:, hd * HEAD_DIM:(hd + 1) * HEAD_DIM] * (HEAD_DIM ** -0.5)).astype(BF16)
        s = _dot_nt(q, ks) - ALIBI_SLOPES[hd] * dist
        s = jnp.where(mask, s, NEG_INF)
        sink = sink_ref[:, hd:hd + 1]
        m = jnp.maximum(jnp.max(s, axis=-1, keepdims=True), sink)
        p = jnp.exp(s - m)
        denom = jnp.sum(p, axis=-1, keepdims=True) + jnp.exp(sink - m)
        o_ref[:, hd * HEAD_DIM:(hd + 1) * HEAD_DIM] = _dot(p.astype(BF16), vs) / denom


def _swa_core(proj, sinks, batch, seq):
    n = proj.shape[0]
    nb = seq // SWA_BLOCK
    kcol = ATTN_Q_DIM // ATTN_KV_DIM
    own = lambda c: (lambda b, i: (b * nb + i, c))
    prev = lambda c: (lambda b, i: (b * nb + jnp.maximum(i - 1, 0), c))
    return pl.pallas_call(
        _swa_kernel,
        out_shape=jax.ShapeDtypeStruct((n, ATTN_Q_DIM), F32),
        grid=(batch, nb),
        in_specs=[
            pl.BlockSpec((SWA_BLOCK, ATTN_Q_DIM), own(0)),
            pl.BlockSpec((SWA_BLOCK, ATTN_KV_DIM), prev(kcol)),
            pl.BlockSpec((SWA_BLOCK, ATTN_KV_DIM), own(kcol)),
            pl.BlockSpec((SWA_BLOCK, ATTN_KV_DIM), prev(kcol + 1)),
            pl.BlockSpec((SWA_BLOCK, ATTN_KV_DIM), own(kcol + 1)),
            pl.BlockSpec((1, N_Q_HEADS), lambda b, i: (0, 0)),
        ],
        out_specs=pl.BlockSpec((SWA_BLOCK, ATTN_Q_DIM), own(0)),
        compiler_params=_params(("parallel", "parallel")),
        name="swa_core",
    )(proj, proj, proj, proj, proj, sinks.reshape(1, N_Q_HEADS))


MOBA_SLOT = 2 * HEAD_DIM
MOBA_Q_WIDTH = N_Q_HEADS * MOBA_SLOT
MOBA_K_WIDTH = N_KV_HEADS * MOBA_SLOT
MOBA_WIDTH = MOBA_Q_WIDTH + MOBA_K_WIDTH + ATTN_KV_DIM
MASKED = -1e30

_KMEAN_BLOCKS = 8


def _kmean_kernel(k_ref, o_ref):
    k = k_ref[...].reshape(_KMEAN_BLOCKS, MOBA_BLOCK, MOBA_K_WIDTH)
    o_ref[...] = jnp.sum(k, axis=1) * (1.0 / MOBA_BLOCK)


def _moba_kmean(proj):
    n = proj.shape[0]
    rows = _KMEAN_BLOCKS * MOBA_BLOCK
    return pl.pallas_call(
        _kmean_kernel,
        out_shape=jax.ShapeDtypeStruct((n // MOBA_BLOCK, MOBA_K_WIDTH), F32),
        grid=(n // rows,),
        in_specs=[pl.BlockSpec((rows, MOBA_K_WIDTH), lambda i: (i, MOBA_Q_WIDTH // MOBA_K_WIDTH))],
        out_specs=pl.BlockSpec((_KMEAN_BLOCKS, MOBA_K_WIDTH), lambda i: (i, 0)),
        compiler_params=_params(("parallel",)),
        name="moba_kmean",
    )(proj)


def _moba_select_kernel(q_ref, km_ref, expand_ref, v_ref, qa_ref, vt_ref, *, nblk):
    n = pl.program_id(1)
    q = q_ref[...] * (HEAD_DIM ** -0.5)
    gate = _dot_split(q, km_ref[0])
    lane_i = lax.broadcasted_iota(jnp.int32, gate.shape, 1)
    blk = jnp.bitwise_and(lane_i, nblk - 1)
    head = jnp.right_shift(lane_i, int(math.log2(nblk)))
    lane = lane_i.astype(F32)
    gate = jnp.where(blk < n, gate, NEG_INF)
    sel = jnp.zeros(gate.shape, F32)
    for h in range(N_KV_HEADS):
        g = jnp.where(head == h, gate, NEG_INF)
        for _ in range(MOBA_TOP_K):
            mx = jnp.max(g, axis=-1, keepdims=True)
            is_max = (g == mx) & (mx > NEG_INF)
            first = jnp.min(jnp.where(is_max, lane, float(4 * LANES)), axis=-1, keepdims=True)
            pick = lane == first
            sel = jnp.where(pick, 1.0, sel)
            g = jnp.where(pick, NEG_INF, g)
    penalty = _dot((1.0 - sel).astype(BF16), expand_ref[...])
    qa_ref[...] = (q + penalty).astype(BF16)
    vt_ref[...] = v_ref[...].T.astype(BF16)


def _moba_select(proj, km_mat, expand, batch, seq):
    n = proj.shape[0]
    nblk = seq // MOBA_BLOCK
    width = N_KV_HEADS * nblk
    vcol = (MOBA_Q_WIDTH + MOBA_K_WIDTH) // ATTN_KV_DIM
    return pl.pallas_call(
        functools.partial(_moba_select_kernel, nblk=nblk),
        out_shape=(jax.ShapeDtypeStruct((n, MOBA_Q_WIDTH), BF16),
                   jax.ShapeDtypeStruct((ATTN_KV_DIM, n), BF16)),
        grid=(batch, nblk),
        in_specs=[
            pl.BlockSpec((MOBA_BLOCK, MOBA_Q_WIDTH), lambda b, i: (b * nblk + i, 0)),
            pl.BlockSpec((1, MOBA_Q_WIDTH, width), lambda b, i: (b, 0, 0)),
            pl.BlockSpec((width, MOBA_Q_WIDTH), lambda b, i: (0, 0)),
            pl.BlockSpec((MOBA_BLOCK, ATTN_KV_DIM), lambda b, i: (b * nblk + i, vcol)),
        ],
        out_specs=(pl.BlockSpec((MOBA_BLOCK, MOBA_Q_WIDTH), lambda b, i: (b * nblk + i, 0)),
                   pl.BlockSpec((ATTN_KV_DIM, MOBA_BLOCK), lambda b, i: (0, b * nblk + i))),
        compiler_params=_params(("parallel", "parallel")),
        name="moba_select",
    )(proj, km_mat, expand, proj)


def _moba_scores_update(ka, vt, qa_ref, tab_ref, m_ref, l_ref, acc_ref, kv, offset_blocks, causal):
    for g in range(GQA_GROUP):
        hd = kv * GQA_GROUP + g
        s = _dot_nt(ka, qa_ref[:, hd * MOBA_SLOT:(hd + 1) * MOBA_SLOT]) + tab_ref[hd]
        if causal is not None:
            s = jnp.where(causal, s, MASKED)
        off = ALIBI_SLOPES[hd] * MOBA_BLOCK * offset_blocks
        m_old = m_ref[hd:hd + 1, :]
        m_new = jnp.maximum(m_old, jnp.max(s, axis=0, keepdims=True) + off)
        alpha = jnp.exp(m_old - m_new)
        p = jnp.exp(s - (m_new - off))
        l_ref[hd:hd + 1, :] = alpha * l_ref[hd:hd + 1, :] + jnp.sum(p, axis=0, keepdims=True)
        rows = slice(hd * HEAD_DIM, (hd + 1) * HEAD_DIM)
        acc_ref[rows, :] = alpha * acc_ref[rows, :] + _dot(vt, p.astype(BF16))
        m_ref[hd:hd + 1, :] = m_new


def _moba_attn_kernel(nq_ref, jk_ref, qa_ref, k_ref, vt_ref, tab_ref, o_ref, m_ref, l_ref, acc_ref):
    Lb = MOBA_BLOCK
    step = pl.program_id(1)
    n = nq_ref[step]
    j = jk_ref[step]

    @pl.when(j == 0)
    def _():
        m_ref[...] = jnp.full(m_ref.shape, NEG_INF, F32)
        l_ref[...] = jnp.zeros(l_ref.shape, F32)
        acc_ref[...] = jnp.zeros(acc_ref.shape, F32)

    @pl.when(j < n)
    def _():
        lane = lax.broadcasted_iota(jnp.int32, (Lb, MOBA_SLOT), 1)
        onehot = jnp.where(lane == HEAD_DIM + j, 1.0, 0.0)
        offset_blocks = (j - n).astype(F32)
        for kv in range(N_KV_HEADS):
            ka = (k_ref[:, kv * MOBA_SLOT:(kv + 1) * MOBA_SLOT] + onehot).astype(BF16)
            vt = vt_ref[kv * HEAD_DIM:(kv + 1) * HEAD_DIM, :]
            _moba_scores_update(ka, vt, qa_ref, tab_ref, m_ref, l_ref, acc_ref, kv, offset_blocks, None)

    @pl.when(j == n)
    def _():
        key = lax.broadcasted_iota(jnp.int32, (Lb, Lb), 0)
        qry = lax.broadcasted_iota(jnp.int32, (Lb, Lb), 1)
        causal = key <= qry
        for kv in range(N_KV_HEADS):
            ka = k_ref[:, kv * MOBA_SLOT:(kv + 1) * MOBA_SLOT].astype(BF16)
            vt = vt_ref[kv * HEAD_DIM:(kv + 1) * HEAD_DIM, :]
            _moba_scores_update(ka, vt, qa_ref, tab_ref, m_ref, l_ref, acc_ref, kv, 0.0, causal)
        inv = 1.0 / l_ref[...]
        acc = acc_ref[...].reshape(N_Q_HEADS, HEAD_DIM, Lb) * inv[:, None, :]
        o_ref[...] = acc.reshape(N_Q_HEADS * HEAD_DIM, Lb).T


def _moba_attn(proj, qaug, vt, batch, seq):
    n = proj.shape[0]
    nblk = seq // MOBA_BLOCK
    pairs = [(i, j) for i in range(nblk) for j in range(i + 1)]
    nq = jnp.asarray([p[0] for p in pairs], jnp.int32)
    jk = jnp.asarray([p[1] for p in pairs], jnp.int32)
    key_off = jnp.arange(MOBA_BLOCK, dtype=F32)[None, :, None]
    tab = jnp.broadcast_to(jnp.asarray(ALIBI_SLOPES, F32)[:, None, None] * key_off,
                           (N_Q_HEADS, MOBA_BLOCK, MOBA_BLOCK))
    kcol = MOBA_Q_WIDTH // MOBA_K_WIDTH
    qmap = lambda b, s, nq_r, jk_r: (b * nblk + nq_r[s], 0)
    grid_spec = pltpu.PrefetchScalarGridSpec(
        num_scalar_prefetch=2,
        grid=(batch, len(pairs)),
        in_specs=[
            pl.BlockSpec((MOBA_BLOCK, MOBA_Q_WIDTH), qmap),
            pl.BlockSpec((MOBA_BLOCK, MOBA_K_WIDTH), lambda b, s, nq_r, jk_r: (b * nblk + jk_r[s], kcol)),
            pl.BlockSpec((ATTN_KV_DIM, MOBA_BLOCK), lambda b, s, nq_r, jk_r: (0, b * nblk + jk_r[s])),
            pl.BlockSpec((N_Q_HEADS, MOBA_BLOCK, MOBA_BLOCK), lambda b, s, nq_r, jk_r: (0, 0, 0),
                         pipeline_mode=pl.Buffered(1)),
        ],
        out_specs=pl.BlockSpec((MOBA_BLOCK, ATTN_Q_DIM), qmap),
        scratch_shapes=[
            pltpu.VMEM((N_Q_HEADS, MOBA_BLOCK), F32),
            pltpu.VMEM((N_Q_HEADS, MOBA_BLOCK), F32),
            pltpu.VMEM((N_Q_HEADS * HEAD_DIM, MOBA_BLOCK), F32),
        ],
    )
    return pl.pallas_call(
        _moba_attn_kernel,
        out_shape=jax.ShapeDtypeStruct((n, ATTN_Q_DIM), F32),
        grid_spec=grid_spec,
        compiler_params=_params(("parallel", "arbitrary")),
        name="moba_attn",
    )(nq, jk, qaug, proj, vt, tab)


def _moba_slot_weights(w_in):
    d = w_in.shape[0]
    fill = MOBA_SLOT - HEAD_DIM
    wq = w_in[:, :ATTN_Q_DIM].reshape(d, N_Q_HEADS, HEAD_DIM)
    wk = w_in[:, ATTN_Q_DIM:ATTN_Q_DIM + ATTN_KV_DIM].reshape(d, N_KV_HEADS, HEAD_DIM)
    wq = jnp.pad(wq, ((0, 0), (0, 0), (0, fill))).reshape(d, MOBA_Q_WIDTH)
    wk = jnp.pad(wk, ((0, 0), (0, 0), (0, fill))).reshape(d, MOBA_K_WIDTH)
    return jnp.concatenate([wq, wk, w_in[:, ATTN_Q_DIM + ATTN_KV_DIM:]], axis=1)


def _moba_core(proj, batch, seq):
    nblk = seq // MOBA_BLOCK
    assert nblk <= HEAD_DIM, "one free query lane per key block"
    width = N_KV_HEADS * nblk
    kmean = _moba_kmean(proj)
    km = kmean.reshape(batch, nblk, N_KV_HEADS, MOBA_SLOT).transpose(0, 2, 3, 1)
    eye = jnp.eye(N_KV_HEADS, dtype=F32)
    km_mat = jnp.einsum('bhej,hH->bheHj', km, eye)
    km_mat = jnp.broadcast_to(km_mat[:, :, None], (batch, N_KV_HEADS, GQA_GROUP, MOBA_SLOT, N_KV_HEADS, nblk))
    km_mat = km_mat.reshape(batch, MOBA_Q_WIDTH, width)
    h_of = jnp.arange(width) // nblk
    c_of = jnp.arange(width) % nblk
    slot = jnp.arange(MOBA_Q_WIDTH) // MOBA_SLOT
    lane = jnp.arange(MOBA_Q_WIDTH) % MOBA_SLOT
    hit = (h_of[:, None] == (slot // GQA_GROUP)[None, :]) & (lane[None, :] == HEAD_DIM + c_of[:, None])
    expand = jnp.where(hit, MASKED, 0.0).astype(BF16)
    qaug, vt = _moba_select(proj, km_mat, expand, batch, seq)
    return _moba_attn(proj, qaug, vt, batch, seq)


def _cumsum_rows(tri, x):
    hi = x.astype(BF16)
    r1 = x - hi.astype(F32)
    mid = r1.astype(BF16)
    lo = (r1 - mid.astype(F32)).astype(BF16)
    return _dot(tri, hi) + (_dot(tri, mid) + _dot(tri, lo))


def _lin_chunk(q, k, v, log_g, st_ref, h, tri, causal):
    b = _cumsum_rows(tri, log_g)
    b_last = b[LIN_CHUNK - 1:LIN_CHUNK, :]
    q_dec = (q * jnp.exp(b)).astype(BF16)
    k_dec = (k * jnp.exp(-b)).astype(BF16)
    attn = jnp.where(causal, _dot_nt(q_dec, k_dec), 0.0).astype(BF16)
    vb = v.astype(BF16)
    st = st_ref[h]
    o = _dot(attn, vb) + _dot_nt(q_dec, st.astype(BF16))
    k_tail = (k * jnp.exp(b_last - b)).astype(BF16)
    st_ref[h] = st * jnp.exp(b_last) + _dot_tn(vb, k_tail)
    return o


def _head_norm_gate(o, gain, g):
    o = o * lax.rsqrt(jnp.mean(o * o, axis=-1, keepdims=True) + RMS_EPS) * gain
    return o * _silu(g)


def _tri_causal():
    r = lax.broadcasted_iota(jnp.int32, (LIN_CHUNK, LIN_CHUNK), 0)
    c = lax.broadcasted_iota(jnp.int32, (LIN_CHUNK, LIN_CHUNK), 1)
    causal = r >= c
    return causal.astype(BF16), causal


def _gla_kernel(q_ref, k_ref, v_ref, g_ref, a_ref, wd_ref, bd_ref, gain_ref, o_ref, st_ref):
    @pl.when(pl.program_id(1) == 0)
    def _():
        st_ref[...] = jnp.zeros(st_ref.shape, F32)

    tri, causal = _tri_causal()
    K, V = GLA_KEY_DIM, GLA_VAL_DIM

    def body(c, carry):
        rows = pl.ds(pl.multiple_of(c * LIN_CHUNK, LIN_CHUNK), LIN_CHUNK)
        z = _dot_split(a_ref[rows, :], wd_ref[...]) + bd_ref[...]
        log_alpha = (jnp.minimum(z, 0.0) - jnp.log(1.0 + jnp.exp(-jnp.abs(z)))) * (1.0 / GLA_GATE_TEMP)
        for h in range(GLA_HEADS):
            q = q_ref[rows, h * K:(h + 1) * K] * (K ** -0.5)
            k = k_ref[rows, h * K:(h + 1) * K]
            v = v_ref[rows, h * V:(h + 1) * V]
            o = _lin_chunk(q, k, v, log_alpha[:, h * K:(h + 1) * K], st_ref, h, tri, causal)
            o_ref[rows, h * V:(h + 1) * V] = _head_norm_gate(o, gain_ref[...], g_ref[rows, h * V:(h + 1) * V])
        return carry

    lax.fori_loop(0, LIN_ROWS // LIN_CHUNK, body, 0)


def _gla_core(proj, wd_pad, bd, gain, batch, seq):
    n = proj.shape[0]
    nt = seq // LIN_ROWS
    dk = GLA_HEADS * GLA_KEY_DIM
    dv = GLA_HEADS * GLA_VAL_DIM
    rows = lambda c: (lambda b, t: (b * nt + t, c))
    const = lambda b, t: (0, 0)
    return pl.pallas_call(
        _gla_kernel,
        out_shape=jax.ShapeDtypeStruct((n, dv), F32),
        grid=(batch, nt),
        in_specs=[
            pl.BlockSpec((LIN_ROWS, dk), rows(0)),
            pl.BlockSpec((LIN_ROWS, dk), rows(1)),
            pl.BlockSpec((LIN_ROWS, dv), rows(1)),
            pl.BlockSpec((LIN_ROWS, dv), rows(2)),
            pl.BlockSpec((LIN_ROWS, LANES), rows((2 * dk + 2 * dv) // LANES)),
            pl.BlockSpec((LANES, dk), const),
            pl.BlockSpec((1, dk), const),
            pl.BlockSpec((1, GLA_VAL_DIM), const),
        ],
        out_specs=pl.BlockSpec((LIN_ROWS, dv), rows(0)),
        scratch_shapes=[pltpu.VMEM((GLA_HEADS, GLA_VAL_DIM, GLA_KEY_DIM), F32)],
        compiler_params=_params(("parallel", "arbitrary")),
        name="gla_core",
    )(proj, proj, proj, proj, proj, wd_pad, bd.reshape(1, dk), gain.reshape(1, GLA_VAL_DIM))


def _hgrn_kernel(q_ref, f_ref, i_ref, g_ref, lbl_ref, gain_ref, o_ref, st_ref, *, layer):
    @pl.when(pl.program_id(1) == 0)
    def _():
        st_ref[...] = jnp.zeros(st_ref.shape, F32)

    tri, causal = _tri_causal()
    K, V = HGRN_KEY_DIM, HGRN_VAL_DIM
    logits = lbl_ref[...]
    e = jnp.exp(logits - jnp.max(logits, axis=0, keepdims=True))
    p = e / jnp.sum(e, axis=0, keepdims=True)
    lb = jnp.zeros((1, logits.shape[1]), F32)
    for l in range(1, layer + 1):
        lb = lb + p[l:l + 1, :]

    def body(c, carry):
        rows = pl.ds(pl.multiple_of(c * LIN_CHUNK, LIN_CHUNK), LIN_CHUNK)
        for h in range(HGRN_HEADS):
            cols = slice(h * K, (h + 1) * K)
            q = _silu(q_ref[rows, cols]) * (K ** -0.5)
            lbh = lb[:, cols]
            f = lbh + (1.0 - lbh) * _sigmoid(f_ref[rows, cols])
            o = _lin_chunk(q, 1.0 - f, i_ref[rows, cols], jnp.log(f), st_ref, h, tri, causal)
            o_ref[rows, cols] = _head_norm_gate(o, gain_ref[...], g_ref[rows, cols])
        return carry

    lax.fori_loop(0, LIN_ROWS // LIN_CHUNK, body, 0)


def _hgrn_core(proj, lb_logits, gain, layer, batch, seq):
    n = proj.shape[0]
    nt = seq // LIN_ROWS
    d = D_MODEL
    rows = lambda c: (lambda b, t: (b * nt + t, c))
    const = lambda b, t: (0, 0)
    return pl.pallas_call(
        functools.partial(_hgrn_kernel, layer=layer),
        out_shape=jax.ShapeDtypeStruct((n, d), F32),
        grid=(batch, nt),
        in_specs=[
            pl.BlockSpec((LIN_ROWS, d), rows(0)),
            pl.BlockSpec((LIN_ROWS, d), rows(1)),
            pl.BlockSpec((LIN_ROWS, d), rows(2)),
            pl.BlockSpec((LIN_ROWS, d), rows(3)),
            pl.BlockSpec(lb_logits.shape, const),
            pl.BlockSpec((1, HGRN_VAL_DIM), const),
        ],
        out_specs=pl.BlockSpec((LIN_ROWS, d), rows(0)),
        scratch_shapes=[pltpu.VMEM((HGRN_HEADS, HGRN_VAL_DIM, HGRN_KEY_DIM), F32)],
        compiler_params=_params(("parallel", "arbitrary")),
        name="hgrn_core",
    )(proj, proj, proj, proj, lb_logits, gain.reshape(1, HGRN_VAL_DIM))


def kernel(x, norm_mix, norm_ffn, swa_w_in, swa_sinks, swa_w_out, moba_w_in, moba_w_out, gla_w_in, gla_w_decay_up, gla_b_decay, gla_out_norm, gla_w_out, hgrn_w_in, hgrn_lb_logits, hgrn_out_norm, hgrn_w_out, ffn_w_gate_up, ffn_w_down, final_norm):
    batch, seq, d = x.shape
    depth = norm_mix.shape[0]
    xf = x.reshape(batch * seq, d)
    for i in range(depth):
        kind, j = i % N_MIXERS, i // N_MIXERS
        if kind == 0:
            proj = _norm_proj(xf, norm_mix[i], swa_w_in[j].astype(BF16), 512)
            o = _swa_core(proj, swa_sinks[j], batch, seq)
            w_out = swa_w_out[j]
        elif kind == 1:
            proj = _norm_proj(xf, norm_mix[i], _moba_slot_weights(moba_w_in[j]).astype(BF16), MOBA_WIDTH // 2)
            o = _moba_core(proj, batch, seq)
            w_out = moba_w_out[j]
        elif kind == 2:
            pad = LANES - GLA_GATE_RANK
            w_in = jnp.pad(gla_w_in[j], ((0, 0), (0, pad))).astype(BF16)
            wd_pad = jnp.pad(gla_w_decay_up[j], ((0, pad), (0, 0)))
            proj = _norm_proj(xf, norm_mix[i], w_in, 640)
            o = _gla_core(proj, wd_pad, gla_b_decay[j], gla_out_norm[j], batch, seq)
            w_out = gla_w_out[j]
        else:
            proj = _norm_proj(xf, norm_mix[i], hgrn_w_in[j].astype(BF16), 1024)
            o = _hgrn_core(proj, hgrn_lb_logits, hgrn_out_norm[j], i, batch, seq)
            w_out = hgrn_w_out[j]
        last = i == depth - 1
        xf = _ffn(xf, o, w_out.astype(BF16), norm_ffn[i], ffn_w_gate_up[i].astype(BF16),
                  ffn_w_down[i].astype(BF16), final_norm, last)
    return xf.reshape(batch, seq, d)
```

```python
import functools
import math

import jax
import jax.numpy as jnp
from jax import lax
from jax.experimental import pallas as pl
from jax.experimental.pallas import tpu as pltpu

F32 = jnp.float32
BF16 = jnp.bfloat16

D_MODEL = 1024
HEAD_DIM = 64
N_Q_HEADS = 16
N_KV_HEADS = 4
GQA_GROUP = 4
ATTN_Q_DIM = 1024
ATTN_KV_DIM = 256
SWA_BLOCK = 128
MOBA_BLOCK = 256
MOBA_TOP_K = 3
GLA_HEADS = 4
GLA_KEY_DIM = 128
GLA_VAL_DIM = 256
GLA_GATE_RANK = 16
GLA_GATE_TEMP = 16.0
HGRN_HEADS = 8
HGRN_KEY_DIM = 128
HGRN_VAL_DIM = 128
LIN_CHUNK = 64
D_FF = 2816
RMS_EPS = 1e-6
N_MIXERS = 4

LANES = 128
VMEM_LIMIT = 56 * 1024 * 1024
FFN_CHUNK = 256
ROW_TILE = 512
LIN_ROWS = 256

ALIBI_SLOPES = tuple(2.0 ** (-8.0 * (i + 1) / N_Q_HEADS) for i in range(N_Q_HEADS))
NEG_INF = float("-inf")


def _params(sem):
    return pltpu.CompilerParams(dimension_semantics=sem, vmem_limit_bytes=VMEM_LIMIT)


def _dot(a, b):
    return jnp.dot(a, b, preferred_element_type=F32)


def _dot_nt(a, b):
    return lax.dot_general(a, b, (((1,), (1,)), ((), ())), preferred_element_type=F32)


def _dot_tn(a, b):
    return lax.dot_general(a, b, (((0,), (0,)), ((), ())), preferred_element_type=F32)


def _split(x):
    hi = x.astype(BF16)
    lo = (x - hi.astype(F32)).astype(BF16)
    return hi, lo


def _dot_split(a, b):
    ah, al = _split(a)
    bh, bl = _split(b)
    return _dot(ah, bh) + (_dot(ah, bl) + _dot(al, bh))


def _rms(x, gain):
    return x * lax.rsqrt(jnp.mean(x * x, axis=-1, keepdims=True) + RMS_EPS) * gain


def _sigmoid(x):
    return 1.0 / (1.0 + jnp.exp(-x))


def _silu(x):
    return x * _sigmoid(x)


def _norm_proj_kernel(x_ref, g_ref, w_ref, o_ref, h_ref):
    @pl.when(pl.program_id(1) == 0)
    def _():
        h_ref[...] = _rms(x_ref[...], g_ref[...]).astype(BF16)

    o_ref[...] = _dot(h_ref[...], w_ref[...])


def _norm_proj(x, gain, w, tn):
    n, d = x.shape
    dout = w.shape[1]
    return pl.pallas_call(
        _norm_proj_kernel,
        out_shape=jax.ShapeDtypeStruct((n, dout), F32),
        grid=(n // ROW_TILE, dout // tn),
        in_specs=[
            pl.BlockSpec((ROW_TILE, d), lambda i, j: (i, 0)),
            pl.BlockSpec((1, d), lambda i, j: (0, 0)),
            pl.BlockSpec((d, tn), lambda i, j: (0, j)),
        ],
        out_specs=pl.BlockSpec((ROW_TILE, tn), lambda i, j: (i, j)),
        scratch_shapes=[pltpu.VMEM((ROW_TILE, d), BF16)],
        compiler_params=_params(("parallel", "arbitrary")),
        name="norm_proj",
    )(x, gain.reshape(1, d), w)


def _ffn_kernel(x_ref, o_ref, wo_ref, g_ref, wgu_ref, wd_ref, fg_ref, out_ref,
                x1_ref, h_ref, acc_ref, *, final_norm):
    x1 = x_ref[...] + _dot(o_ref[...].astype(BF16), wo_ref[...])
    x1_ref[...] = x1
    h_ref[...] = _rms(x1, g_ref[...]).astype(BF16)
    for c in range(D_FF // FFN_CHUNK):
        h = h_ref[...]
        gate = _dot(h, wgu_ref[:, c * FFN_CHUNK:(c + 1) * FFN_CHUNK])
        up = _dot(h, wgu_ref[:, D_FF + c * FFN_CHUNK:D_FF + (c + 1) * FFN_CHUNK])
        act = (_silu(gate) * up).astype(BF16)
        part = _dot(act, wd_ref[c * FFN_CHUNK:(c + 1) * FFN_CHUNK, :])
        if c == 0:
            acc_ref[...] = part
        else:
            acc_ref[...] += part
    y = x1_ref[...] + acc_ref[...]
    if final_norm:
        y = _rms(y, fg_ref[...])
    out_ref[...] = y


def _ffn(x, o, wo, gain, wgu, wd, final_gain, final_norm):
    n, d = x.shape
    const = lambda i: (0, 0)
    row = lambda i: (i, 0)
    return pl.pallas_call(
        functools.partial(_ffn_kernel, final_norm=final_norm),
        out_shape=jax.ShapeDtypeStruct((n, d), F32),
        grid=(n // ROW_TILE,),
        in_specs=[
            pl.BlockSpec((ROW_TILE, d), row),
            pl.BlockSpec((ROW_TILE, d), row),
            pl.BlockSpec((d, d), const, pipeline_mode=pl.Buffered(1)),
            pl.BlockSpec((1, d), const),
            pl.BlockSpec((d, 2 * D_FF), const, pipeline_mode=pl.Buffered(1)),
            pl.BlockSpec((D_FF, d), const, pipeline_mode=pl.Buffered(1)),
            pl.BlockSpec((1, d), const),
        ],
        out_specs=pl.BlockSpec((ROW_TILE, d), row),
        scratch_shapes=[
            pltpu.VMEM((ROW_TILE, d), F32),
            pltpu.VMEM((ROW_TILE, d), BF16),
            pltpu.VMEM((ROW_TILE, d), F32),
        ],
        compiler_params=_params(("parallel",)),
        name="outproj_ffn",
    )(x, o, wo, gain.reshape(1, d), wgu, wd, final_gain.reshape(1, d))


def _swa_kernel(q_ref, kp_ref, ko_ref, vp_ref, vo_ref, sink_ref, o_ref):
    L = SWA_BLOCK
    n = pl.program_id(1)
    k = jnp.concatenate([kp_ref[...], ko_ref[...]], axis=0).astype(BF16)
    v = jnp.concatenate([vp_ref[...], vo_ref[...]], axis=0).astype(BF16)
    qi = lax.broadcasted_iota(jnp.int32, (L, 2 * L), 0)
    kj = lax.broadcasted_iota(jnp.int32, (L, 2 * L), 1)
    dist_i = (L + qi) - kj
    dist = dist_i.astype(F32)
    first_key = jnp.where(n > 0, 0, L)
    mask = (dist_i >= 0) & (dist_i < L) & (kj >= first_key)
    for hd in range(N_Q_HEADS):
        kv = hd // GQA_GROUP
        ks = k[:, kv * HEAD_DIM:(kv + 1) * HEAD_DIM]
        vs = v[:, kv * HEAD_DIM:(kv + 1) * HEAD_DIM]
        q = (q_ref[:, hd * HEAD_DIM:(hd + 1) * HEAD_DIM] * (HEAD_DIM ** -0.5)).astype(BF16)
        s = _dot_nt(q, ks) - ALIBI_SLOPES[hd] * dist
        s = jnp.where(mask, s, NEG_INF)
        sink = sink_ref[:, hd:hd + 1]
        m = jnp.maximum(jnp.max(s, axis=-1, keepdims=True), sink)
        p = jnp.exp(s - m)
        denom = jnp.sum(p, axis=-1, keepdims=True) + jnp.exp(sink - m)
        o_ref[:, hd * HEAD_DIM:(hd + 1) * HEAD_DIM] = _dot(p.astype(BF16), vs) / denom


def _swa_core(proj, sinks, batch, seq):
    n = proj.shape[0]
    nb = seq // SWA_BLOCK
    kcol = ATTN_Q_DIM // ATTN_KV_DIM
    own = lambda c: (lambda b, i: (b * nb + i, c))
    prev = lambda c: (lambda b, i: (b * nb + jnp.maximum(i - 1, 0), c))
    return pl.pallas_call(
        _swa_kernel,
        out_shape=jax.ShapeDtypeStruct((n, ATTN_Q_DIM), F32),
        grid=(batch, nb),
        in_specs=[
            pl.BlockSpec((SWA_BLOCK, ATTN_Q_DIM), own(0)),
            pl.BlockSpec((SWA_BLOCK, ATTN_KV_DIM), prev(kcol)),
            pl.BlockSpec((SWA_BLOCK, ATTN_KV_DIM), own(kcol)),
            pl.BlockSpec((SWA_BLOCK, ATTN_KV_DIM), prev(kcol + 1)),
            pl.BlockSpec((SWA_BLOCK, ATTN_KV_DIM), own(kcol + 1)),
            pl.BlockSpec((1, N_Q_HEADS), lambda b, i: (0, 0)),
        ],
        out_specs=pl.BlockSpec((SWA_BLOCK, ATTN_Q_DIM), own(0)),
        compiler_params=_params(("parallel", "parallel")),
        name="swa_core",
    )(proj, proj, proj, proj, proj, sinks.reshape(1, N_Q_HEADS))


MOBA_SLOT = 2 * HEAD_DIM
MOBA_Q_WIDTH = N_Q_HEADS * MOBA_SLOT
MOBA_K_WIDTH = N_KV_HEADS * MOBA_SLOT
MOBA_WIDTH = MOBA_Q_WIDTH + MOBA_K_WIDTH + ATTN_KV_DIM
MASKED = -1e30
LOG2E = math.log2(math.e)
SUM_ROWS = 16

_KMEAN_BLOCKS = 8


def _kmean_kernel(k_ref, o_ref):
    k = k_ref[...].reshape(_KMEAN_BLOCKS, MOBA_BLOCK, MOBA_K_WIDTH)
    o_ref[...] = jnp.sum(k, axis=1) * (1.0 / MOBA_BLOCK)


def _moba_kmean(proj):
    n = proj.shape[0]
    rows = _KMEAN_BLOCKS * MOBA_BLOCK
    return pl.pallas_call(
        _kmean_kernel,
        out_shape=jax.ShapeDtypeStruct((n // MOBA_BLOCK, MOBA_K_WIDTH), F32),
        grid=(n // rows,),
        in_specs=[pl.BlockSpec((rows, MOBA_K_WIDTH), lambda i: (i, MOBA_Q_WIDTH // MOBA_K_WIDTH))],
        out_specs=pl.BlockSpec((_KMEAN_BLOCKS, MOBA_K_WIDTH), lambda i: (i, 0)),
        compiler_params=_params(("parallel",)),
        name="moba_kmean",
    )(proj)


def _moba_select_kernel(q_ref, km_ref, expand_ref, v_ref, qa_ref, vt_ref, *, nblk):
    n = pl.program_id(1)
    q = q_ref[...] * (HEAD_DIM ** -0.5)
    gate = _dot_split(q, km_ref[0])
    lane_i = lax.broadcasted_iota(jnp.int32, gate.shape, 1)
    blk = jnp.bitwise_and(lane_i, nblk - 1)
    head = jnp.right_shift(lane_i, int(math.log2(nblk)))
    lane = lane_i.astype(F32)
    gate = jnp.where(blk < n, gate, NEG_INF)
    sel = jnp.zeros(gate.shape, F32)
    for h in range(N_KV_HEADS):
        g = jnp.where(head == h, gate, NEG_INF)
        for _ in range(MOBA_TOP_K):
            mx = jnp.max(g, axis=-1, keepdims=True)
            is_max = (g == mx) & (mx > NEG_INF)
            first = jnp.min(jnp.where(is_max, lane, float(4 * LANES)), axis=-1, keepdims=True)
            pick = lane == first
            sel = jnp.where(pick, 1.0, sel)
            g = jnp.where(pick, NEG_INF, g)
    penalty = _dot((1.0 - sel).astype(BF16), expand_ref[...])
    qa_ref[...] = (q * LOG2E + penalty).T.astype(BF16)
    vt_ref[...] = v_ref[...].T.astype(BF16)


def _moba_select(proj, km_mat, expand, batch, seq):
    n = proj.shape[0]
    nblk = seq // MOBA_BLOCK
    width = N_KV_HEADS * nblk
    vcol = (MOBA_Q_WIDTH + MOBA_K_WIDTH) // ATTN_KV_DIM
    return pl.pallas_call(
        functools.partial(_moba_select_kernel, nblk=nblk),
        out_shape=(jax.ShapeDtypeStruct((MOBA_Q_WIDTH, n), BF16),
                   jax.ShapeDtypeStruct((ATTN_KV_DIM, n), BF16)),
        grid=(batch, nblk),
        in_specs=[
            pl.BlockSpec((MOBA_BLOCK, MOBA_Q_WIDTH), lambda b, i: (b * nblk + i, 0)),
            pl.BlockSpec((1, MOBA_Q_WIDTH, width), lambda b, i: (b, 0, 0)),
            pl.BlockSpec((width, MOBA_Q_WIDTH), lambda b, i: (0, 0)),
            pl.BlockSpec((MOBA_BLOCK, ATTN_KV_DIM), lambda b, i: (b * nblk + i, vcol)),
        ],
        out_specs=(pl.BlockSpec((MOBA_Q_WIDTH, MOBA_BLOCK), lambda b, i: (0, b * nblk + i)),
                   pl.BlockSpec((ATTN_KV_DIM, MOBA_BLOCK), lambda b, i: (0, b * nblk + i))),
        compiler_params=_params(("parallel", "parallel")),
        name="moba_select",
    )(proj, km_mat, expand, proj)


def _moba_scores_update(ka, vt, qa_ref, tab_ref, m_ref, l_ref, acc_ref, kv, offset_blocks, causal):
    Lb = MOBA_BLOCK
    heads = [kv * GQA_GROUP + g for g in range(GQA_GROUP)]
    qt = jnp.concatenate([qa_ref[hd * MOBA_SLOT:(hd + 1) * MOBA_SLOT, :] for hd in heads], axis=1)
    s_all = _dot(ka, qt)
    probs, alphas = [], []
    for g, hd in enumerate(heads):
        off = (LOG2E * ALIBI_SLOPES[hd] * MOBA_BLOCK) * offset_blocks
        s = s_all[:, g * Lb:(g + 1) * Lb] + tab_ref[hd]
        if causal is not None:
            s = jnp.where(causal, s, MASKED)
        m_old = m_ref[hd:hd + 1, :]
        m_new = jnp.maximum(m_old, jnp.max(s, axis=0, keepdims=True) + off)
        alphas.append(jnp.exp2(m_old - m_new))
        probs.append(jnp.exp2(s - (m_new - off)).astype(BF16))
        m_ref[hd:hd + 1, :] = m_new
    pv_all = _dot(vt, jnp.concatenate(probs, axis=1))
    for g, hd in enumerate(heads):
        pv = pv_all[:, g * Lb:(g + 1) * Lb]
        rows = slice(hd * HEAD_DIM, (hd + 1) * HEAD_DIM)
        l_ref[hd:hd + 1, :] = alphas[g] * l_ref[hd:hd + 1, :] + pv[HEAD_DIM:HEAD_DIM + 1, :]
        acc_ref[rows, :] = alphas[g] * acc_ref[rows, :] + pv[:HEAD_DIM, :]


def _moba_attn_kernel(nq_ref, jk_ref, qa_ref, k_ref, vt_ref, tab_ref, o_ref, m_ref, l_ref, acc_ref):
    Lb = MOBA_BLOCK
    step = pl.program_id(1)
    n = nq_ref[step]
    j = jk_ref[step]

    @pl.when(j == 0)
    def _():
        m_ref[...] = jnp.full(m_ref.shape, NEG_INF, F32)
        l_ref[...] = jnp.zeros(l_ref.shape, F32)
        acc_ref[...] = jnp.zeros(acc_ref.shape, F32)

    ones = jnp.ones((SUM_ROWS, Lb), BF16)

    @pl.when(j < n)
    def _():
        lane = lax.broadcasted_iota(jnp.int32, (Lb, MOBA_SLOT), 1)
        onehot = jnp.where(lane == HEAD_DIM + j, 1.0, 0.0)
        offset_blocks = (j - n).astype(F32)
        for kv in range(N_KV_HEADS):
            ka = (k_ref[:, kv * MOBA_SLOT:(kv + 1) * MOBA_SLOT] + onehot).astype(BF16)
            vt = jnp.concatenate([vt_ref[kv * HEAD_DIM:(kv + 1) * HEAD_DIM, :], ones], axis=0)
            _moba_scores_update(ka, vt, qa_ref, tab_ref, m_ref, l_ref, acc_ref, kv, offset_blocks, None)

    @pl.when(j == n)
    def _():
        key = lax.broadcasted_iota(jnp.int32, (Lb, Lb), 0)
        qry = lax.broadcasted_iota(jnp.int32, (Lb, Lb), 1)
        causal = key <= qry
        for kv in range(N_KV_HEADS):
            ka = k_ref[:, kv * MOBA_SLOT:(kv + 1) * MOBA_SLOT].astype(BF16)
            vt = jnp.concatenate([vt_ref[kv * HEAD_DIM:(kv + 1) * HEAD_DIM, :], ones], axis=0)
            _moba_scores_update(ka, vt, qa_ref, tab_ref, m_ref, l_ref, acc_ref, kv, 0.0, causal)
        inv = 1.0 / l_ref[...]
        acc = acc_ref[...].reshape(N_Q_HEADS, HEAD_DIM, Lb) * inv[:, None, :]
        o_ref[...] = acc.reshape(N_Q_HEADS * HEAD_DIM, Lb).T


def _moba_attn(proj, qaug, vt, batch, seq):
    n = proj.shape[0]
    nblk = seq // MOBA_BLOCK
    pairs = [(i, j) for i in range(nblk) for j in range(i + 1)]
    nq = jnp.asarray([p[0] for p in pairs], jnp.int32)
    jk = jnp.asarray([p[1] for p in pairs], jnp.int32)
    key_off = jnp.arange(MOBA_BLOCK, dtype=F32)[None, :, None]
    tab = jnp.broadcast_to((LOG2E * jnp.asarray(ALIBI_SLOPES, F32))[:, None, None] * key_off,
                           (N_Q_HEADS, MOBA_BLOCK, MOBA_BLOCK))
    kcol = MOBA_Q_WIDTH // MOBA_K_WIDTH
    qmap = lambda b, s, nq_r, jk_r: (b * nblk + nq_r[s], 0)
    grid_spec = pltpu.PrefetchScalarGridSpec(
        num_scalar_prefetch=2,
        grid=(batch, len(pairs)),
        in_specs=[
            pl.BlockSpec((MOBA_Q_WIDTH, MOBA_BLOCK), lambda b, s, nq_r, jk_r: (0, b * nblk + nq_r[s])),
            pl.BlockSpec((MOBA_BLOCK, MOBA_K_WIDTH), lambda b, s, nq_r, jk_r: (b * nblk + jk_r[s], kcol)),
            pl.BlockSpec((ATTN_KV_DIM, MOBA_BLOCK), lambda b, s, nq_r, jk_r: (0, b * nblk + jk_r[s])),
            pl.BlockSpec((N_Q_HEADS, MOBA_BLOCK, MOBA_BLOCK), lambda b, s, nq_r, jk_r: (0, 0, 0),
                         pipeline_mode=pl.Buffered(1)),
        ],
        out_specs=pl.BlockSpec((MOBA_BLOCK, ATTN_Q_DIM), qmap),
        scratch_shapes=[
            pltpu.VMEM((N_Q_HEADS, MOBA_BLOCK), F32),
            pltpu.VMEM((N_Q_HEADS, MOBA_BLOCK), F32),
            pltpu.VMEM((N_Q_HEADS * HEAD_DIM, MOBA_BLOCK), F32),
        ],
    )
    return pl.pallas_call(
        _moba_attn_kernel,
        out_shape=jax.ShapeDtypeStruct((n, ATTN_Q_DIM), F32),
        grid_spec=grid_spec,
        compiler_params=_params(("parallel", "arbitrary")),
        name="moba_attn",
    )(nq, jk, qaug, proj, vt, tab)


def _moba_slot_weights(w_in):
    d = w_in.shape[0]
    fill = MOBA_SLOT - HEAD_DIM
    wq = w_in[:, :ATTN_Q_DIM].reshape(d, N_Q_HEADS, HEAD_DIM)
    wk = w_in[:, ATTN_Q_DIM:ATTN_Q_DIM + ATTN_KV_DIM].reshape(d, N_KV_HEADS, HEAD_DIM)
    wq = jnp.pad(wq, ((0, 0), (0, 0), (0, fill))).reshape(d, MOBA_Q_WIDTH)
    wk = jnp.pad(wk, ((0, 0), (0, 0), (0, fill))).reshape(d, MOBA_K_WIDTH)
    return jnp.concatenate([wq, wk, w_in[:, ATTN_Q_DIM + ATTN_KV_DIM:]], axis=1)


def _moba_core(proj, batch, seq):
    nblk = seq // MOBA_BLOCK
    assert nblk <= HEAD_DIM, "one free query lane per key block"
    width = N_KV_HEADS * nblk
    kmean = _moba_kmean(proj)
    km = kmean.reshape(batch, nblk, N_KV_HEADS, MOBA_SLOT).transpose(0, 2, 3, 1)
    eye = jnp.eye(N_KV_HEADS, dtype=F32)
    km_mat = jnp.einsum('bhej,hH->bheHj', km, eye)
    km_mat = jnp.broadcast_to(km_mat[:, :, None], (batch, N_KV_HEADS, GQA_GROUP, MOBA_SLOT, N_KV_HEADS, nblk))
    km_mat = km_mat.reshape(batch, MOBA_Q_WIDTH, width)
    h_of = jnp.arange(width) // nblk
    c_of = jnp.arange(width) % nblk
    slot = jnp.arange(MOBA_Q_WIDTH) // MOBA_SLOT
    lane = jnp.arange(MOBA_Q_WIDTH) % MOBA_SLOT
    hit = (h_of[:, None] == (slot // GQA_GROUP)[None, :]) & (lane[None, :] == HEAD_DIM + c_of[:, None])
    expand = jnp.where(hit, MASKED, 0.0).astype(BF16)
    qaug, vt = _moba_select(proj, km_mat, expand, batch, seq)
    return _moba_attn(proj, qaug, vt, batch, seq)


def _cumsum_rows(tri, x):
    hi = x.astype(BF16)
    r1 = x - hi.astype(F32)
    mid = r1.astype(BF16)
    lo = (r1 - mid.astype(F32)).astype(BF16)
    return _dot(tri, hi) + (_dot(tri, mid) + _dot(tri, lo))


def _lin_chunk(q, k, v, log_g, st_ref, h, tri, causal):
    b = _cumsum_rows(tri, log_g)
    b_last = b[LIN_CHUNK - 1:LIN_CHUNK, :]
    q_dec = (q * jnp.exp(b)).astype(BF16)
    k_dec = (k * jnp.exp(-b)).astype(BF16)
    attn = jnp.where(causal, _dot_nt(q_dec, k_dec), 0.0).astype(BF16)
    vb = v.astype(BF16)
    st = st_ref[h]
    o = _dot(attn, vb) + _dot_nt(q_dec, st.astype(BF16))
    k_tail = (k * jnp.exp(b_last - b)).astype(BF16)
    st_ref[h] = st * jnp.exp(b_last) + _dot_tn(vb, k_tail)
    return o


def _head_norm_gate(o, gain, g):
    o = o * lax.rsqrt(jnp.mean(o * o, axis=-1, keepdims=True) + RMS_EPS) * gain
    return o * _silu(g)


def _tri_causal():
    r = lax.broadcasted_iota(jnp.int32, (LIN_CHUNK, LIN_CHUNK), 0)
    c = lax.broadcasted_iota(jnp.int32, (LIN_CHUNK, LIN_CHUNK), 1)
    causal = r >= c
    return causal.astype(BF16), causal


def _gla_kernel(q_ref, k_ref, v_ref, g_ref, a_ref, wd_ref, bd_ref, gain_ref, o_ref, st_ref):
    @pl.when(pl.program_id(1) == 0)
    def _():
        st_ref[...] = jnp.zeros(st_ref.shape, F32)

    tri, causal = _tri_causal()
    K, V = GLA_KEY_DIM, GLA_VAL_DIM

    def body(c, carry):
        rows = pl.ds(pl.multiple_of(c * LIN_CHUNK, LIN_CHUNK), LIN_CHUNK)
        z = _dot_split(a_ref[rows, :], wd_ref[...]) + bd_ref[...]
        log_alpha = (jnp.minimum(z, 0.0) - jnp.log(1.0 + jnp.exp(-jnp.abs(z)))) * (1.0 / GLA_GATE_TEMP)
        for h in range(GLA_HEADS):
            q = q_ref[rows, h * K:(h + 1) * K] * (K ** -0.5)
            k = k_ref[rows, h * K:(h + 1) * K]
            v = v_ref[rows, h * V:(h + 1) * V]
            o = _lin_chunk(q, k, v, log_alpha[:, h * K:(h + 1) * K], st_ref, h, tri, causal)
            o_ref[rows, h * V:(h + 1) * V] = _head_norm_gate(o, gain_ref[...], g_ref[rows, h * V:(h + 1) * V])
        return carry

    lax.fori_loop(0, LIN_ROWS // LIN_CHUNK, body, 0)


def _gla_core(proj, wd_pad, bd, gain, batch, seq):
    n = proj.shape[0]
    nt = seq // LIN_ROWS
    dk = GLA_HEADS * GLA_KEY_DIM
    dv = GLA_HEADS * GLA_VAL_DIM
    rows = lambda c: (lambda b, t: (b * nt + t, c))
    const = lambda b, t: (0, 0)
    return pl.pallas_call(
        _gla_kernel,
        out_shape=jax.ShapeDtypeStruct((n, dv), F32),
        grid=(batch, nt),
        in_specs=[
            pl.BlockSpec((LIN_ROWS, dk), rows(0)),
            pl.BlockSpec((LIN_ROWS, dk), rows(1)),
            pl.BlockSpec((LIN_ROWS, dv), rows(1)),
            pl.BlockSpec((LIN_ROWS, dv), rows(2)),
            pl.BlockSpec((LIN_ROWS, LANES), rows((2 * dk + 2 * dv) // LANES)),
            pl.BlockSpec((LANES, dk), const),
            pl.BlockSpec((1, dk), const),
            pl.BlockSpec((1, GLA_VAL_DIM), const),
        ],
        out_specs=pl.BlockSpec((LIN_ROWS, dv), rows(0)),
        scratch_shapes=[pltpu.VMEM((GLA_HEADS, GLA_VAL_DIM, GLA_KEY_DIM), F32)],
        compiler_params=_params(("parallel", "arbitrary")),
        name="gla_core",
    )(proj, proj, proj, proj, proj, wd_pad, bd.reshape(1, dk), gain.reshape(1, GLA_VAL_DIM))


def _hgrn_kernel(q_ref, f_ref, i_ref, g_ref, lbl_ref, gain_ref, o_ref, st_ref, *, layer):
    @pl.when(pl.program_id(1) == 0)
    def _():
        st_ref[...] = jnp.zeros(st_ref.shape, F32)

    tri, causal = _tri_causal()
    K, V = HGRN_KEY_DIM, HGRN_VAL_DIM
    logits = lbl_ref[...]
    e = jnp.exp(logits - jnp.max(logits, axis=0, keepdims=True))
    p = e / jnp.sum(e, axis=0, keepdims=True)
    lb = jnp.zeros((1, logits.shape[1]), F32)
    for l in range(1, layer + 1):
        lb = lb + p[l:l + 1, :]

    def body(c, carry):
        rows = pl.ds(pl.multiple_of(c * LIN_CHUNK, LIN_CHUNK), LIN_CHUNK)
        for h in range(HGRN_HEADS):
            cols = slice(h * K, (h + 1) * K)
            q = _silu(q_ref[rows, cols]) * (K ** -0.5)
            lbh = lb[:, cols]
            f = lbh + (1.0 - lbh) * _sigmoid(f_ref[rows, cols])
            o = _lin_chunk(q, 1.0 - f, i_ref[rows, cols], jnp.log(f), st_ref, h, tri, causal)
            o_ref[rows, cols] = _head_norm_gate(o, gain_ref[...], g_ref[rows, cols])
        return carry

    lax.fori_loop(0, LIN_ROWS // LIN_CHUNK, body, 0)


def _hgrn_core(proj, lb_logits, gain, layer, batch, seq):
    n = proj.shape[0]
    nt = seq // LIN_ROWS
    d = D_MODEL
    rows = lambda c: (lambda b, t: (b * nt + t, c))
    const = lambda b, t: (0, 0)
    return pl.pallas_call(
        functools.partial(_hgrn_kernel, layer=layer),
        out_shape=jax.ShapeDtypeStruct((n, d), F32),
        grid=(batch, nt),
        in_specs=[
            pl.BlockSpec((LIN_ROWS, d), rows(0)),
            pl.BlockSpec((LIN_ROWS, d), rows(1)),
            pl.BlockSpec((LIN_ROWS, d), rows(2)),
            pl.BlockSpec((LIN_ROWS, d), rows(3)),
            pl.BlockSpec(lb_logits.shape, const),
            pl.BlockSpec((1, HGRN_VAL_DIM), const),
        ],
        out_specs=pl.BlockSpec((LIN_ROWS, d), rows(0)),
        scratch_shapes=[pltpu.VMEM((HGRN_HEADS, HGRN_VAL_DIM, HGRN_KEY_DIM), F32)],
        compiler_params=_params(("parallel", "arbitrary")),
        name="hgrn_core",
    )(proj, proj, proj, proj, lb_logits, gain.reshape(1, HGRN_VAL_DIM))


def kernel(x, norm_mix, norm_ffn, swa_w_in, swa_sinks, swa_w_out, moba_w_in, moba_w_out, gla_w_in, gla_w_decay_up, gla_b_decay, gla_out_norm, gla_w_out, hgrn_w_in, hgrn_lb_logits, hgrn_out_norm, hgrn_w_out, ffn_w_gate_up, ffn_w_down, final_norm):
    batch, seq, d = x.shape
    depth = norm_mix.shape[0]
    xf = x.reshape(batch * seq, d)
    for i in range(depth):
        kind, j = i % N_MIXERS, i // N_MIXERS
        if kind == 0:
            proj = _norm_proj(xf, norm_mix[i], swa_w_in[j].astype(BF16), 512)
            o = _swa_core(proj, swa_sinks[j], batch, seq)
            w_out = swa_w_out[j]
        elif kind == 1:
            proj = _norm_proj(xf, norm_mix[i], _moba_slot_weights(moba_w_in[j]).astype(BF16), MOBA_WIDTH // 2)
            o = _moba_core(proj, batch, seq)
            w_out = moba_w_out[j]
        elif kind == 2:
            pad = LANES - GLA_GATE_RANK
            w_in = jnp.pad(gla_w_in[j], ((0, 0), (0, pad))).astype(BF16)
            wd_pad = jnp.pad(gla_w_decay_up[j], ((0, pad), (0, 0)))
            proj = _norm_proj(xf, norm_mix[i], w_in, 640)
            o = _gla_core(proj, wd_pad, gla_b_decay[j], gla_out_norm[j], batch, seq)
            w_out = gla_w_out[j]
        else:
            proj = _norm_proj(xf, norm_mix[i], hgrn_w_in[j].astype(BF16), 1024)
            o = _hgrn_core(proj, hgrn_lb_logits, hgrn_out_norm[j], i, batch, seq)
            w_out = hgrn_w_out[j]
        last = i == depth - 1
        xf = _ffn(xf, o, w_out.astype(BF16), norm_ffn[i], ffn_w_gate_up[i].astype(BF16),
                  ffn_w_down[i].astype(BF16), final_norm, last)
    return xf.reshape(batch, seq, d)
```

```python
import functools
import math

import jax
import jax.numpy as jnp
from jax import lax
from jax.experimental import pallas as pl
from jax.experimental.pallas import tpu as pltpu

F32 = jnp.float32
BF16 = jnp.bfloat16

D_MODEL = 1024
HEAD_DIM = 64
N_Q_HEADS = 16
N_KV_HEADS = 4
GQA_GROUP = 4
ATTN_Q_DIM = 1024
ATTN_KV_DIM = 256
SWA_BLOCK = 128
SWA_WINDOW = 128
MOBA_BLOCK = 256
MOBA_TOP_K = 3
GLA_HEADS = 4
GLA_KEY_DIM = 128
GLA_VAL_DIM = 256
GLA_GATE_RANK = 16
GLA_GATE_TEMP = 16.0
HGRN_HEADS = 8
HGRN_KEY_DIM = 128
HGRN_VAL_DIM = 128
LIN_CHUNK = 64
D_FF = 2816
RMS_EPS = 1e-6
N_MIXERS = 4

LANES = 128
VMEM_LIMIT = 56 * 1024 * 1024
FFN_CHUNK = 256
ROW_TILE = 512
LIN_ROWS = 256

ALIBI_SLOPES = tuple(2.0 ** (-8.0 * (i + 1) / N_Q_HEADS) for i in range(N_Q_HEADS))
NEG_INF = float("-inf")


def _params(sem):
    return pltpu.CompilerParams(dimension_semantics=sem, vmem_limit_bytes=VMEM_LIMIT)


def _dot(a, b):
    return jnp.dot(a, b, preferred_element_type=F32)


def _dot_nt(a, b):
    return lax.dot_general(a, b, (((1,), (1,)), ((), ())), preferred_element_type=F32)


def _dot_tn(a, b):
    return lax.dot_general(a, b, (((0,), (0,)), ((), ())), preferred_element_type=F32)


def _split(x):
    hi = x.astype(BF16)
    lo = (x - hi.astype(F32)).astype(BF16)
    return hi, lo


def _dot_split(a, b):
    ah, al = _split(a)
    bh, bl = _split(b)
    return _dot(ah, bh) + (_dot(ah, bl) + _dot(al, bh))


def _rms(x, gain):
    return x * lax.rsqrt(jnp.mean(x * x, axis=-1, keepdims=True) + RMS_EPS) * gain


def _sigmoid(x):
    return 1.0 / (1.0 + jnp.exp(-x))


def _silu(x):
    return x * _sigmoid(x)


def _norm_proj_kernel(x_ref, g_ref, w_ref, o_ref, h_ref):
    @pl.when(pl.program_id(1) == 0)
    def _():
        h_ref[...] = _rms(x_ref[...], g_ref[...]).astype(BF16)

    o_ref[...] = _dot(h_ref[...], w_ref[...])


def _norm_proj(x, gain, w, tn):
    n, d = x.shape
    dout = w.shape[1]
    return pl.pallas_call(
        _norm_proj_kernel,
        out_shape=jax.ShapeDtypeStruct((n, dout), F32),
        grid=(n // ROW_TILE, dout // tn),
        in_specs=[
            pl.BlockSpec((ROW_TILE, d), lambda i, j: (i, 0)),
            pl.BlockSpec((1, d), lambda i, j: (0, 0)),
            pl.BlockSpec((d, tn), lambda i, j: (0, j)),
        ],
        out_specs=pl.BlockSpec((ROW_TILE, tn), lambda i, j: (i, j)),
        scratch_shapes=[pltpu.VMEM((ROW_TILE, d), BF16)],
        compiler_params=_params(("parallel", "arbitrary")),
        name="norm_proj",
    )(x, gain.reshape(1, d), w)


def _ffn_kernel(x_ref, o_ref, wo_ref, g_ref, wgu_ref, wd_ref, fg_ref, out_ref,
                x1_ref, h_ref, acc_ref, *, final_norm):
    x1 = x_ref[...] + _dot(o_ref[...].astype(BF16), wo_ref[...])
    x1_ref[...] = x1
    h_ref[...] = _rms(x1, g_ref[...]).astype(BF16)
    for c in range(D_FF // FFN_CHUNK):
        h = h_ref[...]
        gate = _dot(h, wgu_ref[:, c * FFN_CHUNK:(c + 1) * FFN_CHUNK])
        up = _dot(h, wgu_ref[:, D_FF + c * FFN_CHUNK:D_FF + (c + 1) * FFN_CHUNK])
        act = (_silu(gate) * up).astype(BF16)
        part = _dot(act, wd_ref[c * FFN_CHUNK:(c + 1) * FFN_CHUNK, :])
        if c == 0:
            acc_ref[...] = part
        else:
            acc_ref[...] += part
    y = x1_ref[...] + acc_ref[...]
    if final_norm:
        y = _rms(y, fg_ref[...])
    out_ref[...] = y


def _ffn(x, o, wo, gain, wgu, wd, final_gain, final_norm):
    n, d = x.shape
    const = lambda i: (0, 0)
    row = lambda i: (i, 0)
    return pl.pallas_call(
        functools.partial(_ffn_kernel, final_norm=final_norm),
        out_shape=jax.ShapeDtypeStruct((n, d), F32),
        grid=(n // ROW_TILE,),
        in_specs=[
            pl.BlockSpec((ROW_TILE, d), row),
            pl.BlockSpec((ROW_TILE, d), row),
            pl.BlockSpec((d, d), const, pipeline_mode=pl.Buffered(1)),
            pl.BlockSpec((1, d), const),
            pl.BlockSpec((d, 2 * D_FF), const, pipeline_mode=pl.Buffered(1)),
            pl.BlockSpec((D_FF, d), const, pipeline_mode=pl.Buffered(1)),
            pl.BlockSpec((1, d), const),
        ],
        out_specs=pl.BlockSpec((ROW_TILE, d), row),
        scratch_shapes=[
            pltpu.VMEM((ROW_TILE, d), F32),
            pltpu.VMEM((ROW_TILE, d), BF16),
            pltpu.VMEM((ROW_TILE, d), F32),
        ],
        compiler_params=_params(("parallel",)),
        name="outproj_ffn",
    )(x, o, wo, gain.reshape(1, d), wgu, wd, final_gain.reshape(1, d))


def _swa_kernel(q_ref, kp_ref, ko_ref, vp_ref, vo_ref, sink_ref, tab_ref, o_ref, ot_ref):
    L = SWA_BLOCK
    n = pl.program_id(1)
    first = jnp.where(n == 0, 1, 0)
    qt = (q_ref[...] * (LOG2E * HEAD_DIM ** -0.5)).T.astype(BF16)
    k = jnp.concatenate([kp_ref[...], ko_ref[...]], axis=0).astype(BF16)
    vt = jnp.concatenate([vp_ref[...], vo_ref[...]], axis=0).T.astype(BF16)
    ones = jnp.ones((SUM_ROWS, 2 * L), BF16)
    sinks = sink_ref[...] * LOG2E
    for kv in range(N_KV_HEADS):
        heads = [kv * GQA_GROUP + g for g in range(GQA_GROUP)]
        qg = jnp.concatenate([qt[hd * MOBA_SLOT:(hd + 1) * MOBA_SLOT, :] for hd in heads], axis=1)
        s_all = _dot(k[:, kv * MOBA_SLOT:(kv + 1) * MOBA_SLOT], qg)
        probs, extras = [], []
        for g, hd in enumerate(heads):
            s = s_all[:, g * L:(g + 1) * L] + tab_ref[first, hd]
            sink = sinks[:, hd:hd + 1]
            m = jnp.maximum(jnp.max(s, axis=0, keepdims=True), sink)
            probs.append(jnp.exp2(s - m).astype(BF16))
            extras.append(jnp.exp2(sink - m))
        vaug = jnp.concatenate([vt[kv * HEAD_DIM:(kv + 1) * HEAD_DIM, :], ones], axis=0)
        pv_all = _dot(vaug, jnp.concatenate(probs, axis=1))
        for g, hd in enumerate(heads):
            pv = pv_all[:, g * L:(g + 1) * L]
            denom = pv[HEAD_DIM:HEAD_DIM + 1, :] + extras[g]
            ot_ref[hd * HEAD_DIM:(hd + 1) * HEAD_DIM, :] = pv[:HEAD_DIM, :] / denom
    o_ref[...] = ot_ref[...].T


def _swa_core(proj, sinks, batch, seq):
    n = proj.shape[0]
    L = SWA_BLOCK
    nb = seq // L
    kcol = MOBA_Q_WIDTH // MOBA_K_WIDTH
    vcol = (MOBA_Q_WIDTH + MOBA_K_WIDTH) // ATTN_KV_DIM
    own = lambda c: (lambda b, i: (b * nb + i, c))
    prev = lambda c: (lambda b, i: (b * nb + jnp.maximum(i - 1, 0), c))
    key = jnp.arange(2 * L)[:, None]
    qry = jnp.arange(L)[None, :]
    dist = (L + qry) - key
    band = (dist >= 0) & (dist < SWA_WINDOW)
    slopes = (LOG2E * jnp.asarray(ALIBI_SLOPES, F32))[:, None, None]
    bias = -slopes * dist.astype(F32)[None]
    tab = jnp.stack([jnp.where(band[None], bias, MASKED),
                     jnp.where((band & (key >= L))[None], bias, MASKED)])
    return pl.pallas_call(
        _swa_kernel,
        out_shape=jax.ShapeDtypeStruct((n, ATTN_Q_DIM), F32),
        grid=(batch, nb),
        in_specs=[
            pl.BlockSpec((L, MOBA_Q_WIDTH), own(0)),
            pl.BlockSpec((L, MOBA_K_WIDTH), prev(kcol)),
            pl.BlockSpec((L, MOBA_K_WIDTH), own(kcol)),
            pl.BlockSpec((L, ATTN_KV_DIM), prev(vcol)),
            pl.BlockSpec((L, ATTN_KV_DIM), own(vcol)),
            pl.BlockSpec((1, N_Q_HEADS), lambda b, i: (0, 0)),
            pl.BlockSpec((2, N_Q_HEADS, 2 * L, L), lambda b, i: (0, 0, 0, 0), pipeline_mode=pl.Buffered(1)),
        ],
        out_specs=pl.BlockSpec((L, ATTN_Q_DIM), own(0)),
        scratch_shapes=[pltpu.VMEM((ATTN_Q_DIM, L), F32)],
        compiler_params=_params(("parallel", "parallel")),
        name="swa_core",
    )(proj, proj, proj, proj, proj, sinks.reshape(1, N_Q_HEADS), tab)


MOBA_SLOT = 2 * HEAD_DIM
MOBA_Q_WIDTH = N_Q_HEADS * MOBA_SLOT
MOBA_K_WIDTH = N_KV_HEADS * MOBA_SLOT
MOBA_WIDTH = MOBA_Q_WIDTH + MOBA_K_WIDTH + ATTN_KV_DIM
MASKED = -1e30
LOG2E = math.log2(math.e)
SUM_ROWS = 16
MOBA_KEY_GROUP = 8

_KMEAN_BLOCKS = 8


def _kmean_kernel(k_ref, o_ref):
    k = k_ref[...].reshape(_KMEAN_BLOCKS, MOBA_BLOCK, MOBA_K_WIDTH)
    o_ref[...] = jnp.sum(k, axis=1) * (1.0 / MOBA_BLOCK)


def _moba_kmean(proj):
    n = proj.shape[0]
    rows = _KMEAN_BLOCKS * MOBA_BLOCK
    return pl.pallas_call(
        _kmean_kernel,
        out_shape=jax.ShapeDtypeStruct((n // MOBA_BLOCK, MOBA_K_WIDTH), F32),
        grid=(n // rows,),
        in_specs=[pl.BlockSpec((rows, MOBA_K_WIDTH), lambda i: (i, MOBA_Q_WIDTH // MOBA_K_WIDTH))],
        out_specs=pl.BlockSpec((_KMEAN_BLOCKS, MOBA_K_WIDTH), lambda i: (i, 0)),
        compiler_params=_params(("parallel",)),
        name="moba_kmean",
    )(proj)


def _moba_select_kernel(q_ref, km_ref, expand_ref, k_ref, v_ref, qa_ref, ks_ref, vt_ref, *, nblk):
    n = pl.program_id(1)
    q = q_ref[...] * (HEAD_DIM ** -0.5)
    gate = _dot_split(q, km_ref[0])
    lane_i = lax.broadcasted_iota(jnp.int32, gate.shape, 1)
    blk = jnp.bitwise_and(lane_i, nblk - 1)
    head = jnp.right_shift(lane_i, int(math.log2(nblk)))
    lane = lane_i.astype(F32)
    gate = jnp.where(blk < n, gate, NEG_INF)
    sel = jnp.zeros(gate.shape, F32)
    for h in range(N_KV_HEADS):
        g = jnp.where(head == h, gate, NEG_INF)
        for _ in range(MOBA_TOP_K):
            mx = jnp.max(g, axis=-1, keepdims=True)
            is_max = (g == mx) & (mx > NEG_INF)
            first = jnp.min(jnp.where(is_max, lane, float(4 * LANES)), axis=-1, keepdims=True)
            pick = lane == first
            sel = jnp.where(pick, 1.0, sel)
            g = jnp.where(pick, NEG_INF, g)
    penalty = _dot((1.0 - sel).astype(BF16), expand_ref[...])
    qa_ref[...] = (q * LOG2E + penalty).T.astype(BF16)
    ks_ref[0] = k_ref[...].astype(BF16)
    vt_ref[0] = v_ref[...].T.astype(BF16)


def _moba_select(proj, km_mat, expand, batch, seq):
    n = proj.shape[0]
    nblk = seq // MOBA_BLOCK
    width = N_KV_HEADS * nblk
    vcol = (MOBA_Q_WIDTH + MOBA_K_WIDTH) // ATTN_KV_DIM
    return pl.pallas_call(
        functools.partial(_moba_select_kernel, nblk=nblk),
        out_shape=(jax.ShapeDtypeStruct((MOBA_Q_WIDTH, n), BF16),
                   jax.ShapeDtypeStruct((n // MOBA_BLOCK, MOBA_BLOCK, MOBA_K_WIDTH), BF16),
                   jax.ShapeDtypeStruct((n // MOBA_BLOCK, ATTN_KV_DIM, MOBA_BLOCK), BF16)),
        grid=(batch, nblk),
        in_specs=[
            pl.BlockSpec((MOBA_BLOCK, MOBA_Q_WIDTH), lambda b, i: (b * nblk + i, 0)),
            pl.BlockSpec((1, MOBA_Q_WIDTH, width), lambda b, i: (b, 0, 0)),
            pl.BlockSpec((width, MOBA_Q_WIDTH), lambda b, i: (0, 0)),
            pl.BlockSpec((MOBA_BLOCK, MOBA_K_WIDTH), lambda b, i: (b * nblk + i, MOBA_Q_WIDTH // MOBA_K_WIDTH)),
            pl.BlockSpec((MOBA_BLOCK, ATTN_KV_DIM), lambda b, i: (b * nblk + i, vcol)),
        ],
        out_specs=(pl.BlockSpec((MOBA_Q_WIDTH, MOBA_BLOCK), lambda b, i: (0, b * nblk + i)),
                   pl.BlockSpec((1, MOBA_BLOCK, MOBA_K_WIDTH), lambda b, i: (b * nblk + i, 0, 0)),
                   pl.BlockSpec((1, ATTN_KV_DIM, MOBA_BLOCK), lambda b, i: (b * nblk + i, 0, 0))),
        compiler_params=_params(("parallel", "parallel")),
        name="moba_select",
    )(proj, km_mat, expand, proj, proj)


def _moba_scores_update(ka, vt, qa_ref, tab_ref, m_ref, l_ref, acc_ref, kv, offset_blocks, causal):
    Lb = MOBA_BLOCK
    heads = [kv * GQA_GROUP + g for g in range(GQA_GROUP)]
    qt = jnp.concatenate([qa_ref[hd * MOBA_SLOT:(hd + 1) * MOBA_SLOT, :] for hd in heads], axis=1)
    s_all = _dot(ka, qt)
    probs, alphas = [], []
    for g, hd in enumerate(heads):
        off = (LOG2E * ALIBI_SLOPES[hd] * MOBA_BLOCK) * offset_blocks
        s = s_all[:, g * Lb:(g + 1) * Lb] + tab_ref[hd]
        if causal is not None:
            s = jnp.where(causal, s, MASKED)
        m_old = m_ref[hd:hd + 1, :]
        m_new = jnp.maximum(m_old, jnp.max(s, axis=0, keepdims=True) + off)
        alphas.append(jnp.exp2(m_old - m_new))
        probs.append(jnp.exp2(s - (m_new - off)).astype(BF16))
        m_ref[hd:hd + 1, :] = m_new
    pv_all = _dot(vt, jnp.concatenate(probs, axis=1))
    for g, hd in enumerate(heads):
        pv = pv_all[:, g * Lb:(g + 1) * Lb]
        rows = slice(hd * HEAD_DIM, (hd + 1) * HEAD_DIM)
        l_ref[hd:hd + 1, :] = alphas[g] * l_ref[hd:hd + 1, :] + pv[HEAD_DIM:HEAD_DIM + 1, :]
        acc_ref[rows, :] = alphas[g] * acc_ref[rows, :] + pv[:HEAD_DIM, :]


def _moba_attn_kernel(nq_ref, jg_ref, qa_ref, k_ref, vt_ref, tab_ref, o_ref, m_ref, l_ref, acc_ref, *, group):
    Lb = MOBA_BLOCK
    step = pl.program_id(1)
    n = nq_ref[step]
    first = jg_ref[step] * group

    @pl.when(first == 0)
    def _():
        m_ref[...] = jnp.full(m_ref.shape, NEG_INF, F32)
        l_ref[...] = jnp.zeros(l_ref.shape, F32)
        acc_ref[...] = jnp.zeros(acc_ref.shape, F32)

    ones = jnp.ones((SUM_ROWS, Lb), BF16)

    def past_block(i, carry):
        j = first + i
        lane = lax.broadcasted_iota(jnp.int32, (Lb, MOBA_SLOT), 1)
        onehot = jnp.where(lane == HEAD_DIM + j, 1.0, 0.0).astype(BF16)
        offset_blocks = (j - n).astype(F32)
        for kv in range(N_KV_HEADS):
            ka = k_ref[i, :, kv * MOBA_SLOT:(kv + 1) * MOBA_SLOT] + onehot
            vt = jnp.concatenate([vt_ref[i, kv * HEAD_DIM:(kv + 1) * HEAD_DIM, :], ones], axis=0)
            _moba_scores_update(ka, vt, qa_ref, tab_ref, m_ref, l_ref, acc_ref, kv, offset_blocks, None)
        return carry

    lax.fori_loop(0, jnp.minimum(n - first, group), past_block, 0)

    @pl.when(n - first < group)
    def _():
        i = n - first
        key = lax.broadcasted_iota(jnp.int32, (Lb, Lb), 0)
        qry = lax.broadcasted_iota(jnp.int32, (Lb, Lb), 1)
        causal = key <= qry
        for kv in range(N_KV_HEADS):
            ka = k_ref[i, :, kv * MOBA_SLOT:(kv + 1) * MOBA_SLOT]
            vt = jnp.concatenate([vt_ref[i, kv * HEAD_DIM:(kv + 1) * HEAD_DIM, :], ones], axis=0)
            _moba_scores_update(ka, vt, qa_ref, tab_ref, m_ref, l_ref, acc_ref, kv, 0.0, causal)
        inv = 1.0 / l_ref[...]
        acc = acc_ref[...].reshape(N_Q_HEADS, HEAD_DIM, Lb) * inv[:, None, :]
        o_ref[...] = acc.reshape(N_Q_HEADS * HEAD_DIM, Lb).T


def _moba_attn(qaug, kslot, vt, batch, seq):
    n = batch * seq
    nblk = seq // MOBA_BLOCK
    group = min(MOBA_KEY_GROUP, nblk)
    assert nblk % group == 0
    pairs = [(i, jg) for i in range(nblk) for jg in range(i // group + 1)]
    nq = jnp.asarray([p[0] for p in pairs], jnp.int32)
    jk = jnp.asarray([p[1] for p in pairs], jnp.int32)
    ngrp = nblk // group
    key_off = jnp.arange(MOBA_BLOCK, dtype=F32)[None, :, None]
    tab = jnp.broadcast_to((LOG2E * jnp.asarray(ALIBI_SLOPES, F32))[:, None, None] * key_off,
                           (N_Q_HEADS, MOBA_BLOCK, MOBA_BLOCK))
    qmap = lambda b, s, nq_r, jk_r: (b * nblk + nq_r[s], 0)
    kmap = lambda b, s, nq_r, jk_r: (b * ngrp + jk_r[s], 0, 0)
    grid_spec = pltpu.PrefetchScalarGridSpec(
        num_scalar_prefetch=2,
        grid=(batch, len(pairs)),
        in_specs=[
            pl.BlockSpec((MOBA_Q_WIDTH, MOBA_BLOCK), lambda b, s, nq_r, jk_r: (0, b * nblk + nq_r[s])),
            pl.BlockSpec((group, MOBA_BLOCK, MOBA_K_WIDTH), kmap),
            pl.BlockSpec((group, ATTN_KV_DIM, MOBA_BLOCK), kmap),
            pl.BlockSpec((N_Q_HEADS, MOBA_BLOCK, MOBA_BLOCK), lambda b, s, nq_r, jk_r: (0, 0, 0),
                         pipeline_mode=pl.Buffered(1)),
        ],
        out_specs=pl.BlockSpec((MOBA_BLOCK, ATTN_Q_DIM), qmap),
        scratch_shapes=[
            pltpu.VMEM((N_Q_HEADS, MOBA_BLOCK), F32),
            pltpu.VMEM((N_Q_HEADS, MOBA_BLOCK), F32),
            pltpu.VMEM((N_Q_HEADS * HEAD_DIM, MOBA_BLOCK), F32),
        ],
    )
    return pl.pallas_call(
        functools.partial(_moba_attn_kernel, group=group),
        out_shape=jax.ShapeDtypeStruct((n, ATTN_Q_DIM), F32),
        grid_spec=grid_spec,
        compiler_params=_params(("parallel", "arbitrary")),
        name="moba_attn",
    )(nq, jk, qaug, kslot, vt, tab)


def _moba_slot_weights(w_in):
    d = w_in.shape[0]
    fill = MOBA_SLOT - HEAD_DIM
    wq = w_in[:, :ATTN_Q_DIM].reshape(d, N_Q_HEADS, HEAD_DIM)
    wk = w_in[:, ATTN_Q_DIM:ATTN_Q_DIM + ATTN_KV_DIM].reshape(d, N_KV_HEADS, HEAD_DIM)
    wq = jnp.pad(wq, ((0, 0), (0, 0), (0, fill))).reshape(d, MOBA_Q_WIDTH)
    wk = jnp.pad(wk, ((0, 0), (0, 0), (0, fill))).reshape(d, MOBA_K_WIDTH)
    return jnp.concatenate([wq, wk, w_in[:, ATTN_Q_DIM + ATTN_KV_DIM:]], axis=1)


def _moba_core(proj, batch, seq):
    nblk = seq // MOBA_BLOCK
    assert nblk <= HEAD_DIM, "one free query lane per key block"
    width = N_KV_HEADS * nblk
    kmean = _moba_kmean(proj)
    km = kmean.reshape(batch, nblk, N_KV_HEADS, MOBA_SLOT).transpose(0, 2, 3, 1)
    eye = jnp.eye(N_KV_HEADS, dtype=F32)
    km_mat = jnp.einsum('bhej,hH->bheHj', km, eye)
    km_mat = jnp.broadcast_to(km_mat[:, :, None], (batch, N_KV_HEADS, GQA_GROUP, MOBA_SLOT, N_KV_HEADS, nblk))
    km_mat = km_mat.reshape(batch, MOBA_Q_WIDTH, width)
    h_of = jnp.arange(width) // nblk
    c_of = jnp.arange(width) % nblk
    slot = jnp.arange(MOBA_Q_WIDTH) // MOBA_SLOT
    lane = jnp.arange(MOBA_Q_WIDTH) % MOBA_SLOT
    hit = (h_of[:, None] == (slot // GQA_GROUP)[None, :]) & (lane[None, :] == HEAD_DIM + c_of[:, None])
    expand = jnp.where(hit, MASKED, 0.0).astype(BF16)
    qaug, kslot, vt = _moba_select(proj, km_mat, expand, batch, seq)
    return _moba_attn(qaug, kslot, vt, batch, seq)


def _cumsum_rows(tri, x):
    hi = x.astype(BF16)
    r1 = x - hi.astype(F32)
    mid = r1.astype(BF16)
    lo = (r1 - mid.astype(F32)).astype(BF16)
    return _dot(tri, hi) + (_dot(tri, mid) + _dot(tri, lo))


def _lin_chunk(q, k, v, log_g, st_ref, h, tri, causal):
    b = _cumsum_rows(tri, log_g)
    b_last = b[LIN_CHUNK - 1:LIN_CHUNK, :]
    q_dec = (q * jnp.exp(b)).astype(BF16)
    k_dec = (k * jnp.exp(-b)).astype(BF16)
    attn = jnp.where(causal, _dot_nt(q_dec, k_dec), 0.0).astype(BF16)
    vb = v.astype(BF16)
    st = st_ref[h]
    o = _dot(attn, vb) + _dot_nt(q_dec, st.astype(BF16))
    k_tail = (k * jnp.exp(b_last - b)).astype(BF16)
    st_ref[h] = st * jnp.exp(b_last) + _dot_tn(vb, k_tail)
    return o


def _head_norm_gate(o, gain, g):
    o = o * lax.rsqrt(jnp.mean(o * o, axis=-1, keepdims=True) + RMS_EPS) * gain
    return o * _silu(g)


def _tri_causal():
    r = lax.broadcasted_iota(jnp.int32, (LIN_CHUNK, LIN_CHUNK), 0)
    c = lax.broadcasted_iota(jnp.int32, (LIN_CHUNK, LIN_CHUNK), 1)
    causal = r >= c
    return causal.astype(BF16), causal


def _gla_kernel(q_ref, k_ref, v_ref, g_ref, a_ref, wd_ref, bd_ref, gain_ref, o_ref, st_ref):
    @pl.when(pl.program_id(1) == 0)
    def _():
        st_ref[...] = jnp.zeros(st_ref.shape, F32)

    tri, causal = _tri_causal()
    K, V = GLA_KEY_DIM, GLA_VAL_DIM

    def body(c, carry):
        rows = pl.ds(pl.multiple_of(c * LIN_CHUNK, LIN_CHUNK), LIN_CHUNK)
        z = _dot_split(a_ref[rows, :], wd_ref[...]) + bd_ref[...]
        log_alpha = (jnp.minimum(z, 0.0) - jnp.log(1.0 + jnp.exp(-jnp.abs(z)))) * (1.0 / GLA_GATE_TEMP)
        for h in range(GLA_HEADS):
            q = q_ref[rows, h * K:(h + 1) * K] * (K ** -0.5)
            k = k_ref[rows, h * K:(h + 1) * K]
            v = v_ref[rows, h * V:(h + 1) * V]
            o = _lin_chunk(q, k, v, log_alpha[:, h * K:(h + 1) * K], st_ref, h, tri, causal)
            o_ref[rows, h * V:(h + 1) * V] = _head_norm_gate(o, gain_ref[...], g_ref[rows, h * V:(h + 1) * V])
        return carry

    lax.fori_loop(0, LIN_ROWS // LIN_CHUNK, body, 0)


def _gla_core(proj, wd_pad, bd, gain, batch, seq):
    n = proj.shape[0]
    nt = seq // LIN_ROWS
    dk = GLA_HEADS * GLA_KEY_DIM
    dv = GLA_HEADS * GLA_VAL_DIM
    rows = lambda c: (lambda b, t: (b * nt + t, c))
    const = lambda b, t: (0, 0)
    return pl.pallas_call(
        _gla_kernel,
        out_shape=jax.ShapeDtypeStruct((n, dv), F32),
        grid=(batch, nt),
        in_specs=[
            pl.BlockSpec((LIN_ROWS, dk), rows(0)),
            pl.BlockSpec((LIN_ROWS, dk), rows(1)),
            pl.BlockSpec((LIN_ROWS, dv), rows(1)),
            pl.BlockSpec((LIN_ROWS, dv), rows(2)),
            pl.BlockSpec((LIN_ROWS, LANES), rows((2 * dk + 2 * dv) // LANES)),
            pl.BlockSpec((LANES, dk), const),
            pl.BlockSpec((1, dk), const),
            pl.BlockSpec((1, GLA_VAL_DIM), const),
        ],
        out_specs=pl.BlockSpec((LIN_ROWS, dv), rows(0)),
        scratch_shapes=[pltpu.VMEM((GLA_HEADS, GLA_VAL_DIM, GLA_KEY_DIM), F32)],
        compiler_params=_params(("parallel", "arbitrary")),
        name="gla_core",
    )(proj, proj, proj, proj, proj, wd_pad, bd.reshape(1, dk), gain.reshape(1, GLA_VAL_DIM))


def _hgrn_kernel(q_ref, f_ref, i_ref, g_ref, lbl_ref, gain_ref, o_ref, st_ref, *, layer):
    @pl.when(pl.program_id(1) == 0)
    def _():
        st_ref[...] = jnp.zeros(st_ref.shape, F32)

    tri, causal = _tri_causal()
    K, V = HGRN_KEY_DIM, HGRN_VAL_DIM
    logits = lbl_ref[...]
    e = jnp.exp(logits - jnp.max(logits, axis=0, keepdims=True))
    p = e / jnp.sum(e, axis=0, keepdims=True)
    lb = jnp.zeros((1, logits.shape[1]), F32)
    for l in range(1, layer + 1):
        lb = lb + p[l:l + 1, :]

    def body(c, carry):
        rows = pl.ds(pl.multiple_of(c * LIN_CHUNK, LIN_CHUNK), LIN_CHUNK)
        for h in range(HGRN_HEADS):
            cols = slice(h * K, (h + 1) * K)
            q = _silu(q_ref[rows, cols]) * (K ** -0.5)
            lbh = lb[:, cols]
            f = lbh + (1.0 - lbh) * _sigmoid(f_ref[rows, cols])
            o = _lin_chunk(q, 1.0 - f, i_ref[rows, cols], jnp.log(f), st_ref, h, tri, causal)
            o_ref[rows, cols] = _head_norm_gate(o, gain_ref[...], g_ref[rows, cols])
        return carry

    lax.fori_loop(0, LIN_ROWS // LIN_CHUNK, body, 0)


def _hgrn_core(proj, lb_logits, gain, layer, batch, seq):
    n = proj.shape[0]
    nt = seq // LIN_ROWS
    d = D_MODEL
    rows = lambda c: (lambda b, t: (b * nt + t, c))
    const = lambda b, t: (0, 0)
    return pl.pallas_call(
        functools.partial(_hgrn_kernel, layer=layer),
        out_shape=jax.ShapeDtypeStruct((n, d), F32),
        grid=(batch, nt),
        in_specs=[
            pl.BlockSpec((LIN_ROWS, d), rows(0)),
            pl.BlockSpec((LIN_ROWS, d), rows(1)),
            pl.BlockSpec((LIN_ROWS, d), rows(2)),
            pl.BlockSpec((LIN_ROWS, d), rows(3)),
            pl.BlockSpec(lb_logits.shape, const),
            pl.BlockSpec((1, HGRN_VAL_DIM), const),
        ],
        out_specs=pl.BlockSpec((LIN_ROWS, d), rows(0)),
        scratch_shapes=[pltpu.VMEM((HGRN_HEADS, HGRN_VAL_DIM, HGRN_KEY_DIM), F32)],
        compiler_params=_params(("parallel", "arbitrary")),
        name="hgrn_core",
    )(proj, proj, proj, proj, lb_logits, gain.reshape(1, HGRN_VAL_DIM))


def kernel(x, norm_mix, norm_ffn, swa_w_in, swa_sinks, swa_w_out, moba_w_in, moba_w_out, gla_w_in, gla_w_decay_up, gla_b_decay, gla_out_norm, gla_w_out, hgrn_w_in, hgrn_lb_logits, hgrn_out_norm, hgrn_w_out, ffn_w_gate_up, ffn_w_down, final_norm):
    batch, seq, d = x.shape
    depth = norm_mix.shape[0]
    xf = x.reshape(batch * seq, d)
    for i in range(depth):
        kind, j = i % N_MIXERS, i // N_MIXERS
        if kind == 0:
            proj = _norm_proj(xf, norm_mix[i], _moba_slot_weights(swa_w_in[j]).astype(BF16), MOBA_WIDTH // 2)
            o = _swa_core(proj, swa_sinks[j], batch, seq)
            w_out = swa_w_out[j]
        elif kind == 1:
            proj = _norm_proj(xf, norm_mix[i], _moba_slot_weights(moba_w_in[j]).astype(BF16), MOBA_WIDTH // 2)
            o = _moba_core(proj, batch, seq)
            w_out = moba_w_out[j]
        elif kind == 2:
            pad = LANES - GLA_GATE_RANK
            w_in = jnp.pad(gla_w_in[j], ((0, 0), (0, pad))).astype(BF16)
            wd_pad = jnp.pad(gla_w_decay_up[j], ((0, pad), (0, 0)))
            proj = _norm_proj(xf, norm_mix[i], w_in, 640)
            o = _gla_core(proj, wd_pad, gla_b_decay[j], gla_out_norm[j], batch, seq)
            w_out = gla_w_out[j]
        else:
            proj = _norm_proj(xf, norm_mix[i], hgrn_w_in[j].astype(BF16), 1024)
            o = _hgrn_core(proj, hgrn_lb_logits, hgrn_out_norm[j], i, batch, seq)
            w_out = hgrn_w_out[j]
        last = i == depth - 1
        xf = _ffn(xf, o, w_out.astype(BF16), norm_ffn[i], ffn_w_gate_up[i].astype(BF16),
                  ffn_w_down[i].astype(BF16), final_norm, last)
    return xf.reshape(batch, seq, d)
```

```python
import functools
import math

import jax
import jax.numpy as jnp
from jax import lax
from jax.experimental import pallas as pl
from jax.experimental.pallas import tpu as pltpu

F32 = jnp.float32
BF16 = jnp.bfloat16

D_MODEL = 1024
HEAD_DIM = 64
N_Q_HEADS = 16
N_KV_HEADS = 4
GQA_GROUP = 4
ATTN_Q_DIM = 1024
ATTN_KV_DIM = 256
SWA_BLOCK = 128
SWA_WINDOW = 128
MOBA_BLOCK = 256
MOBA_TOP_K = 3
GLA_HEADS = 4
GLA_KEY_DIM = 128
GLA_VAL_DIM = 256
GLA_GATE_RANK = 16
GLA_GATE_TEMP = 16.0
HGRN_HEADS = 8
HGRN_KEY_DIM = 128
HGRN_VAL_DIM = 128
LIN_CHUNK = 64
D_FF = 2816
RMS_EPS = 1e-6
N_MIXERS = 4

LANES = 128
VMEM_LIMIT = 56 * 1024 * 1024
FFN_CHUNK = 256
ROW_TILE = 512
LIN_ROWS = 256

ALIBI_SLOPES = tuple(2.0 ** (-8.0 * (i + 1) / N_Q_HEADS) for i in range(N_Q_HEADS))
NEG_INF = float("-inf")


def _params(sem):
    return pltpu.CompilerParams(dimension_semantics=sem, vmem_limit_bytes=VMEM_LIMIT)


def _dot(a, b):
    return jnp.dot(a, b, preferred_element_type=F32)


def _dot_nt(a, b):
    return lax.dot_general(a, b, (((1,), (1,)), ((), ())), preferred_element_type=F32)


def _dot_tn(a, b):
    return lax.dot_general(a, b, (((0,), (0,)), ((), ())), preferred_element_type=F32)


def _split(x):
    hi = x.astype(BF16)
    lo = (x - hi.astype(F32)).astype(BF16)
    return hi, lo


def _dot_split(a, b):
    ah, al = _split(a)
    bh, bl = _split(b)
    return _dot(ah, bh) + (_dot(ah, bl) + _dot(al, bh))


def _rms(x, gain):
    return x * lax.rsqrt(jnp.mean(x * x, axis=-1, keepdims=True) + RMS_EPS) * gain


def _sigmoid(x):
    return 1.0 / (1.0 + jnp.exp(-x))


def _silu(x):
    return x * _sigmoid(x)


def _norm_proj_kernel(x_ref, g_ref, w_ref, o_ref, h_ref):
    @pl.when(pl.program_id(1) == 0)
    def _():
        h_ref[...] = _rms(x_ref[...], g_ref[...]).astype(BF16)

    o_ref[...] = _dot(h_ref[...], w_ref[...])


def _norm_proj(x, gain, w, tn):
    n, d = x.shape
    dout = w.shape[1]
    return pl.pallas_call(
        _norm_proj_kernel,
        out_shape=jax.ShapeDtypeStruct((n, dout), F32),
        grid=(n // ROW_TILE, dout // tn),
        in_specs=[
            pl.BlockSpec((ROW_TILE, d), lambda i, j: (i, 0)),
            pl.BlockSpec((1, d), lambda i, j: (0, 0)),
            pl.BlockSpec((d, tn), lambda i, j: (0, j)),
        ],
        out_specs=pl.BlockSpec((ROW_TILE, tn), lambda i, j: (i, j)),
        scratch_shapes=[pltpu.VMEM((ROW_TILE, d), BF16)],
        compiler_params=_params(("parallel", "arbitrary")),
        name="norm_proj",
    )(x, gain.reshape(1, d), w)


def _ffn_kernel(x_ref, o_ref, wo_ref, g_ref, wgu_ref, wd_ref, fg_ref, out_ref,
                x1_ref, h_ref, acc_ref, *, final_norm):
    x1 = x_ref[...] + _dot(o_ref[...].astype(BF16), wo_ref[...])
    x1_ref[...] = x1
    h_ref[...] = _rms(x1, g_ref[...]).astype(BF16)
    for c in range(D_FF // FFN_CHUNK):
        h = h_ref[...]
        gate = _dot(h, wgu_ref[:, c * FFN_CHUNK:(c + 1) * FFN_CHUNK])
        up = _dot(h, wgu_ref[:, D_FF + c * FFN_CHUNK:D_FF + (c + 1) * FFN_CHUNK])
        act = (_silu(gate) * up).astype(BF16)
        part = _dot(act, wd_ref[c * FFN_CHUNK:(c + 1) * FFN_CHUNK, :])
        if c == 0:
            acc_ref[...] = part
        else:
            acc_ref[...] += part
    y = x1_ref[...] + acc_ref[...]
    if final_norm:
        y = _rms(y, fg_ref[...])
    out_ref[...] = y


def _ffn(x, o, wo, gain, wgu, wd, final_gain, final_norm):
    n, d = x.shape
    const = lambda i: (0, 0)
    row = lambda i: (i, 0)
    return pl.pallas_call(
        functools.partial(_ffn_kernel, final_norm=final_norm),
        out_shape=jax.ShapeDtypeStruct((n, d), F32),
        grid=(n // ROW_TILE,),
        in_specs=[
            pl.BlockSpec((ROW_TILE, d), row),
            pl.BlockSpec((ROW_TILE, d), row),
            pl.BlockSpec((d, d), const, pipeline_mode=pl.Buffered(1)),
            pl.BlockSpec((1, d), const),
            pl.BlockSpec((d, 2 * D_FF), const, pipeline_mode=pl.Buffered(1)),
            pl.BlockSpec((D_FF, d), const, pipeline_mode=pl.Buffered(1)),
            pl.BlockSpec((1, d), const),
        ],
        out_specs=pl.BlockSpec((ROW_TILE, d), row),
        scratch_shapes=[
            pltpu.VMEM((ROW_TILE, d), F32),
            pltpu.VMEM((ROW_TILE, d), BF16),
            pltpu.VMEM((ROW_TILE, d), F32),
        ],
        compiler_params=_params(("parallel",)),
        name="outproj_ffn",
    )(x, o, wo, gain.reshape(1, d), wgu, wd, final_gain.reshape(1, d))


def _swa_kernel(q_ref, kp_ref, ko_ref, vp_ref, vo_ref, sink_ref, tab_ref, o_ref, ot_ref):
    L = SWA_BLOCK
    n = pl.program_id(1)
    first = jnp.where(n == 0, 1, 0)
    qt = (q_ref[...] * (LOG2E * HEAD_DIM ** -0.5)).T.astype(BF16)
    k = jnp.concatenate([kp_ref[...], ko_ref[...]], axis=0).astype(BF16)
    vt = jnp.concatenate([vp_ref[...], vo_ref[...]], axis=0).T.astype(BF16)
    ones = jnp.ones((SUM_ROWS, 2 * L), BF16)
    sinks = sink_ref[...] * LOG2E
    for kv in range(N_KV_HEADS):
        heads = [kv * GQA_GROUP + g for g in range(GQA_GROUP)]
        qg = jnp.concatenate([qt[hd * MOBA_SLOT:(hd + 1) * MOBA_SLOT, :] for hd in heads], axis=1)
        s_all = _dot(k[:, kv * MOBA_SLOT:(kv + 1) * MOBA_SLOT], qg)
        probs, extras = [], []
        for g, hd in enumerate(heads):
            s = s_all[:, g * L:(g + 1) * L] + tab_ref[first, hd]
            sink = sinks[:, hd:hd + 1]
            m = jnp.maximum(jnp.max(s, axis=0, keepdims=True), sink)
            probs.append(jnp.exp2(s - m).astype(BF16))
            extras.append(jnp.exp2(sink - m))
        vaug = jnp.concatenate([vt[kv * HEAD_DIM:(kv + 1) * HEAD_DIM, :], ones], axis=0)
        pv_all = _dot(vaug, jnp.concatenate(probs, axis=1))
        for g, hd in enumerate(heads):
            pv = pv_all[:, g * L:(g + 1) * L]
            denom = pv[HEAD_DIM:HEAD_DIM + 1, :] + extras[g]
            ot_ref[hd * HEAD_DIM:(hd + 1) * HEAD_DIM, :] = pv[:HEAD_DIM, :] / denom
    o_ref[...] = ot_ref[...].T


def _swa_core(proj, sinks, batch, seq):
    n = proj.shape[0]
    L = SWA_BLOCK
    nb = seq // L
    kcol = MOBA_Q_WIDTH // MOBA_K_WIDTH
    vcol = (MOBA_Q_WIDTH + MOBA_K_WIDTH) // ATTN_KV_DIM
    own = lambda c: (lambda b, i: (b * nb + i, c))
    prev = lambda c: (lambda b, i: (b * nb + jnp.maximum(i - 1, 0), c))
    key = jnp.arange(2 * L)[:, None]
    qry = jnp.arange(L)[None, :]
    dist = (L + qry) - key
    band = (dist >= 0) & (dist < SWA_WINDOW)
    slopes = (LOG2E * jnp.asarray(ALIBI_SLOPES, F32))[:, None, None]
    bias = -slopes * dist.astype(F32)[None]
    tab = jnp.stack([jnp.where(band[None], bias, MASKED),
                     jnp.where((band & (key >= L))[None], bias, MASKED)])
    return pl.pallas_call(
        _swa_kernel,
        out_shape=jax.ShapeDtypeStruct((n, ATTN_Q_DIM), F32),
        grid=(batch, nb),
        in_specs=[
            pl.BlockSpec((L, MOBA_Q_WIDTH), own(0)),
            pl.BlockSpec((L, MOBA_K_WIDTH), prev(kcol)),
            pl.BlockSpec((L, MOBA_K_WIDTH), own(kcol)),
            pl.BlockSpec((L, ATTN_KV_DIM), prev(vcol)),
            pl.BlockSpec((L, ATTN_KV_DIM), own(vcol)),
            pl.BlockSpec((1, N_Q_HEADS), lambda b, i: (0, 0)),
            pl.BlockSpec((2, N_Q_HEADS, 2 * L, L), lambda b, i: (0, 0, 0, 0), pipeline_mode=pl.Buffered(1)),
        ],
        out_specs=pl.BlockSpec((L, ATTN_Q_DIM), own(0)),
        scratch_shapes=[pltpu.VMEM((ATTN_Q_DIM, L), F32)],
        compiler_params=_params(("parallel", "parallel")),
        name="swa_core",
    )(proj, proj, proj, proj, proj, sinks.reshape(1, N_Q_HEADS), tab)


MOBA_SLOT = 2 * HEAD_DIM
PEN_LANES = 32
SLOPE_LANES = (HEAD_DIM + PEN_LANES, HEAD_DIM + PEN_LANES + 1)
MOBA_Q_WIDTH = N_Q_HEADS * MOBA_SLOT
MOBA_K_WIDTH = N_KV_HEADS * MOBA_SLOT
MOBA_WIDTH = MOBA_Q_WIDTH + MOBA_K_WIDTH + ATTN_KV_DIM
MASKED = -1e30
LOG2E = math.log2(math.e)
SUM_ROWS = 16
MOBA_KEY_GROUP = 8

_KMEAN_BLOCKS = 8


def _kmean_kernel(k_ref, o_ref):
    k = k_ref[...].reshape(_KMEAN_BLOCKS, MOBA_BLOCK, MOBA_K_WIDTH)
    o_ref[...] = jnp.sum(k, axis=1) * (1.0 / MOBA_BLOCK)


def _moba_kmean(proj):
    n = proj.shape[0]
    rows = _KMEAN_BLOCKS * MOBA_BLOCK
    return pl.pallas_call(
        _kmean_kernel,
        out_shape=jax.ShapeDtypeStruct((n // MOBA_BLOCK, MOBA_K_WIDTH), F32),
        grid=(n // rows,),
        in_specs=[pl.BlockSpec((rows, MOBA_K_WIDTH), lambda i: (i, MOBA_Q_WIDTH // MOBA_K_WIDTH))],
        out_specs=pl.BlockSpec((_KMEAN_BLOCKS, MOBA_K_WIDTH), lambda i: (i, 0)),
        compiler_params=_params(("parallel",)),
        name="moba_kmean",
    )(proj)


def _moba_select_kernel(q_ref, km_ref, expand_ref, slope_ref, k_ref, v_ref, qa_ref, ks_ref, vt_ref, *, nblk):
    n = pl.program_id(1)
    q = q_ref[...] * (HEAD_DIM ** -0.5)
    gate = _dot_split(q, km_ref[0])
    lane_i = lax.broadcasted_iota(jnp.int32, gate.shape, 1)
    blk = jnp.bitwise_and(lane_i, nblk - 1)
    head = jnp.right_shift(lane_i, int(math.log2(nblk)))
    lane = lane_i.astype(F32)
    gate = jnp.where(blk < n, gate, NEG_INF)
    sel = jnp.zeros(gate.shape, F32)
    for h in range(N_KV_HEADS):
        g = jnp.where(head == h, gate, NEG_INF)
        for _ in range(MOBA_TOP_K):
            mx = jnp.max(g, axis=-1, keepdims=True)
            is_max = (g == mx) & (mx > NEG_INF)
            first = jnp.min(jnp.where(is_max, lane, float(4 * LANES)), axis=-1, keepdims=True)
            pick = lane == first
            sel = jnp.where(pick, 1.0, sel)
            g = jnp.where(pick, NEG_INF, g)
    unselected = (1.0 - sel).astype(BF16)
    q_feat = q * LOG2E + slope_ref[...]
    for var in range(qa_ref.shape[0]):
        penalty = _dot(unselected, expand_ref[var])
        qa_ref[var] = (q_feat + penalty).T.astype(BF16)
    ks_ref[0] = k_ref[...].astype(BF16)
    vt_ref[0] = v_ref[...].T.astype(BF16)


def _moba_select(proj, km_mat, expand, slope_feat, batch, seq):
    n = proj.shape[0]
    nblk = seq // MOBA_BLOCK
    width = N_KV_HEADS * nblk
    nvar = expand.shape[0]
    vcol = (MOBA_Q_WIDTH + MOBA_K_WIDTH) // ATTN_KV_DIM
    return pl.pallas_call(
        functools.partial(_moba_select_kernel, nblk=nblk),
        out_shape=(jax.ShapeDtypeStruct((nvar, MOBA_Q_WIDTH, n), BF16),
                   jax.ShapeDtypeStruct((n // MOBA_BLOCK, MOBA_BLOCK, MOBA_K_WIDTH), BF16),
                   jax.ShapeDtypeStruct((n // MOBA_BLOCK, ATTN_KV_DIM, MOBA_BLOCK), BF16)),
        grid=(batch, nblk),
        in_specs=[
            pl.BlockSpec((MOBA_BLOCK, MOBA_Q_WIDTH), lambda b, i: (b * nblk + i, 0)),
            pl.BlockSpec((1, MOBA_Q_WIDTH, width), lambda b, i: (b, 0, 0)),
            pl.BlockSpec((nvar, width, MOBA_Q_WIDTH), lambda b, i: (0, 0, 0)),
            pl.BlockSpec((1, MOBA_Q_WIDTH), lambda b, i: (0, 0)),
            pl.BlockSpec((MOBA_BLOCK, MOBA_K_WIDTH), lambda b, i: (b * nblk + i, MOBA_Q_WIDTH // MOBA_K_WIDTH)),
            pl.BlockSpec((MOBA_BLOCK, ATTN_KV_DIM), lambda b, i: (b * nblk + i, vcol)),
        ],
        out_specs=(pl.BlockSpec((nvar, MOBA_Q_WIDTH, MOBA_BLOCK), lambda b, i: (0, 0, b * nblk + i)),
                   pl.BlockSpec((1, MOBA_BLOCK, MOBA_K_WIDTH), lambda b, i: (b * nblk + i, 0, 0)),
                   pl.BlockSpec((1, ATTN_KV_DIM, MOBA_BLOCK), lambda b, i: (b * nblk + i, 0, 0))),
        compiler_params=_params(("parallel", "parallel")),
        name="moba_select",
    )(proj, km_mat, expand, slope_feat, proj, proj)


def _moba_scores(ka, qa_ref, var, kv):
    heads = [kv * GQA_GROUP + g for g in range(GQA_GROUP)]
    qt = jnp.concatenate([qa_ref[var, hd * MOBA_SLOT:(hd + 1) * MOBA_SLOT, :] for hd in heads], axis=1)
    return _dot(ka, qt)


def _moba_block_update(k_slots, vt_all, qa_ref, var, m_ref, l_ref, acc_ref, offset_blocks, causal):
    ones = jnp.ones((SUM_ROWS, MOBA_BLOCK), BF16)
    s_next = _moba_scores(k_slots(0), qa_ref, var, 0)
    for kv in range(N_KV_HEADS):
        s_all = s_next
        if kv + 1 < N_KV_HEADS:
            s_next = _moba_scores(k_slots(kv + 1), qa_ref, var, kv + 1)
        vt = jnp.concatenate([vt_all(kv), ones], axis=0)
        _moba_softmax_pv(s_all, vt, m_ref, l_ref, acc_ref, kv, offset_blocks, causal)


def _moba_softmax_pv(s_all, vt, m_ref, l_ref, acc_ref, kv, offset_blocks, causal):
    Lb = MOBA_BLOCK
    heads = [kv * GQA_GROUP + g for g in range(GQA_GROUP)]
    probs, alphas = [], []
    for g, hd in enumerate(heads):
        off = (LOG2E * ALIBI_SLOPES[hd] * MOBA_BLOCK) * offset_blocks
        s = s_all[:, g * Lb:(g + 1) * Lb]
        if causal is not None:
            s = jnp.where(causal, s, MASKED)
        m_old = m_ref[hd:hd + 1, :]
        m_new = jnp.maximum(m_old, jnp.max(s, axis=0, keepdims=True) + off)
        alphas.append(jnp.exp2(m_old - m_new))
        probs.append(jnp.exp2(s - (m_new - off)).astype(BF16))
        m_ref[hd:hd + 1, :] = m_new
    pv_all = _dot(vt, jnp.concatenate(probs, axis=1))
    for g, hd in enumerate(heads):
        pv = pv_all[:, g * Lb:(g + 1) * Lb]
        rows = slice(hd * HEAD_DIM, (hd + 1) * HEAD_DIM)
        l_ref[hd:hd + 1, :] = alphas[g] * l_ref[hd:hd + 1, :] + pv[HEAD_DIM:HEAD_DIM + 1, :]
        acc_ref[rows, :] = alphas[g] * acc_ref[rows, :] + pv[:HEAD_DIM, :]


def _moba_attn_kernel(nq_ref, jg_ref, qa_ref, k_ref, vt_ref, o_ref, m_ref, l_ref, acc_ref, *, group):
    Lb = MOBA_BLOCK
    step = pl.program_id(1)
    n = nq_ref[step]
    first = jg_ref[step] * group
    var = first // PEN_LANES

    @pl.when(first == 0)
    def _():
        m_ref[...] = jnp.full(m_ref.shape, NEG_INF, F32)
        l_ref[...] = jnp.zeros(l_ref.shape, F32)
        acc_ref[...] = jnp.zeros(acc_ref.shape, F32)

    lane = lax.broadcasted_iota(jnp.int32, (Lb, MOBA_SLOT), 1)
    key_off = (lax.broadcasted_iota(jnp.int32, (Lb, MOBA_SLOT), 0) - (Lb - 1)).astype(F32)
    slope_lanes = (lane == SLOPE_LANES[0]) | (lane == SLOPE_LANES[1])
    key_feat = jnp.where(slope_lanes, key_off, 0.0)

    def past_block(i, carry):
        j = first + i
        extra = jnp.where(lane == HEAD_DIM + (j - var * PEN_LANES), 1.0, key_feat).astype(BF16)
        _moba_block_update(lambda kv: k_ref[i, :, kv * MOBA_SLOT:(kv + 1) * MOBA_SLOT] + extra,
                           lambda kv: vt_ref[i, kv * HEAD_DIM:(kv + 1) * HEAD_DIM, :],
                           qa_ref, var, m_ref, l_ref, acc_ref, (j - n).astype(F32), None)
        return carry

    lax.fori_loop(0, jnp.minimum(n - first, group), past_block, 0)

    @pl.when(n - first < group)
    def _():
        i = n - first
        key = lax.broadcasted_iota(jnp.int32, (Lb, Lb), 0)
        qry = lax.broadcasted_iota(jnp.int32, (Lb, Lb), 1)
        causal = key <= qry
        extra = key_feat.astype(BF16)
        _moba_block_update(lambda kv: k_ref[i, :, kv * MOBA_SLOT:(kv + 1) * MOBA_SLOT] + extra,
                           lambda kv: vt_ref[i, kv * HEAD_DIM:(kv + 1) * HEAD_DIM, :],
                           qa_ref, var, m_ref, l_ref, acc_ref, 0.0, causal)
        inv = 1.0 / l_ref[...]
        acc = acc_ref[...].reshape(N_Q_HEADS, HEAD_DIM, Lb) * inv[:, None, :]
        o_ref[...] = acc.reshape(N_Q_HEADS * HEAD_DIM, Lb).T


def _moba_attn(qaug, kslot, vt, batch, seq):
    n = batch * seq
    nblk = seq // MOBA_BLOCK
    group = min(MOBA_KEY_GROUP, nblk)
    assert nblk % group == 0
    pairs = [(i, jg) for i in range(nblk) for jg in range(i // group + 1)]
    nq = jnp.asarray([p[0] for p in pairs], jnp.int32)
    jk = jnp.asarray([p[1] for p in pairs], jnp.int32)
    ngrp = nblk // group
    nvar = qaug.shape[0]
    assert PEN_LANES % group == 0, "a key group must not straddle two query variants"
    qmap = lambda b, s, nq_r, jk_r: (b * nblk + nq_r[s], 0)
    kmap = lambda b, s, nq_r, jk_r: (b * ngrp + jk_r[s], 0, 0)
    grid_spec = pltpu.PrefetchScalarGridSpec(
        num_scalar_prefetch=2,
        grid=(batch, len(pairs)),
        in_specs=[
            pl.BlockSpec((nvar, MOBA_Q_WIDTH, MOBA_BLOCK), lambda b, s, nq_r, jk_r: (0, 0, b * nblk + nq_r[s])),
            pl.BlockSpec((group, MOBA_BLOCK, MOBA_K_WIDTH), kmap),
            pl.BlockSpec((group, ATTN_KV_DIM, MOBA_BLOCK), kmap),
        ],
        out_specs=pl.BlockSpec((MOBA_BLOCK, ATTN_Q_DIM), qmap),
        scratch_shapes=[
            pltpu.VMEM((N_Q_HEADS, MOBA_BLOCK), F32),
            pltpu.VMEM((N_Q_HEADS, MOBA_BLOCK), F32),
            pltpu.VMEM((N_Q_HEADS * HEAD_DIM, MOBA_BLOCK), F32),
        ],
    )
    return pl.pallas_call(
        functools.partial(_moba_attn_kernel, group=group),
        out_shape=jax.ShapeDtypeStruct((n, ATTN_Q_DIM), F32),
        grid_spec=grid_spec,
        compiler_params=_params(("parallel", "arbitrary")),
        name="moba_attn",
    )(nq, jk, qaug, kslot, vt)


def _moba_slot_weights(w_in):
    d = w_in.shape[0]
    fill = MOBA_SLOT - HEAD_DIM
    wq = w_in[:, :ATTN_Q_DIM].reshape(d, N_Q_HEADS, HEAD_DIM)
    wk = w_in[:, ATTN_Q_DIM:ATTN_Q_DIM + ATTN_KV_DIM].reshape(d, N_KV_HEADS, HEAD_DIM)
    wq = jnp.pad(wq, ((0, 0), (0, 0), (0, fill))).reshape(d, MOBA_Q_WIDTH)
    wk = jnp.pad(wk, ((0, 0), (0, 0), (0, fill))).reshape(d, MOBA_K_WIDTH)
    return jnp.concatenate([wq, wk, w_in[:, ATTN_Q_DIM + ATTN_KV_DIM:]], axis=1)


def _moba_core(proj, batch, seq):
    nblk = seq // MOBA_BLOCK
    nvar = -(-nblk // PEN_LANES)
    width = N_KV_HEADS * nblk
    kmean = _moba_kmean(proj)
    km = kmean.reshape(batch, nblk, N_KV_HEADS, MOBA_SLOT).transpose(0, 2, 3, 1)
    eye = jnp.eye(N_KV_HEADS, dtype=F32)
    km_mat = jnp.einsum('bhej,hH->bheHj', km, eye)
    km_mat = jnp.broadcast_to(km_mat[:, :, None], (batch, N_KV_HEADS, GQA_GROUP, MOBA_SLOT, N_KV_HEADS, nblk))
    km_mat = km_mat.reshape(batch, MOBA_Q_WIDTH, width)
    h_of = jnp.arange(width) // nblk
    c_of = jnp.arange(width) % nblk
    slot = jnp.arange(MOBA_Q_WIDTH) // MOBA_SLOT
    lane = jnp.arange(MOBA_Q_WIDTH) % MOBA_SLOT
    hit = (h_of[:, None] == (slot // GQA_GROUP)[None, :]) & (lane[None, :] == HEAD_DIM + (c_of % PEN_LANES)[:, None])
    in_var = (c_of // PEN_LANES)[None, :, None] == jnp.arange(nvar)[:, None, None]
    expand = jnp.where(hit[None] & in_var, MASKED, 0.0).astype(BF16)
    sigma = LOG2E * jnp.asarray(ALIBI_SLOPES, F32)
    hi = sigma.astype(BF16).astype(F32)
    lo = (sigma - hi).astype(BF16).astype(F32)
    slope_feat = (jnp.where(lane == SLOPE_LANES[0], hi[slot], 0.0)
                  + jnp.where(lane == SLOPE_LANES[1], lo[slot], 0.0)).reshape(1, MOBA_Q_WIDTH)
    qaug, kslot, vt = _moba_select(proj, km_mat, expand, slope_feat, batch, seq)
    return _moba_attn(qaug, kslot, vt, batch, seq)


def _cumsum_rows(tri, x):
    hi = x.astype(BF16)
    r1 = x - hi.astype(F32)
    mid = r1.astype(BF16)
    lo = (r1 - mid.astype(F32)).astype(BF16)
    return _dot(tri, hi) + (_dot(tri, mid) + _dot(tri, lo))


def _lin_chunk(q, k, v, log_g, st_ref, h, tri, causal):
    b = _cumsum_rows(tri, log_g)
    b_last = b[LIN_CHUNK - 1:LIN_CHUNK, :]
    q_dec = (q * jnp.exp(b)).astype(BF16)
    k_dec = (k * jnp.exp(-b)).astype(BF16)
    attn = jnp.where(causal, _dot_nt(q_dec, k_dec), 0.0).astype(BF16)
    vb = v.astype(BF16)
    st = st_ref[h]
    o = _dot(attn, vb) + _dot_nt(q_dec, st.astype(BF16))
    k_tail = (k * jnp.exp(b_last - b)).astype(BF16)
    st_ref[h] = st * jnp.exp(b_last) + _dot_tn(vb, k_tail)
    return o


def _head_norm_gate(o, gain, g):
    o = o * lax.rsqrt(jnp.mean(o * o, axis=-1, keepdims=True) + RMS_EPS) * gain
    return o * _silu(g)


def _tri_causal():
    r = lax.broadcasted_iota(jnp.int32, (LIN_CHUNK, LIN_CHUNK), 0)
    c = lax.broadcasted_iota(jnp.int32, (LIN_CHUNK, LIN_CHUNK), 1)
    causal = r >= c
    return causal.astype(BF16), causal


def _gla_kernel(q_ref, k_ref, v_ref, g_ref, a_ref, wd_ref, bd_ref, gain_ref, o_ref, st_ref):
    @pl.when(pl.program_id(1) == 0)
    def _():
        st_ref[...] = jnp.zeros(st_ref.shape, F32)

    tri, causal = _tri_causal()
    K, V = GLA_KEY_DIM, GLA_VAL_DIM

    def body(c, carry):
        rows = pl.ds(pl.multiple_of(c * LIN_CHUNK, LIN_CHUNK), LIN_CHUNK)
        z = _dot_split(a_ref[rows, :], wd_ref[...]) + bd_ref[...]
        log_alpha = (jnp.minimum(z, 0.0) - jnp.log(1.0 + jnp.exp(-jnp.abs(z)))) * (1.0 / GLA_GATE_TEMP)
        for h in range(GLA_HEADS):
            q = q_ref[rows, h * K:(h + 1) * K] * (K ** -0.5)
            k = k_ref[rows, h * K:(h + 1) * K]
            v = v_ref[rows, h * V:(h + 1) * V]
            o = _lin_chunk(q, k, v, log_alpha[:, h * K:(h + 1) * K], st_ref, h, tri, causal)
            o_ref[rows, h * V:(h + 1) * V] = _head_norm_gate(o, gain_ref[...], g_ref[rows, h * V:(h + 1) * V])
        return carry

    lax.fori_loop(0, LIN_ROWS // LIN_CHUNK, body, 0)


def _gla_core(proj, wd_pad, bd, gain, batch, seq):
    n = proj.shape[0]
    nt = seq // LIN_ROWS
    dk = GLA_HEADS * GLA_KEY_DIM
    dv = GLA_HEADS * GLA_VAL_DIM
    rows = lambda c: (lambda b, t: (b * nt + t, c))
    const = lambda b, t: (0, 0)
    return pl.pallas_call(
        _gla_kernel,
        out_shape=jax.ShapeDtypeStruct((n, dv), F32),
        grid=(batch, nt),
        in_specs=[
            pl.BlockSpec((LIN_ROWS, dk), rows(0)),
            pl.BlockSpec((LIN_ROWS, dk), rows(1)),
            pl.BlockSpec((LIN_ROWS, dv), rows(1)),
            pl.BlockSpec((LIN_ROWS, dv), rows(2)),
            pl.BlockSpec((LIN_ROWS, LANES), rows((2 * dk + 2 * dv) // LANES)),
            pl.BlockSpec((LANES, dk), const),
            pl.BlockSpec((1, dk), const),
            pl.BlockSpec((1, GLA_VAL_DIM), const),
        ],
        out_specs=pl.BlockSpec((LIN_ROWS, dv), rows(0)),
        scratch_shapes=[pltpu.VMEM((GLA_HEADS, GLA_VAL_DIM, GLA_KEY_DIM), F32)],
        compiler_params=_params(("parallel", "arbitrary")),
        name="gla_core",
    )(proj, proj, proj, proj, proj, wd_pad, bd.reshape(1, dk), gain.reshape(1, GLA_VAL_DIM))


def _hgrn_kernel(q_ref, f_ref, i_ref, g_ref, lbl_ref, gain_ref, o_ref, st_ref, *, layer):
    @pl.when(pl.program_id(1) == 0)
    def _():
        st_ref[...] = jnp.zeros(st_ref.shape, F32)

    tri, causal = _tri_causal()
    K, V = HGRN_KEY_DIM, HGRN_VAL_DIM
    logits = lbl_ref[...]
    e = jnp.exp(logits - jnp.max(logits, axis=0, keepdims=True))
    p = e / jnp.sum(e, axis=0, keepdims=True)
    lb = jnp.zeros((1, logits.shape[1]), F32)
    for l in range(1, layer + 1):
        lb = lb + p[l:l + 1, :]

    def body(c, carry):
        rows = pl.ds(pl.multiple_of(c * LIN_CHUNK, LIN_CHUNK), LIN_CHUNK)
        for h in range(HGRN_HEADS):
            cols = slice(h * K, (h + 1) * K)
            q = _silu(q_ref[rows, cols]) * (K ** -0.5)
            lbh = lb[:, cols]
            f = lbh + (1.0 - lbh) * _sigmoid(f_ref[rows, cols])
            o = _lin_chunk(q, 1.0 - f, i_ref[rows, cols], jnp.log(f), st_ref, h, tri, causal)
            o_ref[rows, cols] = _head_norm_gate(o, gain_ref[...], g_ref[rows, cols])
        return carry

    lax.fori_loop(0, LIN_ROWS // LIN_CHUNK, body, 0)


def _hgrn_core(proj, lb_logits, gain, layer, batch, seq):
    n = proj.shape[0]
    nt = seq // LIN_ROWS
    d = D_MODEL
    rows = lambda c: (lambda b, t: (b * nt + t, c))
    const = lambda b, t: (0, 0)
    return pl.pallas_call(
        functools.partial(_hgrn_kernel, layer=layer),
        out_shape=jax.ShapeDtypeStruct((n, d), F32),
        grid=(batch, nt),
        in_specs=[
            pl.BlockSpec((LIN_ROWS, d), rows(0)),
            pl.BlockSpec((LIN_ROWS, d), rows(1)),
            pl.BlockSpec((LIN_ROWS, d), rows(2)),
            pl.BlockSpec((LIN_ROWS, d), rows(3)),
            pl.BlockSpec(lb_logits.shape, const),
            pl.BlockSpec((1, HGRN_VAL_DIM), const),
        ],
        out_specs=pl.BlockSpec((LIN_ROWS, d), rows(0)),
        scratch_shapes=[pltpu.VMEM((HGRN_HEADS, HGRN_VAL_DIM, HGRN_KEY_DIM), F32)],
        compiler_params=_params(("parallel", "arbitrary")),
        name="hgrn_core",
    )(proj, proj, proj, proj, lb_logits, gain.reshape(1, HGRN_VAL_DIM))


def kernel(x, norm_mix, norm_ffn, swa_w_in, swa_sinks, swa_w_out, moba_w_in, moba_w_out, gla_w_in, gla_w_decay_up, gla_b_decay, gla_out_norm, gla_w_out, hgrn_w_in, hgrn_lb_logits, hgrn_out_norm, hgrn_w_out, ffn_w_gate_up, ffn_w_down, final_norm):
    batch, seq, d = x.shape
    depth = norm_mix.shape[0]
    xf = x.reshape(batch * seq, d)
    for i in range(depth):
        kind, j = i % N_MIXERS, i // N_MIXERS
        if kind == 0:
            proj = _norm_proj(xf, norm_mix[i], _moba_slot_weights(swa_w_in[j]).astype(BF16), MOBA_WIDTH // 2)
            o = _swa_core(proj, swa_sinks[j], batch, seq)
            w_out = swa_w_out[j]
        elif kind == 1:
            proj = _norm_proj(xf, norm_mix[i], _moba_slot_weights(moba_w_in[j]).astype(BF16), MOBA_WIDTH // 2)
            o = _moba_core(proj, batch, seq)
            w_out = moba_w_out[j]
        elif kind == 2:
            pad = LANES - GLA_GATE_RANK
            w_in = jnp.pad(gla_w_in[j], ((0, 0), (0, pad))).astype(BF16)
            wd_pad = jnp.pad(gla_w_decay_up[j], ((0, pad), (0, 0)))
            proj = _norm_proj(xf, norm_mix[i], w_in, 640)
            o = _gla_core(proj, wd_pad, gla_b_decay[j], gla_out_norm[j], batch, seq)
            w_out = gla_w_out[j]
        else:
            proj = _norm_proj(xf, norm_mix[i], hgrn_w_in[j].astype(BF16), 1024)
            o = _hgrn_core(proj, hgrn_lb_logits, hgrn_out_norm[j], i, batch, seq)
            w_out = hgrn_w_out[j]
        last = i == depth - 1
        xf = _ffn(xf, o, w_out.astype(BF16), norm_ffn[i], ffn_w_gate_up[i].astype(BF16),
                  ffn_w_down[i].astype(BF16), final_norm, last)
    return xf.reshape(batch, seq, d)
```

```python
import functools
import math

import jax
import jax.numpy as jnp
from jax import lax
from jax.experimental import pallas as pl
from jax.experimental.pallas import tpu as pltpu

F32 = jnp.float32
BF16 = jnp.bfloat16

D_MODEL = 1024
HEAD_DIM = 64
N_Q_HEADS = 16
N_KV_HEADS = 4
GQA_GROUP = 4
ATTN_Q_DIM = 1024
ATTN_KV_DIM = 256
SWA_BLOCK = 128
SWA_WINDOW = 128
MOBA_BLOCK = 256
MOBA_TOP_K = 3
GLA_HEADS = 4
GLA_KEY_DIM = 128
GLA_VAL_DIM = 256
GLA_GATE_RANK = 16
GLA_GATE_TEMP = 16.0
HGRN_HEADS = 8
HGRN_KEY_DIM = 128
HGRN_VAL_DIM = 128
LIN_CHUNK = 64
D_FF = 2816
RMS_EPS = 1e-6
N_MIXERS = 4

LANES = 128
VMEM_LIMIT = 56 * 1024 * 1024
FFN_CHUNK = 256
ROW_TILE = 512
LIN_ROWS = 256

ALIBI_SLOPES = tuple(2.0 ** (-8.0 * (i + 1) / N_Q_HEADS) for i in range(N_Q_HEADS))
NEG_INF = float("-inf")


def _params(sem):
    return pltpu.CompilerParams(dimension_semantics=sem, vmem_limit_bytes=VMEM_LIMIT)


def _dot(a, b):
    return jnp.dot(a, b, preferred_element_type=F32)


def _dot_nt(a, b):
    return lax.dot_general(a, b, (((1,), (1,)), ((), ())), preferred_element_type=F32)


def _dot_tn(a, b):
    return lax.dot_general(a, b, (((0,), (0,)), ((), ())), preferred_element_type=F32)


def _split(x):
    hi = x.astype(BF16)
    lo = (x - hi.astype(F32)).astype(BF16)
    return hi, lo


def _dot_split(a, b):
    ah, al = _split(a)
    bh, bl = _split(b)
    return _dot(ah, bh) + (_dot(ah, bl) + _dot(al, bh))


def _rms(x, gain):
    return x * lax.rsqrt(jnp.mean(x * x, axis=-1, keepdims=True) + RMS_EPS) * gain


def _sigmoid(x):
    return 1.0 / (1.0 + jnp.exp(-x))


def _silu(x):
    return x * _sigmoid(x)


def _norm_proj_kernel(x_ref, g_ref, w_ref, o_ref, h_ref):
    @pl.when(pl.program_id(1) == 0)
    def _():
        h_ref[...] = _rms(x_ref[...], g_ref[...]).astype(BF16)

    o_ref[...] = _dot(h_ref[...], w_ref[...])


def _norm_proj(x, gain, w, tn):
    n, d = x.shape
    dout = w.shape[1]
    return pl.pallas_call(
        _norm_proj_kernel,
        out_shape=jax.ShapeDtypeStruct((n, dout), F32),
        grid=(n // ROW_TILE, dout // tn),
        in_specs=[
            pl.BlockSpec((ROW_TILE, d), lambda i, j: (i, 0)),
            pl.BlockSpec((1, d), lambda i, j: (0, 0)),
            pl.BlockSpec((d, tn), lambda i, j: (0, j)),
        ],
        out_specs=pl.BlockSpec((ROW_TILE, tn), lambda i, j: (i, j)),
        scratch_shapes=[pltpu.VMEM((ROW_TILE, d), BF16)],
        compiler_params=_params(("parallel", "arbitrary")),
        name="norm_proj",
    )(x, gain.reshape(1, d), w)


def _ffn_kernel(x_ref, o_ref, wo_ref, g_ref, wgu_ref, wd_ref, fg_ref, out_ref,
                x1_ref, h_ref, acc_ref, *, final_norm):
    x1 = x_ref[...] + _dot(o_ref[...].astype(BF16), wo_ref[...])
    x1_ref[...] = x1
    h_ref[...] = _rms(x1, g_ref[...]).astype(BF16)
    for c in range(D_FF // FFN_CHUNK):
        h = h_ref[...]
        gate = _dot(h, wgu_ref[:, c * FFN_CHUNK:(c + 1) * FFN_CHUNK])
        up = _dot(h, wgu_ref[:, D_FF + c * FFN_CHUNK:D_FF + (c + 1) * FFN_CHUNK])
        act = (_silu(gate) * up).astype(BF16)
        part = _dot(act, wd_ref[c * FFN_CHUNK:(c + 1) * FFN_CHUNK, :])
        if c == 0:
            acc_ref[...] = part
        else:
            acc_ref[...] += part
    y = x1_ref[...] + acc_ref[...]
    if final_norm:
        y = _rms(y, fg_ref[...])
    out_ref[...] = y


def _ffn(x, o, wo, gain, wgu, wd, final_gain, final_norm):
    n, d = x.shape
    const = lambda i: (0, 0)
    row = lambda i: (i, 0)
    return pl.pallas_call(
        functools.partial(_ffn_kernel, final_norm=final_norm),
        out_shape=jax.ShapeDtypeStruct((n, d), F32),
        grid=(n // ROW_TILE,),
        in_specs=[
            pl.BlockSpec((ROW_TILE, d), row),
            pl.BlockSpec((ROW_TILE, d), row),
            pl.BlockSpec((d, d), const, pipeline_mode=pl.Buffered(1)),
            pl.BlockSpec((1, d), const),
            pl.BlockSpec((d, 2 * D_FF), const, pipeline_mode=pl.Buffered(1)),
            pl.BlockSpec((D_FF, d), const, pipeline_mode=pl.Buffered(1)),
            pl.BlockSpec((1, d), const),
        ],
        out_specs=pl.BlockSpec((ROW_TILE, d), row),
        scratch_shapes=[
            pltpu.VMEM((ROW_TILE, d), F32),
            pltpu.VMEM((ROW_TILE, d), BF16),
            pltpu.VMEM((ROW_TILE, d), F32),
        ],
        compiler_params=_params(("parallel",)),
        name="outproj_ffn",
    )(x, o, wo, gain.reshape(1, d), wgu, wd, final_gain.reshape(1, d))


def _swa_kernel(q_ref, kp_ref, ko_ref, vp_ref, vo_ref, sink_ref, tab_ref, o_ref, ot_ref):
    L = SWA_BLOCK
    n = pl.program_id(1)
    first = jnp.where(n == 0, 1, 0)
    qt = (q_ref[...] * (LOG2E * HEAD_DIM ** -0.5)).T.astype(BF16)
    k = jnp.concatenate([kp_ref[...], ko_ref[...]], axis=0).astype(BF16)
    vt = jnp.concatenate([vp_ref[...], vo_ref[...]], axis=0).T.astype(BF16)
    ones = jnp.ones((SUM_ROWS, 2 * L), BF16)
    sinks = sink_ref[...] * LOG2E
    for kv in range(N_KV_HEADS):
        heads = [kv * GQA_GROUP + g for g in range(GQA_GROUP)]
        qg = jnp.concatenate([qt[hd * MOBA_SLOT:(hd + 1) * MOBA_SLOT, :] for hd in heads], axis=1)
        s_all = _dot(k[:, kv * MOBA_SLOT:(kv + 1) * MOBA_SLOT], qg)
        probs, extras = [], []
        for g, hd in enumerate(heads):
            s = s_all[:, g * L:(g + 1) * L] + tab_ref[first, hd]
            sink = sinks[:, hd:hd + 1]
            m = jnp.maximum(jnp.max(s, axis=0, keepdims=True), sink)
            probs.append(jnp.exp2(s - m).astype(BF16))
            extras.append(jnp.exp2(sink - m))
        vaug = jnp.concatenate([vt[kv * HEAD_DIM:(kv + 1) * HEAD_DIM, :], ones], axis=0)
        pv_all = _dot(vaug, jnp.concatenate(probs, axis=1))
        for g, hd in enumerate(heads):
            pv = pv_all[:, g * L:(g + 1) * L]
            denom = pv[HEAD_DIM:HEAD_DIM + 1, :] + extras[g]
            ot_ref[hd * HEAD_DIM:(hd + 1) * HEAD_DIM, :] = pv[:HEAD_DIM, :] / denom
    o_ref[...] = ot_ref[...].T


def _swa_core(proj, sinks, batch, seq):
    n = proj.shape[0]
    L = SWA_BLOCK
    nb = seq // L
    kcol = MOBA_Q_WIDTH // MOBA_K_WIDTH
    vcol = (MOBA_Q_WIDTH + MOBA_K_WIDTH) // ATTN_KV_DIM
    own = lambda c: (lambda b, i: (b * nb + i, c))
    prev = lambda c: (lambda b, i: (b * nb + jnp.maximum(i - 1, 0), c))
    key = jnp.arange(2 * L)[:, None]
    qry = jnp.arange(L)[None, :]
    dist = (L + qry) - key
    band = (dist >= 0) & (dist < SWA_WINDOW)
    slopes = (LOG2E * jnp.asarray(ALIBI_SLOPES, F32))[:, None, None]
    bias = -slopes * dist.astype(F32)[None]
    tab = jnp.stack([jnp.where(band[None], bias, MASKED),
                     jnp.where((band & (key >= L))[None], bias, MASKED)])
    return pl.pallas_call(
        _swa_kernel,
        out_shape=jax.ShapeDtypeStruct((n, ATTN_Q_DIM), F32),
        grid=(batch, nb),
        in_specs=[
            pl.BlockSpec((L, MOBA_Q_WIDTH), own(0)),
            pl.BlockSpec((L, MOBA_K_WIDTH), prev(kcol)),
            pl.BlockSpec((L, MOBA_K_WIDTH), own(kcol)),
            pl.BlockSpec((L, ATTN_KV_DIM), prev(vcol)),
            pl.BlockSpec((L, ATTN_KV_DIM), own(vcol)),
            pl.BlockSpec((1, N_Q_HEADS), lambda b, i: (0, 0)),
            pl.BlockSpec((2, N_Q_HEADS, 2 * L, L), lambda b, i: (0, 0, 0, 0), pipeline_mode=pl.Buffered(1)),
        ],
        out_specs=pl.BlockSpec((L, ATTN_Q_DIM), own(0)),
        scratch_shapes=[pltpu.VMEM((ATTN_Q_DIM, L), F32)],
        compiler_params=_params(("parallel", "parallel")),
        name="swa_core",
    )(proj, proj, proj, proj, proj, sinks.reshape(1, N_Q_HEADS), tab)


MOBA_SLOT = 2 * HEAD_DIM
PEN_LANES = 32
SLOPE_LANES = (HEAD_DIM + PEN_LANES, HEAD_DIM + PEN_LANES + 1)
MOBA_Q_WIDTH = N_Q_HEADS * MOBA_SLOT
MOBA_K_WIDTH = N_KV_HEADS * MOBA_SLOT
MOBA_WIDTH = MOBA_Q_WIDTH + MOBA_K_WIDTH + ATTN_KV_DIM
MASKED = -1e30
LOG2E = math.log2(math.e)
SUM_ROWS = 16
MOBA_KEY_GROUP = 8
MOBA_UNROLL = 2

_KMEAN_BLOCKS = 8


def _kmean_kernel(k_ref, o_ref):
    k = k_ref[...].reshape(_KMEAN_BLOCKS, MOBA_BLOCK, MOBA_K_WIDTH)
    o_ref[...] = jnp.sum(k, axis=1) * (1.0 / MOBA_BLOCK)


def _moba_kmean(proj):
    n = proj.shape[0]
    rows = _KMEAN_BLOCKS * MOBA_BLOCK
    return pl.pallas_call(
        _kmean_kernel,
        out_shape=jax.ShapeDtypeStruct((n // MOBA_BLOCK, MOBA_K_WIDTH), F32),
        grid=(n // rows,),
        in_specs=[pl.BlockSpec((rows, MOBA_K_WIDTH), lambda i: (i, MOBA_Q_WIDTH // MOBA_K_WIDTH))],
        out_specs=pl.BlockSpec((_KMEAN_BLOCKS, MOBA_K_WIDTH), lambda i: (i, 0)),
        compiler_params=_params(("parallel",)),
        name="moba_kmean",
    )(proj)


def _moba_select_kernel(q_ref, km_ref, expand_ref, slope_ref, k_ref, v_ref, qa_ref, ks_ref, vt_ref, *, nblk):
    n = pl.program_id(1)
    q = q_ref[...] * (HEAD_DIM ** -0.5)
    gate = _dot_split(q, km_ref[0])
    lane_i = lax.broadcasted_iota(jnp.int32, gate.shape, 1)
    blk = jnp.bitwise_and(lane_i, nblk - 1)
    head = jnp.right_shift(lane_i, int(math.log2(nblk)))
    lane = lane_i.astype(F32)
    gate = jnp.where(blk < n, gate, NEG_INF)
    sel = jnp.zeros(gate.shape, F32)
    for h in range(N_KV_HEADS):
        g = jnp.where(head == h, gate, NEG_INF)
        for _ in range(MOBA_TOP_K):
            mx = jnp.max(g, axis=-1, keepdims=True)
            is_max = (g == mx) & (mx > NEG_INF)
            first = jnp.min(jnp.where(is_max, lane, float(4 * LANES)), axis=-1, keepdims=True)
            pick = lane == first
            sel = jnp.where(pick, 1.0, sel)
            g = jnp.where(pick, NEG_INF, g)
    unselected = (1.0 - sel).astype(BF16)
    q_feat = q * LOG2E + slope_ref[...]
    for var in range(qa_ref.shape[0]):
        penalty = _dot(unselected, expand_ref[var])
        qa_ref[var] = (q_feat + penalty).T.astype(BF16)
    ks_ref[0] = k_ref[...].astype(BF16)
    vt_ref[0] = v_ref[...].T.astype(BF16)


def _moba_select(proj, km_mat, expand, slope_feat, batch, seq):
    n = proj.shape[0]
    nblk = seq // MOBA_BLOCK
    width = N_KV_HEADS * nblk
    nvar = expand.shape[0]
    vcol = (MOBA_Q_WIDTH + MOBA_K_WIDTH) // ATTN_KV_DIM
    return pl.pallas_call(
        functools.partial(_moba_select_kernel, nblk=nblk),
        out_shape=(jax.ShapeDtypeStruct((nvar, MOBA_Q_WIDTH, n), BF16),
                   jax.ShapeDtypeStruct((n // MOBA_BLOCK, MOBA_BLOCK, MOBA_K_WIDTH), BF16),
                   jax.ShapeDtypeStruct((n // MOBA_BLOCK, ATTN_KV_DIM, MOBA_BLOCK), BF16)),
        grid=(batch, nblk),
        in_specs=[
            pl.BlockSpec((MOBA_BLOCK, MOBA_Q_WIDTH), lambda b, i: (b * nblk + i, 0)),
            pl.BlockSpec((1, MOBA_Q_WIDTH, width), lambda b, i: (b, 0, 0)),
            pl.BlockSpec((nvar, width, MOBA_Q_WIDTH), lambda b, i: (0, 0, 0)),
            pl.BlockSpec((1, MOBA_Q_WIDTH), lambda b, i: (0, 0)),
            pl.BlockSpec((MOBA_BLOCK, MOBA_K_WIDTH), lambda b, i: (b * nblk + i, MOBA_Q_WIDTH // MOBA_K_WIDTH)),
            pl.BlockSpec((MOBA_BLOCK, ATTN_KV_DIM), lambda b, i: (b * nblk + i, vcol)),
        ],
        out_specs=(pl.BlockSpec((nvar, MOBA_Q_WIDTH, MOBA_BLOCK), lambda b, i: (0, 0, b * nblk + i)),
                   pl.BlockSpec((1, MOBA_BLOCK, MOBA_K_WIDTH), lambda b, i: (b * nblk + i, 0, 0)),
                   pl.BlockSpec((1, ATTN_KV_DIM, MOBA_BLOCK), lambda b, i: (b * nblk + i, 0, 0))),
        compiler_params=_params(("parallel", "parallel")),
        name="moba_select",
    )(proj, km_mat, expand, slope_feat, proj, proj)


def _moba_scores(ka, qa_ref, var, kv):
    heads = [kv * GQA_GROUP + g for g in range(GQA_GROUP)]
    qt = jnp.concatenate([qa_ref[var, hd * MOBA_SLOT:(hd + 1) * MOBA_SLOT, :] for hd in heads], axis=1)
    return _dot(ka, qt)


def _moba_blocks_update(blocks, qa_ref, var, m_ref, l_ref, acc_ref):
    ones = jnp.ones((SUM_ROWS, MOBA_BLOCK), BF16)
    items = [(blk, kv) for blk in blocks for kv in range(N_KV_HEADS)]
    scores = lambda item: _moba_scores(item[0][0](item[1]), qa_ref, var, item[1])
    s_next = scores(items[0])
    for idx, (blk, kv) in enumerate(items):
        s_all = s_next
        if idx + 1 < len(items):
            s_next = scores(items[idx + 1])
        _, vt_all, offset_blocks, causal = blk
        vt = jnp.concatenate([vt_all(kv), ones], axis=0)
        _moba_softmax_pv(s_all, vt, m_ref, l_ref, acc_ref, kv, offset_blocks, causal)


def _moba_softmax_pv(s_all, vt, m_ref, l_ref, acc_ref, kv, offset_blocks, causal):
    Lb = MOBA_BLOCK
    heads = [kv * GQA_GROUP + g for g in range(GQA_GROUP)]
    probs, alphas = [], []
    for g, hd in enumerate(heads):
        off = (LOG2E * ALIBI_SLOPES[hd] * MOBA_BLOCK) * offset_blocks
        s = s_all[:, g * Lb:(g + 1) * Lb]
        if causal is not None:
            s = jnp.where(causal, s, MASKED)
        m_old = m_ref[hd:hd + 1, :]
        m_new = jnp.maximum(m_old, jnp.max(s, axis=0, keepdims=True) + off)
        alphas.append(jnp.exp2(m_old - m_new))
        probs.append(jnp.exp2(s - (m_new - off)).astype(BF16))
        m_ref[hd:hd + 1, :] = m_new
    pv_all = _dot(vt, jnp.concatenate(probs, axis=1))
    for g, hd in enumerate(heads):
        pv = pv_all[:, g * Lb:(g + 1) * Lb]
        rows = slice(hd * HEAD_DIM, (hd + 1) * HEAD_DIM)
        l_ref[hd:hd + 1, :] = alphas[g] * l_ref[hd:hd + 1, :] + pv[HEAD_DIM:HEAD_DIM + 1, :]
        acc_ref[rows, :] = alphas[g] * acc_ref[rows, :] + pv[:HEAD_DIM, :]


def _moba_attn_kernel(nq_ref, jg_ref, qa_ref, k_ref, vt_ref, o_ref, m_ref, l_ref, acc_ref, *, group):
    Lb = MOBA_BLOCK
    step = pl.program_id(1)
    n = nq_ref[step]
    first = jg_ref[step] * group
    var = first // PEN_LANES

    @pl.when(first == 0)
    def _():
        m_ref[...] = jnp.full(m_ref.shape, NEG_INF, F32)
        l_ref[...] = jnp.zeros(l_ref.shape, F32)
        acc_ref[...] = jnp.zeros(acc_ref.shape, F32)

    lane = lax.broadcasted_iota(jnp.int32, (Lb, MOBA_SLOT), 1)
    key_off = (lax.broadcasted_iota(jnp.int32, (Lb, MOBA_SLOT), 0) - (Lb - 1)).astype(F32)
    slope_lanes = (lane == SLOPE_LANES[0]) | (lane == SLOPE_LANES[1])
    key_feat = jnp.where(slope_lanes, key_off, 0.0)

    def past_block(i):
        j = first + i
        extra = jnp.where(lane == HEAD_DIM + (j - var * PEN_LANES), 1.0, key_feat).astype(BF16)
        return (lambda kv: k_ref[i, :, kv * MOBA_SLOT:(kv + 1) * MOBA_SLOT] + extra,
                lambda kv: vt_ref[i, kv * HEAD_DIM:(kv + 1) * HEAD_DIM, :],
                (j - n).astype(F32), None)

    def past_blocks(count):
        def body(it, carry):
            blocks = [past_block(carry + u) for u in range(count)]
            _moba_blocks_update(blocks, qa_ref, var, m_ref, l_ref, acc_ref)
            return carry + count
        return body

    n_past = jnp.minimum(n - first, group)
    n_multi = n_past // MOBA_UNROLL
    done = lax.fori_loop(0, n_multi, past_blocks(MOBA_UNROLL), 0)
    lax.fori_loop(0, n_past - done, past_blocks(1), done)

    @pl.when(n - first < group)
    def _():
        i = n - first
        key = lax.broadcasted_iota(jnp.int32, (Lb, Lb), 0)
        qry = lax.broadcasted_iota(jnp.int32, (Lb, Lb), 1)
        causal = key <= qry
        extra = key_feat.astype(BF16)
        own = (lambda kv: k_ref[i, :, kv * MOBA_SLOT:(kv + 1) * MOBA_SLOT] + extra,
               lambda kv: vt_ref[i, kv * HEAD_DIM:(kv + 1) * HEAD_DIM, :], 0.0, causal)
        _moba_blocks_update([own], qa_ref, var, m_ref, l_ref, acc_ref)
        inv = 1.0 / l_ref[...]
        acc = acc_ref[...].reshape(N_Q_HEADS, HEAD_DIM, Lb) * inv[:, None, :]
        o_ref[...] = acc.reshape(N_Q_HEADS * HEAD_DIM, Lb).T


def _moba_attn(qaug, kslot, vt, batch, seq):
    n = batch * seq
    nblk = seq // MOBA_BLOCK
    group = min(MOBA_KEY_GROUP, nblk)
    assert nblk % group == 0
    pairs = [(i, jg) for i in range(nblk) for jg in range(i // group + 1)]
    nq = jnp.asarray([p[0] for p in pairs], jnp.int32)
    jk = jnp.asarray([p[1] for p in pairs], jnp.int32)
    ngrp = nblk // group
    nvar = qaug.shape[0]
    assert PEN_LANES % group == 0, "a key group must not straddle two query variants"
    qmap = lambda b, s, nq_r, jk_r: (b * nblk + nq_r[s], 0)
    kmap = lambda b, s, nq_r, jk_r: (b * ngrp + jk_r[s], 0, 0)
    grid_spec = pltpu.PrefetchScalarGridSpec(
        num_scalar_prefetch=2,
        grid=(batch, len(pairs)),
        in_specs=[
            pl.BlockSpec((nvar, MOBA_Q_WIDTH, MOBA_BLOCK), lambda b, s, nq_r, jk_r: (0, 0, b * nblk + nq_r[s])),
            pl.BlockSpec((group, MOBA_BLOCK, MOBA_K_WIDTH), kmap),
            pl.BlockSpec((group, ATTN_KV_DIM, MOBA_BLOCK), kmap),
        ],
        out_specs=pl.BlockSpec((MOBA_BLOCK, ATTN_Q_DIM), qmap),
        scratch_shapes=[
            pltpu.VMEM((N_Q_HEADS, MOBA_BLOCK), F32),
            pltpu.VMEM((N_Q_HEADS, MOBA_BLOCK), F32),
            pltpu.VMEM((N_Q_HEADS * HEAD_DIM, MOBA_BLOCK), F32),
        ],
    )
    return pl.pallas_call(
        functools.partial(_moba_attn_kernel, group=group),
        out_shape=jax.ShapeDtypeStruct((n, ATTN_Q_DIM), F32),
        grid_spec=grid_spec,
        compiler_params=_params(("parallel", "arbitrary")),
        name="moba_attn",
    )(nq, jk, qaug, kslot, vt)


def _moba_slot_weights(w_in):
    d = w_in.shape[0]
    fill = MOBA_SLOT - HEAD_DIM
    wq = w_in[:, :ATTN_Q_DIM].reshape(d, N_Q_HEADS, HEAD_DIM)
    wk = w_in[:, ATTN_Q_DIM:ATTN_Q_DIM + ATTN_KV_DIM].reshape(d, N_KV_HEADS, HEAD_DIM)
    wq = jnp.pad(wq, ((0, 0), (0, 0), (0, fill))).reshape(d, MOBA_Q_WIDTH)
    wk = jnp.pad(wk, ((0, 0), (0, 0), (0, fill))).reshape(d, MOBA_K_WIDTH)
    return jnp.concatenate([wq, wk, w_in[:, ATTN_Q_DIM + ATTN_KV_DIM:]], axis=1)


def _moba_core(proj, batch, seq):
    nblk = seq // MOBA_BLOCK
    nvar = -(-nblk // PEN_LANES)
    width = N_KV_HEADS * nblk
    kmean = _moba_kmean(proj)
    km = kmean.reshape(batch, nblk, N_KV_HEADS, MOBA_SLOT).transpose(0, 2, 3, 1)
    eye = jnp.eye(N_KV_HEADS, dtype=F32)
    km_mat = jnp.einsum('bhej,hH->bheHj', km, eye)
    km_mat = jnp.broadcast_to(km_mat[:, :, None], (batch, N_KV_HEADS, GQA_GROUP, MOBA_SLOT, N_KV_HEADS, nblk))
    km_mat = km_mat.reshape(batch, MOBA_Q_WIDTH, width)
    h_of = jnp.arange(width) // nblk
    c_of = jnp.arange(width) % nblk
    slot = jnp.arange(MOBA_Q_WIDTH) // MOBA_SLOT
    lane = jnp.arange(MOBA_Q_WIDTH) % MOBA_SLOT
    hit = (h_of[:, None] == (slot // GQA_GROUP)[None, :]) & (lane[None, :] == HEAD_DIM + (c_of % PEN_LANES)[:, None])
    in_var = (c_of // PEN_LANES)[None, :, None] == jnp.arange(nvar)[:, None, None]
    expand = jnp.where(hit[None] & in_var, MASKED, 0.0).astype(BF16)
    sigma = LOG2E * jnp.asarray(ALIBI_SLOPES, F32)
    hi = sigma.astype(BF16).astype(F32)
    lo = (sigma - hi).astype(BF16).astype(F32)
    slope_feat = (jnp.where(lane == SLOPE_LANES[0], hi[slot], 0.0)
                  + jnp.where(lane == SLOPE_LANES[1], lo[slot], 0.0)).reshape(1, MOBA_Q_WIDTH)
    qaug, kslot, vt = _moba_select(proj, km_mat, expand, slope_feat, batch, seq)
    return _moba_attn(qaug, kslot, vt, batch, seq)


LIN_CHUNKS = LIN_ROWS // LIN_CHUNK


def _lin_masks(key_dim):
    R = LIN_ROWS
    chunk_bits = int(math.log2(LIN_CHUNK))
    key_bits = int(math.log2(key_dim))
    assert (1 << chunk_bits) == LIN_CHUNK and (1 << key_bits) == key_dim
    r = lax.broadcasted_iota(jnp.int32, (R, R), 0)
    c = lax.broadcasted_iota(jnp.int32, (R, R), 1)
    same = jnp.right_shift(r, chunk_bits) == jnp.right_shift(c, chunk_bits)
    causal = same & (r >= c)
    sums = jnp.concatenate([jnp.where(causal, 1.0, 0.0), jnp.where(same, 1.0, 0.0)], axis=0).astype(BF16)
    rr = lax.broadcasted_iota(jnp.int32, (R, LIN_CHUNKS * key_dim), 0)
    cc = lax.broadcasted_iota(jnp.int32, (R, LIN_CHUNKS * key_dim), 1)
    own = jnp.right_shift(rr, chunk_bits) == jnp.right_shift(cc, key_bits)
    return sums, causal, own


def _lin_tile(q, k, v, log_g, st_ref, h, masks):
    sums, causal, own = masks
    R, K = q.shape
    hi = log_g.astype(BF16)
    r1 = log_g - hi.astype(F32)
    mid = r1.astype(BF16)
    lo = (r1 - mid.astype(F32)).astype(BF16)
    acc = _dot(sums, jnp.concatenate([hi, mid, lo], axis=1))
    acc = acc[:, :K] + (acc[:, K:2 * K] + acc[:, 2 * K:])
    b, b_chunk = acc[:R], acc[R:]
    q_dec = (q * jnp.exp(b)).astype(BF16)
    k_dec = (k * jnp.exp(-b)).astype(BF16)
    k_tail = (k * jnp.exp(b_chunk - b)).astype(BF16)
    vb = v.astype(BF16)
    attn = jnp.where(causal, _dot_nt(q_dec, k_dec), 0.0).astype(BF16)
    o = _dot(attn, vb)
    tile_lanes = lambda x: jnp.concatenate([x] * LIN_CHUNKS, axis=1)
    zero = jnp.zeros((), BF16)
    ds = _dot_tn(vb, jnp.where(own, tile_lanes(k_tail), zero))
    st = st_ref[h]
    states = []
    for c in range(LIN_CHUNKS):
        states.append(st.astype(BF16))
        decay = jnp.exp(b_chunk[c * LIN_CHUNK:c * LIN_CHUNK + 1, :])
        st = st * decay + ds[:, c * K:(c + 1) * K]
    st_ref[h] = st
    o = o + _dot_nt(jnp.where(own, tile_lanes(q_dec), zero), jnp.concatenate(states, axis=1))
    return o


def _head_norm_gate(o, gain, g):
    o = o * lax.rsqrt(jnp.mean(o * o, axis=-1, keepdims=True) + RMS_EPS) * gain
    return o * _silu(g)


def _gla_kernel(q_ref, k_ref, v_ref, g_ref, a_ref, wd_ref, bd_ref, gain_ref, o_ref, st_ref):
    @pl.when(pl.program_id(1) == 0)
    def _():
        st_ref[...] = jnp.zeros(st_ref.shape, F32)

    K, V = GLA_KEY_DIM, GLA_VAL_DIM
    masks = _lin_masks(K)
    z = _dot_split(a_ref[...], wd_ref[...]) + bd_ref[...]
    log_alpha = (jnp.minimum(z, 0.0) - jnp.log(1.0 + jnp.exp(-jnp.abs(z)))) * (1.0 / GLA_GATE_TEMP)
    for h in range(GLA_HEADS):
        q = q_ref[:, h * K:(h + 1) * K] * (K ** -0.5)
        k = k_ref[:, h * K:(h + 1) * K]
        v = v_ref[:, h * V:(h + 1) * V]
        o = _lin_tile(q, k, v, log_alpha[:, h * K:(h + 1) * K], st_ref, h, masks)
        o_ref[:, h * V:(h + 1) * V] = _head_norm_gate(o, gain_ref[...], g_ref[:, h * V:(h + 1) * V])


def _gla_core(proj, wd_pad, bd, gain, batch, seq):
    n = proj.shape[0]
    nt = seq // LIN_ROWS
    dk = GLA_HEADS * GLA_KEY_DIM
    dv = GLA_HEADS * GLA_VAL_DIM
    rows = lambda c: (lambda b, t: (b * nt + t, c))
    const = lambda b, t: (0, 0)
    return pl.pallas_call(
        _gla_kernel,
        out_shape=jax.ShapeDtypeStruct((n, dv), F32),
        grid=(batch, nt),
        in_specs=[
            pl.BlockSpec((LIN_ROWS, dk), rows(0)),
            pl.BlockSpec((LIN_ROWS, dk), rows(1)),
            pl.BlockSpec((LIN_ROWS, dv), rows(1)),
            pl.BlockSpec((LIN_ROWS, dv), rows(2)),
            pl.BlockSpec((LIN_ROWS, LANES), rows((2 * dk + 2 * dv) // LANES)),
            pl.BlockSpec((LANES, dk), const),
            pl.BlockSpec((1, dk), const),
            pl.BlockSpec((1, GLA_VAL_DIM), const),
        ],
        out_specs=pl.BlockSpec((LIN_ROWS, dv), rows(0)),
        scratch_shapes=[pltpu.VMEM((GLA_HEADS, GLA_VAL_DIM, GLA_KEY_DIM), F32)],
        compiler_params=_params(("parallel", "arbitrary")),
        name="gla_core",
    )(proj, proj, proj, proj, proj, wd_pad, bd.reshape(1, dk), gain.reshape(1, GLA_VAL_DIM))


def _hgrn_kernel(q_ref, f_ref, i_ref, g_ref, lbl_ref, gain_ref, o_ref, st_ref, *, layer):
    @pl.when(pl.program_id(1) == 0)
    def _():
        st_ref[...] = jnp.zeros(st_ref.shape, F32)

    K, V = HGRN_KEY_DIM, HGRN_VAL_DIM
    masks = _lin_masks(K)
    logits = lbl_ref[...]
    e = jnp.exp(logits - jnp.max(logits, axis=0, keepdims=True))
    p = e / jnp.sum(e, axis=0, keepdims=True)
    lb = jnp.zeros((1, logits.shape[1]), F32)
    for l in range(1, layer + 1):
        lb = lb + p[l:l + 1, :]

    for h in range(HGRN_HEADS):
        cols = slice(h * K, (h + 1) * K)
        q = _silu(q_ref[:, cols]) * (K ** -0.5)
        lbh = lb[:, cols]
        f = lbh + (1.0 - lbh) * _sigmoid(f_ref[:, cols])
        o = _lin_tile(q, 1.0 - f, i_ref[:, cols], jnp.log(f), st_ref, h, masks)
        o_ref[:, cols] = _head_norm_gate(o, gain_ref[...], g_ref[:, cols])


def _hgrn_core(proj, lb_logits, gain, layer, batch, seq):
    n = proj.shape[0]
    nt = seq // LIN_ROWS
    d = D_MODEL
    rows = lambda c: (lambda b, t: (b * nt + t, c))
    const = lambda b, t: (0, 0)
    return pl.pallas_call(
        functools.partial(_hgrn_kernel, layer=layer),
        out_shape=jax.ShapeDtypeStruct((n, d), F32),
        grid=(batch, nt),
        in_specs=[
            pl.BlockSpec((LIN_ROWS, d), rows(0)),
            pl.BlockSpec((LIN_ROWS, d), rows(1)),
            pl.BlockSpec((LIN_ROWS, d), rows(2)),
            pl.BlockSpec((LIN_ROWS, d), rows(3)),
            pl.BlockSpec(lb_logits.shape, const),
            pl.BlockSpec((1, HGRN_VAL_DIM), const),
        ],
        out_specs=pl.BlockSpec((LIN_ROWS, d), rows(0)),
        scratch_shapes=[pltpu.VMEM((HGRN_HEADS, HGRN_VAL_DIM, HGRN_KEY_DIM), F32)],
        compiler_params=_params(("parallel", "arbitrary")),
        name="hgrn_core",
    )(proj, proj, proj, proj, lb_logits, gain.reshape(1, HGRN_VAL_DIM))


def kernel(x, norm_mix, norm_ffn, swa_w_in, swa_sinks, swa_w_out, moba_w_in, moba_w_out, gla_w_in, gla_w_decay_up, gla_b_decay, gla_out_norm, gla_w_out, hgrn_w_in, hgrn_lb_logits, hgrn_out_norm, hgrn_w_out, ffn_w_gate_up, ffn_w_down, final_norm):
    batch, seq, d = x.shape
    depth = norm_mix.shape[0]
    xf = x.reshape(batch * seq, d)
    for i in range(depth):
        kind, j = i % N_MIXERS, i // N_MIXERS
        if kind == 0:
            proj = _norm_proj(xf, norm_mix[i], _moba_slot_weights(swa_w_in[j]).astype(BF16), MOBA_WIDTH // 2)
            o = _swa_core(proj, swa_sinks[j], batch, seq)
            w_out = swa_w_out[j]
        elif kind == 1:
            proj = _norm_proj(xf, norm_mix[i], _moba_slot_weights(moba_w_in[j]).astype(BF16), MOBA_WIDTH // 2)
            o = _moba_core(proj, batch, seq)
            w_out = moba_w_out[j]
        elif kind == 2:
            pad = LANES - GLA_GATE_RANK
            w_in = jnp.pad(gla_w_in[j], ((0, 0), (0, pad))).astype(BF16)
            wd_pad = jnp.pad(gla_w_decay_up[j], ((0, pad), (0, 0)))
            proj = _norm_proj(xf, norm_mix[i], w_in, 640)
            o = _gla_core(proj, wd_pad, gla_b_decay[j], gla_out_norm[j], batch, seq)
            w_out = gla_w_out[j]
        else:
            proj = _norm_proj(xf, norm_mix[i], hgrn_w_in[j].astype(BF16), 1024)
            o = _hgrn_core(proj, hgrn_lb_logits, hgrn_out_norm[j], i, batch, seq)
            w_out = hgrn_w_out[j]
        last = i == depth - 1
        xf = _ffn(xf, o, w_out.astype(BF16), norm_ffn[i], ffn_w_gate_up[i].astype(BF16),
                  ffn_w_down[i].astype(BF16), final_norm, last)
    return xf.reshape(batch, seq, d)
```

```python
import functools
import math

import jax
import jax.numpy as jnp
from jax import lax
from jax.experimental import pallas as pl
from jax.experimental.pallas import tpu as pltpu

F32 = jnp.float32
BF16 = jnp.bfloat16

D_MODEL = 1024
HEAD_DIM = 64
N_Q_HEADS = 16
N_KV_HEADS = 4
GQA_GROUP = 4
ATTN_Q_DIM = 1024
ATTN_KV_DIM = 256
SWA_BLOCK = 128
SWA_WINDOW = 128
MOBA_BLOCK = 256
MOBA_TOP_K = 3
GLA_HEADS = 4
GLA_KEY_DIM = 128
GLA_VAL_DIM = 256
GLA_GATE_RANK = 16
GLA_GATE_TEMP = 16.0
HGRN_HEADS = 8
HGRN_KEY_DIM = 128
HGRN_VAL_DIM = 128
LIN_CHUNK = 64
D_FF = 2816
RMS_EPS = 1e-6
N_MIXERS = 4

LANES = 128
VMEM_LIMIT = 56 * 1024 * 1024
FFN_CHUNK = 256
ROW_TILE = 512
PROJ_ROWS = 1024
LIN_ROWS = 256

ALIBI_SLOPES = tuple(2.0 ** (-8.0 * (i + 1) / N_Q_HEADS) for i in range(N_Q_HEADS))
NEG_INF = float("-inf")


def _params(sem):
    return pltpu.CompilerParams(dimension_semantics=sem, vmem_limit_bytes=VMEM_LIMIT)


def _dot(a, b):
    return jnp.dot(a, b, preferred_element_type=F32)


def _dot_nt(a, b):
    return lax.dot_general(a, b, (((1,), (1,)), ((), ())), preferred_element_type=F32)


def _dot_tn(a, b):
    return lax.dot_general(a, b, (((0,), (0,)), ((), ())), preferred_element_type=F32)


def _split(x):
    hi = x.astype(BF16)
    lo = (x - hi.astype(F32)).astype(BF16)
    return hi, lo


def _dot_split(a, b):
    ah, al = _split(a)
    bh, bl = _split(b)
    return _dot(ah, bh) + (_dot(ah, bl) + _dot(al, bh))


def _rms(x, gain):
    return x * lax.rsqrt(jnp.mean(x * x, axis=-1, keepdims=True) + RMS_EPS) * gain


def _sigmoid(x):
    return 1.0 / (1.0 + jnp.exp(-x))


def _silu(x):
    return x * _sigmoid(x)


def _norm_proj_kernel(x_ref, g_ref, w_ref, o_ref, h_ref):
    @pl.when(pl.program_id(1) == 0)
    def _():
        h_ref[...] = _rms(x_ref[...], g_ref[...]).astype(BF16)

    o_ref[...] = _dot(h_ref[...], w_ref[...])


def _norm_proj(x, gain, w, tn):
    n, d = x.shape
    dout = w.shape[1]
    return pl.pallas_call(
        _norm_proj_kernel,
        out_shape=jax.ShapeDtypeStruct((n, dout), F32),
        grid=(n // PROJ_ROWS, dout // tn),
        in_specs=[
            pl.BlockSpec((PROJ_ROWS, d), lambda i, j: (i, 0)),
            pl.BlockSpec((1, d), lambda i, j: (0, 0)),
            pl.BlockSpec((d, tn), lambda i, j: (0, j)),
        ],
        out_specs=pl.BlockSpec((PROJ_ROWS, tn), lambda i, j: (i, j)),
        scratch_shapes=[pltpu.VMEM((PROJ_ROWS, d), BF16)],
        compiler_params=_params(("parallel", "arbitrary")),
        name="norm_proj",
    )(x, gain.reshape(1, d), w)


def _ffn_kernel(x_ref, o_ref, wo_ref, g_ref, wgu_ref, wd_ref, fg_ref, out_ref,
                x1_ref, h_ref, acc_ref, *, final_norm):
    x1 = x_ref[...] + _dot(o_ref[...].astype(BF16), wo_ref[...])
    x1_ref[...] = x1
    h_ref[...] = _rms(x1, g_ref[...]).astype(BF16)
    for c in range(D_FF // FFN_CHUNK):
        h = h_ref[...]
        gate = _dot(h, wgu_ref[:, c * FFN_CHUNK:(c + 1) * FFN_CHUNK])
        up = _dot(h, wgu_ref[:, D_FF + c * FFN_CHUNK:D_FF + (c + 1) * FFN_CHUNK])
        act = (_silu(gate) * up).astype(BF16)
        part = _dot(act, wd_ref[c * FFN_CHUNK:(c + 1) * FFN_CHUNK, :])
        if c == 0:
            acc_ref[...] = part
        else:
            acc_ref[...] += part
    y = x1_ref[...] + acc_ref[...]
    if final_norm:
        y = _rms(y, fg_ref[...])
    out_ref[...] = y


def _ffn(x, o, wo, gain, wgu, wd, final_gain, final_norm):
    n, d = x.shape
    const = lambda i: (0, 0)
    row = lambda i: (i, 0)
    return pl.pallas_call(
        functools.partial(_ffn_kernel, final_norm=final_norm),
        out_shape=jax.ShapeDtypeStruct((n, d), F32),
        grid=(n // ROW_TILE,),
        in_specs=[
            pl.BlockSpec((ROW_TILE, d), row),
            pl.BlockSpec((ROW_TILE, d), row),
            pl.BlockSpec((d, d), const, pipeline_mode=pl.Buffered(1)),
            pl.BlockSpec((1, d), const),
            pl.BlockSpec((d, 2 * D_FF), const, pipeline_mode=pl.Buffered(1)),
            pl.BlockSpec((D_FF, d), const, pipeline_mode=pl.Buffered(1)),
            pl.BlockSpec((1, d), const),
        ],
        out_specs=pl.BlockSpec((ROW_TILE, d), row),
        scratch_shapes=[
            pltpu.VMEM((ROW_TILE, d), F32),
            pltpu.VMEM((ROW_TILE, d), BF16),
            pltpu.VMEM((ROW_TILE, d), F32),
        ],
        compiler_params=_params(("parallel",)),
        name="outproj_ffn",
    )(x, o, wo, gain.reshape(1, d), wgu, wd, final_gain.reshape(1, d))


def _swa_kernel(q_ref, kp_ref, ko_ref, vp_ref, vo_ref, sink_ref, tab_ref, o_ref, ot_ref):
    L = SWA_BLOCK
    n = pl.program_id(1)
    first = jnp.where(n == 0, 1, 0)
    qt = (q_ref[...] * (LOG2E * HEAD_DIM ** -0.5)).T.astype(BF16)
    k = jnp.concatenate([kp_ref[...], ko_ref[...]], axis=0).astype(BF16)
    vt = jnp.concatenate([vp_ref[...], vo_ref[...]], axis=0).T.astype(BF16)
    ones = jnp.ones((SUM_ROWS, 2 * L), BF16)
    sinks = sink_ref[...] * LOG2E
    for kv in range(N_KV_HEADS):
        heads = [kv * GQA_GROUP + g for g in range(GQA_GROUP)]
        qg = jnp.concatenate([qt[hd * MOBA_SLOT:(hd + 1) * MOBA_SLOT, :] for hd in heads], axis=1)
        s_all = _dot(k[:, kv * MOBA_SLOT:(kv + 1) * MOBA_SLOT], qg)
        probs, extras = [], []
        for g, hd in enumerate(heads):
            s = s_all[:, g * L:(g + 1) * L] + tab_ref[first, hd]
            sink = sinks[:, hd:hd + 1]
            m = jnp.maximum(jnp.max(s, axis=0, keepdims=True), sink)
            probs.append(jnp.exp2(s - m).astype(BF16))
            extras.append(jnp.exp2(sink - m))
        vaug = jnp.concatenate([vt[kv * HEAD_DIM:(kv + 1) * HEAD_DIM, :], ones], axis=0)
        pv_all = _dot(vaug, jnp.concatenate(probs, axis=1))
        for g, hd in enumerate(heads):
            pv = pv_all[:, g * L:(g + 1) * L]
            denom = pv[HEAD_DIM:HEAD_DIM + 1, :] + extras[g]
            ot_ref[hd * HEAD_DIM:(hd + 1) * HEAD_DIM, :] = pv[:HEAD_DIM, :] / denom
    o_ref[...] = ot_ref[...].T


def _swa_core(proj, sinks, batch, seq):
    n = proj.shape[0]
    L = SWA_BLOCK
    nb = seq // L
    kcol = MOBA_Q_WIDTH // MOBA_K_WIDTH
    vcol = (MOBA_Q_WIDTH + MOBA_K_WIDTH) // ATTN_KV_DIM
    own = lambda c: (lambda b, i: (b * nb + i, c))
    prev = lambda c: (lambda b, i: (b * nb + jnp.maximum(i - 1, 0), c))
    key = jnp.arange(2 * L)[:, None]
    qry = jnp.arange(L)[None, :]
    dist = (L + qry) - key
    band = (dist >= 0) & (dist < SWA_WINDOW)
    slopes = (LOG2E * jnp.asarray(ALIBI_SLOPES, F32))[:, None, None]
    bias = -slopes * dist.astype(F32)[None]
    tab = jnp.stack([jnp.where(band[None], bias, MASKED),
                     jnp.where((band & (key >= L))[None], bias, MASKED)])
    return pl.pallas_call(
        _swa_kernel,
        out_shape=jax.ShapeDtypeStruct((n, ATTN_Q_DIM), F32),
        grid=(batch, nb),
        in_specs=[
            pl.BlockSpec((L, MOBA_Q_WIDTH), own(0)),
            pl.BlockSpec((L, MOBA_K_WIDTH), prev(kcol)),
            pl.BlockSpec((L, MOBA_K_WIDTH), own(kcol)),
            pl.BlockSpec((L, ATTN_KV_DIM), prev(vcol)),
            pl.BlockSpec((L, ATTN_KV_DIM), own(vcol)),
            pl.BlockSpec((1, N_Q_HEADS), lambda b, i: (0, 0)),
            pl.BlockSpec((2, N_Q_HEADS, 2 * L, L), lambda b, i: (0, 0, 0, 0), pipeline_mode=pl.Buffered(1)),
        ],
        out_specs=pl.BlockSpec((L, ATTN_Q_DIM), own(0)),
        scratch_shapes=[pltpu.VMEM((ATTN_Q_DIM, L), F32)],
        compiler_params=_params(("parallel", "parallel")),
        name="swa_core",
    )(proj, proj, proj, proj, proj, sinks.reshape(1, N_Q_HEADS), tab)


MOBA_SLOT = 2 * HEAD_DIM
PEN_LANES = 32
SLOPE_LANES = (HEAD_DIM + PEN_LANES, HEAD_DIM + PEN_LANES + 1)
STRIDE_LANES = (HEAD_DIM + PEN_LANES + 2, HEAD_DIM + PEN_LANES + 3)
MOBA_Q_WIDTH = N_Q_HEADS * MOBA_SLOT
MOBA_K_WIDTH = N_KV_HEADS * MOBA_SLOT
MOBA_WIDTH = MOBA_Q_WIDTH + MOBA_K_WIDTH + ATTN_KV_DIM
MASKED = -1e30
LOG2E = math.log2(math.e)
SUM_ROWS = 16
MOBA_KEY_GROUP = 8
MOBA_UNROLL = 2

_KMEAN_BLOCKS = 8


def _kmean_kernel(k_ref, o_ref):
    k = k_ref[...].reshape(_KMEAN_BLOCKS, MOBA_BLOCK, MOBA_K_WIDTH)
    o_ref[...] = jnp.sum(k, axis=1) * (1.0 / MOBA_BLOCK)


def _moba_kmean(proj):
    n = proj.shape[0]
    rows = _KMEAN_BLOCKS * MOBA_BLOCK
    return pl.pallas_call(
        _kmean_kernel,
        out_shape=jax.ShapeDtypeStruct((n // MOBA_BLOCK, MOBA_K_WIDTH), F32),
        grid=(n // rows,),
        in_specs=[pl.BlockSpec((rows, MOBA_K_WIDTH), lambda i: (i, MOBA_Q_WIDTH // MOBA_K_WIDTH))],
        out_specs=pl.BlockSpec((_KMEAN_BLOCKS, MOBA_K_WIDTH), lambda i: (i, 0)),
        compiler_params=_params(("parallel",)),
        name="moba_kmean",
    )(proj)


def _moba_select_kernel(q_ref, km_ref, expand_ref, slope_ref, k_ref, v_ref, qa_ref, ks_ref, vt_ref, *, nblk):
    n = pl.program_id(1)
    q = q_ref[...] * (HEAD_DIM ** -0.5)
    gate = _dot_split(q, km_ref[0])
    lane_i = lax.broadcasted_iota(jnp.int32, gate.shape, 1)
    blk = jnp.bitwise_and(lane_i, nblk - 1)
    head = jnp.right_shift(lane_i, int(math.log2(nblk)))
    lane = lane_i.astype(F32)
    gate = jnp.where(blk < n, gate, NEG_INF)
    sel = jnp.zeros(gate.shape, F32)
    for h in range(N_KV_HEADS):
        g = jnp.where(head == h, gate, NEG_INF)
        for _ in range(MOBA_TOP_K):
            mx = jnp.max(g, axis=-1, keepdims=True)
            is_max = (g == mx) & (mx > NEG_INF)
            first = jnp.min(jnp.where(is_max, lane, float(4 * LANES)), axis=-1, keepdims=True)
            pick = lane == first
            sel = jnp.where(pick, 1.0, sel)
            g = jnp.where(pick, NEG_INF, g)
    unselected = (1.0 - sel).astype(BF16)
    q_feat = q * LOG2E + slope_ref[...]
    for var in range(qa_ref.shape[0]):
        penalty = _dot(unselected, expand_ref[var])
        qa_ref[var] = (q_feat + penalty).T.astype(BF16)
    ks_ref[0] = k_ref[...].astype(BF16)
    vt_ref[0] = v_ref[...].T.astype(BF16)


def _moba_select(proj, km_mat, expand, slope_feat, batch, seq):
    n = proj.shape[0]
    nblk = seq // MOBA_BLOCK
    width = N_KV_HEADS * nblk
    nvar = expand.shape[0]
    vcol = (MOBA_Q_WIDTH + MOBA_K_WIDTH) // ATTN_KV_DIM
    return pl.pallas_call(
        functools.partial(_moba_select_kernel, nblk=nblk),
        out_shape=(jax.ShapeDtypeStruct((nvar, MOBA_Q_WIDTH, n), BF16),
                   jax.ShapeDtypeStruct((n // MOBA_BLOCK, MOBA_BLOCK, MOBA_K_WIDTH), BF16),
                   jax.ShapeDtypeStruct((n // MOBA_BLOCK, ATTN_KV_DIM, MOBA_BLOCK), BF16)),
        grid=(batch, nblk),
        in_specs=[
            pl.BlockSpec((MOBA_BLOCK, MOBA_Q_WIDTH), lambda b, i: (b * nblk + i, 0)),
            pl.BlockSpec((1, MOBA_Q_WIDTH, width), lambda b, i: (b, 0, 0)),
            pl.BlockSpec((nvar, width, MOBA_Q_WIDTH), lambda b, i: (0, 0, 0)),
            pl.BlockSpec((1, MOBA_Q_WIDTH), lambda b, i: (0, 0)),
            pl.BlockSpec((MOBA_BLOCK, MOBA_K_WIDTH), lambda b, i: (b * nblk + i, MOBA_Q_WIDTH // MOBA_K_WIDTH)),
            pl.BlockSpec((MOBA_BLOCK, ATTN_KV_DIM), lambda b, i: (b * nblk + i, vcol)),
        ],
        out_specs=(pl.BlockSpec((nvar, MOBA_Q_WIDTH, MOBA_BLOCK), lambda b, i: (0, 0, b * nblk + i)),
                   pl.BlockSpec((1, MOBA_BLOCK, MOBA_K_WIDTH), lambda b, i: (b * nblk + i, 0, 0)),
                   pl.BlockSpec((1, ATTN_KV_DIM, MOBA_BLOCK), lambda b, i: (b * nblk + i, 0, 0))),
        compiler_params=_params(("parallel", "parallel")),
        name="moba_select",
    )(proj, km_mat, expand, slope_feat, proj, proj)


def _moba_scores(ka, qa_ref, var, kv):
    heads = [kv * GQA_GROUP + g for g in range(GQA_GROUP)]
    qt = jnp.concatenate([qa_ref[var, hd * MOBA_SLOT:(hd + 1) * MOBA_SLOT, :] for hd in heads], axis=1)
    return _dot(ka, qt)


def _moba_blocks_update(blocks, qa_ref, var, m_ref, l_ref, acc_ref, causal=None):
    keys = lambda kv: jnp.concatenate([blk[0](kv) for blk in blocks], axis=0)
    s_next = _moba_scores(keys(0), qa_ref, var, 0)
    for kv in range(N_KV_HEADS):
        s_all = s_next
        if kv + 1 < N_KV_HEADS:
            s_next = _moba_scores(keys(kv + 1), qa_ref, var, kv + 1)
        vt = jnp.concatenate([blk[1](kv) for blk in blocks], axis=1)
        vt = jnp.concatenate([vt, jnp.ones((SUM_ROWS, vt.shape[1]), BF16)], axis=0)
        _moba_softmax_pv(s_all, vt, m_ref, l_ref, acc_ref, kv, causal)


def _moba_softmax_pv(s_all, vt, m_ref, l_ref, acc_ref, kv, causal):
    Lb = MOBA_BLOCK
    heads = [kv * GQA_GROUP + g for g in range(GQA_GROUP)]
    probs, alphas = [], []
    for g, hd in enumerate(heads):
        s = s_all[:, g * Lb:(g + 1) * Lb]
        if causal is not None:
            s = jnp.where(causal, s, MASKED)
        m_old = m_ref[hd:hd + 1, :]
        m_new = jnp.maximum(m_old, jnp.max(s, axis=0, keepdims=True))
        alphas.append(jnp.exp2(m_old - m_new))
        probs.append(jnp.exp2(s - m_new).astype(BF16))
        m_ref[hd:hd + 1, :] = m_new
    pv_all = _dot(vt, jnp.concatenate(probs, axis=1))
    for g, hd in enumerate(heads):
        pv = pv_all[:, g * Lb:(g + 1) * Lb]
        rows = slice(hd * HEAD_DIM, (hd + 1) * HEAD_DIM)
        l_ref[hd:hd + 1, :] = alphas[g] * l_ref[hd:hd + 1, :] + pv[HEAD_DIM:HEAD_DIM + 1, :]
        acc_ref[rows, :] = alphas[g] * acc_ref[rows, :] + pv[:HEAD_DIM, :]


def _moba_attn_kernel(nq_ref, jg_ref, qa_ref, k_ref, vt_ref, o_ref, m_ref, l_ref, acc_ref, *, group):
    Lb = MOBA_BLOCK
    step = pl.program_id(1)
    n = nq_ref[step]
    first = jg_ref[step] * group
    var = first // PEN_LANES

    @pl.when(first == 0)
    def _():
        m_ref[...] = jnp.full(m_ref.shape, NEG_INF, F32)
        l_ref[...] = jnp.zeros(l_ref.shape, F32)
        acc_ref[...] = jnp.zeros(acc_ref.shape, F32)

    lane = lax.broadcasted_iota(jnp.int32, (Lb, MOBA_SLOT), 1)
    key_off = (lax.broadcasted_iota(jnp.int32, (Lb, MOBA_SLOT), 0) - (Lb - 1)).astype(F32)
    slope_lanes = (lane == SLOPE_LANES[0]) | (lane == SLOPE_LANES[1])
    stride_lanes = (lane == STRIDE_LANES[0]) | (lane == STRIDE_LANES[1])
    key_feat = jnp.where(slope_lanes, key_off, 0.0)

    def past_block(i):
        j = first + i
        extra = jnp.where(stride_lanes, (j - n).astype(F32), key_feat)
        extra = jnp.where(lane == HEAD_DIM + (j - var * PEN_LANES), 1.0, extra).astype(BF16)
        return (lambda kv: k_ref[i, :, kv * MOBA_SLOT:(kv + 1) * MOBA_SLOT] + extra,
                lambda kv: vt_ref[i, kv * HEAD_DIM:(kv + 1) * HEAD_DIM, :])

    def past_blocks(count):
        def body(it, carry):
            blocks = [past_block(carry + u) for u in range(count)]
            _moba_blocks_update(blocks, qa_ref, var, m_ref, l_ref, acc_ref)
            return carry + count
        return body

    n_past = jnp.minimum(n - first, group)
    n_multi = n_past // MOBA_UNROLL
    done = lax.fori_loop(0, n_multi, past_blocks(MOBA_UNROLL), 0)
    lax.fori_loop(0, n_past - done, past_blocks(1), done)

    @pl.when(n - first < group)
    def _():
        i = n - first
        key = lax.broadcasted_iota(jnp.int32, (Lb, Lb), 0)
        qry = lax.broadcasted_iota(jnp.int32, (Lb, Lb), 1)
        causal = key <= qry
        extra = key_feat.astype(BF16)
        own = (lambda kv: k_ref[i, :, kv * MOBA_SLOT:(kv + 1) * MOBA_SLOT] + extra,
               lambda kv: vt_ref[i, kv * HEAD_DIM:(kv + 1) * HEAD_DIM, :])
        _moba_blocks_update([own], qa_ref, var, m_ref, l_ref, acc_ref, causal)
        inv = 1.0 / l_ref[...]
        acc = acc_ref[...].reshape(N_Q_HEADS, HEAD_DIM, Lb) * inv[:, None, :]
        o_ref[...] = acc.reshape(N_Q_HEADS * HEAD_DIM, Lb).T


def _moba_attn(qaug, kslot, vt, batch, seq):
    n = batch * seq
    nblk = seq // MOBA_BLOCK
    group = min(MOBA_KEY_GROUP, nblk)
    assert nblk % group == 0
    pairs = [(i, jg) for i in range(nblk) for jg in range(i // group + 1)]
    nq = jnp.asarray([p[0] for p in pairs], jnp.int32)
    jk = jnp.asarray([p[1] for p in pairs], jnp.int32)
    ngrp = nblk // group
    nvar = qaug.shape[0]
    assert PEN_LANES % group == 0, "a key group must not straddle two query variants"
    qmap = lambda b, s, nq_r, jk_r: (b * nblk + nq_r[s], 0)
    kmap = lambda b, s, nq_r, jk_r: (b * ngrp + jk_r[s], 0, 0)
    grid_spec = pltpu.PrefetchScalarGridSpec(
        num_scalar_prefetch=2,
        grid=(batch, len(pairs)),
        in_specs=[
            pl.BlockSpec((nvar, MOBA_Q_WIDTH, MOBA_BLOCK), lambda b, s, nq_r, jk_r: (0, 0, b * nblk + nq_r[s])),
            pl.BlockSpec((group, MOBA_BLOCK, MOBA_K_WIDTH), kmap),
            pl.BlockSpec((group, ATTN_KV_DIM, MOBA_BLOCK), kmap),
        ],
        out_specs=pl.BlockSpec((MOBA_BLOCK, ATTN_Q_DIM), qmap),
        scratch_shapes=[
            pltpu.VMEM((N_Q_HEADS, MOBA_BLOCK), F32),
            pltpu.VMEM((N_Q_HEADS, MOBA_BLOCK), F32),
            pltpu.VMEM((N_Q_HEADS * HEAD_DIM, MOBA_BLOCK), F32),
        ],
    )
    return pl.pallas_call(
        functools.partial(_moba_attn_kernel, group=group),
        out_shape=jax.ShapeDtypeStruct((n, ATTN_Q_DIM), F32),
        grid_spec=grid_spec,
        compiler_params=_params(("parallel", "arbitrary")),
        name="moba_attn",
    )(nq, jk, qaug, kslot, vt)


def _moba_slot_weights(w_in):
    d = w_in.shape[0]
    fill = MOBA_SLOT - HEAD_DIM
    wq = w_in[:, :ATTN_Q_DIM].reshape(d, N_Q_HEADS, HEAD_DIM)
    wk = w_in[:, ATTN_Q_DIM:ATTN_Q_DIM + ATTN_KV_DIM].reshape(d, N_KV_HEADS, HEAD_DIM)
    wq = jnp.pad(wq, ((0, 0), (0, 0), (0, fill))).reshape(d, MOBA_Q_WIDTH)
    wk = jnp.pad(wk, ((0, 0), (0, 0), (0, fill))).reshape(d, MOBA_K_WIDTH)
    return jnp.concatenate([wq, wk, w_in[:, ATTN_Q_DIM + ATTN_KV_DIM:]], axis=1)


def _moba_core(proj, batch, seq):
    nblk = seq // MOBA_BLOCK
    nvar = -(-nblk // PEN_LANES)
    width = N_KV_HEADS * nblk
    kmean = _moba_kmean(proj)
    km = kmean.reshape(batch, nblk, N_KV_HEADS, MOBA_SLOT).transpose(0, 2, 3, 1)
    eye = jnp.eye(N_KV_HEADS, dtype=F32)
    km_mat = jnp.einsum('bhej,hH->bheHj', km, eye)
    km_mat = jnp.broadcast_to(km_mat[:, :, None], (batch, N_KV_HEADS, GQA_GROUP, MOBA_SLOT, N_KV_HEADS, nblk))
    km_mat = km_mat.reshape(batch, MOBA_Q_WIDTH, width)
    h_of = jnp.arange(width) // nblk
    c_of = jnp.arange(width) % nblk
    slot = jnp.arange(MOBA_Q_WIDTH) // MOBA_SLOT
    lane = jnp.arange(MOBA_Q_WIDTH) % MOBA_SLOT
    hit = (h_of[:, None] == (slot // GQA_GROUP)[None, :]) & (lane[None, :] == HEAD_DIM + (c_of % PEN_LANES)[:, None])
    in_var = (c_of // PEN_LANES)[None, :, None] == jnp.arange(nvar)[:, None, None]
    expand = jnp.where(hit[None] & in_var, MASKED, 0.0).astype(BF16)
    sigma = LOG2E * jnp.asarray(ALIBI_SLOPES, F32)
    slope_feat = jnp.zeros((MOBA_Q_WIDTH,), F32)
    for lanes, value in ((SLOPE_LANES, sigma), (STRIDE_LANES, sigma * MOBA_BLOCK)):
        hi = value.astype(BF16).astype(F32)
        lo = (value - hi).astype(BF16).astype(F32)
        slope_feat = slope_feat + jnp.where(lane == lanes[0], hi[slot], 0.0) + jnp.where(lane == lanes[1], lo[slot], 0.0)
    slope_feat = slope_feat.reshape(1, MOBA_Q_WIDTH)
    qaug, kslot, vt = _moba_select(proj, km_mat, expand, slope_feat, batch, seq)
    return _moba_attn(qaug, kslot, vt, batch, seq)


LIN_CHUNKS = LIN_ROWS // LIN_CHUNK


def _lin_masks(key_dim):
    R = LIN_ROWS
    chunk_bits = int(math.log2(LIN_CHUNK))
    key_bits = int(math.log2(key_dim))
    assert (1 << chunk_bits) == LIN_CHUNK and (1 << key_bits) == key_dim
    r = lax.broadcasted_iota(jnp.int32, (R, R), 0)
    c = lax.broadcasted_iota(jnp.int32, (R, R), 1)
    same = jnp.right_shift(r, chunk_bits) == jnp.right_shift(c, chunk_bits)
    causal = same & (r >= c)
    sums = jnp.concatenate([jnp.where(causal, 1.0, 0.0), jnp.where(same, 1.0, 0.0)], axis=0).astype(BF16)
    rr = lax.broadcasted_iota(jnp.int32, (R, LIN_CHUNKS * key_dim), 0)
    cc = lax.broadcasted_iota(jnp.int32, (R, LIN_CHUNKS * key_dim), 1)
    own = jnp.right_shift(rr, chunk_bits) == jnp.right_shift(cc, key_bits)
    return sums, causal, own


def _lin_tile(q, k, v, log_g, st_ref, h, masks):
    sums, causal, own = masks
    R, K = q.shape
    hi = log_g.astype(BF16)
    r1 = log_g - hi.astype(F32)
    mid = r1.astype(BF16)
    lo = (r1 - mid.astype(F32)).astype(BF16)
    acc = _dot(sums, jnp.concatenate([hi, mid, lo], axis=1))
    acc = acc[:, :K] + (acc[:, K:2 * K] + acc[:, 2 * K:])
    b, b_chunk = acc[:R], acc[R:]
    q_dec = (q * jnp.exp(b)).astype(BF16)
    k_dec = (k * jnp.exp(-b)).astype(BF16)
    k_tail = (k * jnp.exp(b_chunk - b)).astype(BF16)
    vb = v.astype(BF16)
    attn = jnp.where(causal, _dot_nt(q_dec, k_dec), 0.0).astype(BF16)
    o = _dot(attn, vb)
    tile_lanes = lambda x: jnp.concatenate([x] * LIN_CHUNKS, axis=1)
    zero = jnp.zeros((), BF16)
    ds = _dot_tn(vb, jnp.where(own, tile_lanes(k_tail), zero))
    st = st_ref[h]
    states = []
    for c in range(LIN_CHUNKS):
        states.append(st.astype(BF16))
        decay = jnp.exp(b_chunk[c * LIN_CHUNK:c * LIN_CHUNK + 1, :])
        st = st * decay + ds[:, c * K:(c + 1) * K]
    st_ref[h] = st
    o = o + _dot_nt(jnp.where(own, tile_lanes(q_dec), zero), jnp.concatenate(states, axis=1))
    return o


def _head_norm_gate(o, gain, g):
    o = o * lax.rsqrt(jnp.mean(o * o, axis=-1, keepdims=True) + RMS_EPS) * gain
    return o * _silu(g)


def _gla_kernel(q_ref, k_ref, v_ref, g_ref, a_ref, wd_ref, bd_ref, gain_ref, o_ref, st_ref):
    @pl.when(pl.program_id(1) == 0)
    def _():
        st_ref[...] = jnp.zeros(st_ref.shape, F32)

    K, V = GLA_KEY_DIM, GLA_VAL_DIM
    masks = _lin_masks(K)
    z = _dot_split(a_ref[...], wd_ref[...]) + bd_ref[...]
    log_alpha = (jnp.minimum(z, 0.0) - jnp.log(1.0 + jnp.exp(-jnp.abs(z)))) * (1.0 / GLA_GATE_TEMP)
    for h in range(GLA_HEADS):
        q = q_ref[:, h * K:(h + 1) * K] * (K ** -0.5)
        k = k_ref[:, h * K:(h + 1) * K]
        v = v_ref[:, h * V:(h + 1) * V]
        o = _lin_tile(q, k, v, log_alpha[:, h * K:(h + 1) * K], st_ref, h, masks)
        o_ref[:, h * V:(h + 1) * V] = _head_norm_gate(o, gain_ref[...], g_ref[:, h * V:(h + 1) * V])


def _gla_core(proj, wd_pad, bd, gain, batch, seq):
    n = proj.shape[0]
    nt = seq // LIN_ROWS
    dk = GLA_HEADS * GLA_KEY_DIM
    dv = GLA_HEADS * GLA_VAL_DIM
    rows = lambda c: (lambda b, t: (b * nt + t, c))
    const = lambda b, t: (0, 0)
    return pl.pallas_call(
        _gla_kernel,
        out_shape=jax.ShapeDtypeStruct((n, dv), F32),
        grid=(batch, nt),
        in_specs=[
            pl.BlockSpec((LIN_ROWS, dk), rows(0)),
            pl.BlockSpec((LIN_ROWS, dk), rows(1)),
            pl.BlockSpec((LIN_ROWS, dv), rows(1)),
            pl.BlockSpec((LIN_ROWS, dv), rows(2)),
            pl.BlockSpec((LIN_ROWS, LANES), rows((2 * dk + 2 * dv) // LANES)),
            pl.BlockSpec((LANES, dk), const),
            pl.BlockSpec((1, dk), const),
            pl.BlockSpec((1, GLA_VAL_DIM), const),
        ],
        out_specs=pl.BlockSpec((LIN_ROWS, dv), rows(0)),
        scratch_shapes=[pltpu.VMEM((GLA_HEADS, GLA_VAL_DIM, GLA_KEY_DIM), F32)],
        compiler_params=_params(("parallel", "arbitrary")),
        name="gla_core",
    )(proj, proj, proj, proj, proj, wd_pad, bd.reshape(1, dk), gain.reshape(1, GLA_VAL_DIM))


def _hgrn_kernel(q_ref, f_ref, i_ref, g_ref, lbl_ref, gain_ref, o_ref, st_ref, *, layer):
    @pl.when(pl.program_id(1) == 0)
    def _():
        st_ref[...] = jnp.zeros(st_ref.shape, F32)

    K, V = HGRN_KEY_DIM, HGRN_VAL_DIM
    masks = _lin_masks(K)
    logits = lbl_ref[...]
    e = jnp.exp(logits - jnp.max(logits, axis=0, keepdims=True))
    p = e / jnp.sum(e, axis=0, keepdims=True)
    lb = jnp.zeros((1, logits.shape[1]), F32)
    for l in range(1, layer + 1):
        lb = lb + p[l:l + 1, :]

    for h in range(HGRN_HEADS):
        cols = slice(h * K, (h + 1) * K)
        q = _silu(q_ref[:, cols]) * (K ** -0.5)
        lbh = lb[:, cols]
        f = lbh + (1.0 - lbh) * _sigmoid(f_ref[:, cols])
        o = _lin_tile(q, 1.0 - f, i_ref[:, cols], jnp.log(f), st_ref, h, masks)
        o_ref[:, cols] = _head_norm_gate(o, gain_ref[...], g_ref[:, cols])


def _hgrn_core(proj, lb_logits, gain, layer, batch, seq):
    n = proj.shape[0]
    nt = seq // LIN_ROWS
    d = D_MODEL
    rows = lambda c: (lambda b, t: (b * nt + t, c))
    const = lambda b, t: (0, 0)
    return pl.pallas_call(
        functools.partial(_hgrn_kernel, layer=layer),
        out_shape=jax.ShapeDtypeStruct((n, d), F32),
        grid=(batch, nt),
        in_specs=[
            pl.BlockSpec((LIN_ROWS, d), rows(0)),
            pl.BlockSpec((LIN_ROWS, d), rows(1)),
            pl.BlockSpec((LIN_ROWS, d), rows(2)),
            pl.BlockSpec((LIN_ROWS, d), rows(3)),
            pl.BlockSpec(lb_logits.shape, const),
            pl.BlockSpec((1, HGRN_VAL_DIM), const),
        ],
        out_specs=pl.BlockSpec((LIN_ROWS, d), rows(0)),
        scratch_shapes=[pltpu.VMEM((HGRN_HEADS, HGRN_VAL_DIM, HGRN_KEY_DIM), F32)],
        compiler_params=_params(("parallel", "arbitrary")),
        name="hgrn_core",
    )(proj, proj, proj, proj, lb_logits, gain.reshape(1, HGRN_VAL_DIM))


def kernel(x, norm_mix, norm_ffn, swa_w_in, swa_sinks, swa_w_out, moba_w_in, moba_w_out, gla_w_in, gla_w_decay_up, gla_b_decay, gla_out_norm, gla_w_out, hgrn_w_in, hgrn_lb_logits, hgrn_out_norm, hgrn_w_out, ffn_w_gate_up, ffn_w_down, final_norm):
    batch, seq, d = x.shape
    depth = norm_mix.shape[0]
    xf = x.reshape(batch * seq, d)
    for i in range(depth):
        kind, j = i % N_MIXERS, i // N_MIXERS
        if kind == 0:
            proj = _norm_proj(xf, norm_mix[i], _moba_slot_weights(swa_w_in[j]).astype(BF16), MOBA_WIDTH // 2)
            o = _swa_core(proj, swa_sinks[j], batch, seq)
            w_out = swa_w_out[j]
        elif kind == 1:
            proj = _norm_proj(xf, norm_mix[i], _moba_slot_weights(moba_w_in[j]).astype(BF16), MOBA_WIDTH // 2)
            o = _moba_core(proj, batch, seq)
            w_out = moba_w_out[j]
        elif kind == 2:
            pad = LANES - GLA_GATE_RANK
            w_in = jnp.pad(gla_w_in[j], ((0, 0), (0, pad))).astype(BF16)
            wd_pad = jnp.pad(gla_w_decay_up[j], ((0, pad), (0, 0)))
            proj = _norm_proj(xf, norm_mix[i], w_in, 640)
            o = _gla_core(proj, wd_pad, gla_b_decay[j], gla_out_norm[j], batch, seq)
            w_out = gla_w_out[j]
        else:
            proj = _norm_proj(xf, norm_mix[i], hgrn_w_in[j].astype(BF16), 1024)
            o = _hgrn_core(proj, hgrn_lb_logits, hgrn_out_norm[j], i, batch, seq)
            w_out = hgrn_w_out[j]
        last = i == depth - 1
        xf = _ffn(xf, o, w_out.astype(BF16), norm_ffn[i], ffn_w_gate_up[i].astype(BF16),
                  ffn_w_down[i].astype(BF16), final_norm, last)
    return xf.reshape(batch, seq, d)
```

```python
import functools
import math

import jax
import jax.numpy as jnp
from jax import lax
from jax.experimental import pallas as pl
from jax.experimental.pallas import tpu as pltpu

F32 = jnp.float32
BF16 = jnp.bfloat16

D_MODEL = 1024
HEAD_DIM = 64
N_Q_HEADS = 16
N_KV_HEADS = 4
GQA_GROUP = 4
ATTN_Q_DIM = 1024
ATTN_KV_DIM = 256
SWA_BLOCK = 128
SWA_WINDOW = 128
MOBA_BLOCK = 256
MOBA_TOP_K = 3
GLA_HEADS = 4
GLA_KEY_DIM = 128
GLA_VAL_DIM = 256
GLA_GATE_RANK = 16
GLA_GATE_TEMP = 16.0
HGRN_HEADS = 8
HGRN_KEY_DIM = 128
HGRN_VAL_DIM = 128
LIN_CHUNK = 64
D_FF = 2816
RMS_EPS = 1e-6
N_MIXERS = 4

LANES = 128
VMEM_LIMIT = 56 * 1024 * 1024
FFN_CHUNK = 256
ROW_TILE = 512
PROJ_ROWS = 1024
LIN_ROWS = 256

ALIBI_SLOPES = tuple(2.0 ** (-8.0 * (i + 1) / N_Q_HEADS) for i in range(N_Q_HEADS))
NEG_INF = float("-inf")


def _params(sem):
    return pltpu.CompilerParams(dimension_semantics=sem, vmem_limit_bytes=VMEM_LIMIT)


def _dot(a, b):
    return jnp.dot(a, b, preferred_element_type=F32)


def _dot_nt(a, b):
    return lax.dot_general(a, b, (((1,), (1,)), ((), ())), preferred_element_type=F32)


def _dot_tn(a, b):
    return lax.dot_general(a, b, (((0,), (0,)), ((), ())), preferred_element_type=F32)


def _split(x):
    hi = x.astype(BF16)
    lo = (x - hi.astype(F32)).astype(BF16)
    return hi, lo


def _dot_split(a, b):
    ah, al = _split(a)
    bh, bl = _split(b)
    return _dot(ah, bh) + (_dot(ah, bl) + _dot(al, bh))


def _rms(x, gain):
    return x * lax.rsqrt(jnp.mean(x * x, axis=-1, keepdims=True) + RMS_EPS) * gain


def _sigmoid(x):
    return 1.0 / (1.0 + jnp.exp(-x))


def _silu(x):
    return x * _sigmoid(x)


def _norm_proj_kernel(x_ref, g_ref, w_ref, o_ref, h_ref):
    @pl.when(pl.program_id(1) == 0)
    def _():
        h_ref[...] = _rms(x_ref[...], g_ref[...]).astype(BF16)

    o_ref[...] = _dot(h_ref[...], w_ref[...])


def _norm_proj(x, gain, w, tn):
    n, d = x.shape
    dout = w.shape[1]
    return pl.pallas_call(
        _norm_proj_kernel,
        out_shape=jax.ShapeDtypeStruct((n, dout), F32),
        grid=(n // PROJ_ROWS, dout // tn),
        in_specs=[
            pl.BlockSpec((PROJ_ROWS, d), lambda i, j: (i, 0)),
            pl.BlockSpec((1, d), lambda i, j: (0, 0)),
            pl.BlockSpec((d, tn), lambda i, j: (0, j)),
        ],
        out_specs=pl.BlockSpec((PROJ_ROWS, tn), lambda i, j: (i, j)),
        scratch_shapes=[pltpu.VMEM((PROJ_ROWS, d), BF16)],
        compiler_params=_params(("parallel", "arbitrary")),
        name="norm_proj",
    )(x, gain.reshape(1, d), w)


def _ffn_kernel(x_ref, o_ref, wo_ref, g_ref, wgu_ref, wd_ref, fg_ref, out_ref,
                x1_ref, h_ref, acc_ref, *, final_norm):
    x1 = x_ref[...] + _dot(o_ref[...].astype(BF16), wo_ref[...])
    x1_ref[...] = x1
    h_ref[...] = _rms(x1, g_ref[...]).astype(BF16)
    for c in range(D_FF // FFN_CHUNK):
        h = h_ref[...]
        gate = _dot(h, wgu_ref[:, c * FFN_CHUNK:(c + 1) * FFN_CHUNK])
        up = _dot(h, wgu_ref[:, D_FF + c * FFN_CHUNK:D_FF + (c + 1) * FFN_CHUNK])
        act = (_silu(gate) * up).astype(BF16)
        part = _dot(act, wd_ref[c * FFN_CHUNK:(c + 1) * FFN_CHUNK, :])
        if c == 0:
            acc_ref[...] = part
        else:
            acc_ref[...] += part
    y = x1_ref[...] + acc_ref[...]
    if final_norm:
        y = _rms(y, fg_ref[...])
    out_ref[...] = y


def _ffn(x, o, wo, gain, wgu, wd, final_gain, final_norm):
    n, d = x.shape
    const = lambda i: (0, 0)
    row = lambda i: (i, 0)
    return pl.pallas_call(
        functools.partial(_ffn_kernel, final_norm=final_norm),
        out_shape=jax.ShapeDtypeStruct((n, d), F32),
        grid=(n // ROW_TILE,),
        in_specs=[
            pl.BlockSpec((ROW_TILE, d), row),
            pl.BlockSpec((ROW_TILE, d), row),
            pl.BlockSpec((d, d), const, pipeline_mode=pl.Buffered(1)),
            pl.BlockSpec((1, d), const),
            pl.BlockSpec((d, 2 * D_FF), const, pipeline_mode=pl.Buffered(1)),
            pl.BlockSpec((D_FF, d), const, pipeline_mode=pl.Buffered(1)),
            pl.BlockSpec((1, d), const),
        ],
        out_specs=pl.BlockSpec((ROW_TILE, d), row),
        scratch_shapes=[
            pltpu.VMEM((ROW_TILE, d), F32),
            pltpu.VMEM((ROW_TILE, d), BF16),
            pltpu.VMEM((ROW_TILE, d), F32),
        ],
        compiler_params=_params(("parallel",)),
        name="outproj_ffn",
    )(x, o, wo, gain.reshape(1, d), wgu, wd, final_gain.reshape(1, d))


def _swa_kernel(q_ref, kp_ref, ko_ref, vp_ref, vo_ref, sink_ref, tab_ref, o_ref, ot_ref):
    L = SWA_BLOCK
    n = pl.program_id(1)
    first = jnp.where(n == 0, 1, 0)
    qt = (q_ref[...] * (LOG2E * HEAD_DIM ** -0.5)).T.astype(BF16)
    k = jnp.concatenate([kp_ref[...], ko_ref[...]], axis=0).astype(BF16)
    vt = jnp.concatenate([vp_ref[...], vo_ref[...]], axis=0).T.astype(BF16)
    ones = jnp.ones((SUM_ROWS, 2 * L), BF16)
    sinks = sink_ref[...] * LOG2E
    for kv in range(N_KV_HEADS):
        heads = [kv * GQA_GROUP + g for g in range(GQA_GROUP)]
        qg = jnp.concatenate([qt[hd * MOBA_SLOT:(hd + 1) * MOBA_SLOT, :] for hd in heads], axis=1)
        s_all = _dot(k[:, kv * MOBA_SLOT:(kv + 1) * MOBA_SLOT], qg)
        probs, extras = [], []
        for g, hd in enumerate(heads):
            s = s_all[:, g * L:(g + 1) * L] + tab_ref[first, hd]
            sink = sinks[:, hd:hd + 1]
            m = jnp.maximum(jnp.max(s, axis=0, keepdims=True), sink)
            probs.append(jnp.exp2(s - m).astype(BF16))
            extras.append(jnp.exp2(sink - m))
        vaug = jnp.concatenate([vt[kv * HEAD_DIM:(kv + 1) * HEAD_DIM, :], ones], axis=0)
        pv_all = _dot(vaug, jnp.concatenate(probs, axis=1))
        for g, hd in enumerate(heads):
            pv = pv_all[:, g * L:(g + 1) * L]
            denom = pv[HEAD_DIM:HEAD_DIM + 1, :] + extras[g]
            ot_ref[hd * HEAD_DIM:(hd + 1) * HEAD_DIM, :] = pv[:HEAD_DIM, :] / denom
    o_ref[...] = ot_ref[...].T


def _swa_core(proj, sinks, batch, seq):
    n = proj.shape[0]
    L = SWA_BLOCK
    nb = seq // L
    kcol = MOBA_Q_WIDTH // MOBA_K_WIDTH
    vcol = (MOBA_Q_WIDTH + MOBA_K_WIDTH) // ATTN_KV_DIM
    own = lambda c: (lambda b, i: (b * nb + i, c))
    prev = lambda c: (lambda b, i: (b * nb + jnp.maximum(i - 1, 0), c))
    key = jnp.arange(2 * L)[:, None]
    qry = jnp.arange(L)[None, :]
    dist = (L + qry) - key
    band = (dist >= 0) & (dist < SWA_WINDOW)
    slopes = (LOG2E * jnp.asarray(ALIBI_SLOPES, F32))[:, None, None]
    bias = -slopes * dist.astype(F32)[None]
    tab = jnp.stack([jnp.where(band[None], bias, MASKED),
                     jnp.where((band & (key >= L))[None], bias, MASKED)])
    return pl.pallas_call(
        _swa_kernel,
        out_shape=jax.ShapeDtypeStruct((n, ATTN_Q_DIM), F32),
        grid=(batch, nb),
        in_specs=[
            pl.BlockSpec((L, MOBA_Q_WIDTH), own(0)),
            pl.BlockSpec((L, MOBA_K_WIDTH), prev(kcol)),
            pl.BlockSpec((L, MOBA_K_WIDTH), own(kcol)),
            pl.BlockSpec((L, ATTN_KV_DIM), prev(vcol)),
            pl.BlockSpec((L, ATTN_KV_DIM), own(vcol)),
            pl.BlockSpec((1, N_Q_HEADS), lambda b, i: (0, 0)),
            pl.BlockSpec((2, N_Q_HEADS, 2 * L, L), lambda b, i: (0, 0, 0, 0), pipeline_mode=pl.Buffered(1)),
        ],
        out_specs=pl.BlockSpec((L, ATTN_Q_DIM), own(0)),
        scratch_shapes=[pltpu.VMEM((ATTN_Q_DIM, L), F32)],
        compiler_params=_params(("parallel", "parallel")),
        name="swa_core",
    )(proj, proj, proj, proj, proj, sinks.reshape(1, N_Q_HEADS), tab)


MOBA_SLOT = 2 * HEAD_DIM
PEN_LANES = 32
SLOPE_LANES = (HEAD_DIM + PEN_LANES, HEAD_DIM + PEN_LANES + 1)
STRIDE_LANES = (HEAD_DIM + PEN_LANES + 2, HEAD_DIM + PEN_LANES + 3)
MOBA_Q_WIDTH = N_Q_HEADS * MOBA_SLOT
MOBA_K_WIDTH = N_KV_HEADS * MOBA_SLOT
MOBA_WIDTH = MOBA_Q_WIDTH + MOBA_K_WIDTH + ATTN_KV_DIM
MASKED = -1e30
LOG2E = math.log2(math.e)
SUM_ROWS = 16
MOBA_KEY_GROUP = 8
MOBA_UNROLL = 2
NORM_MARGIN = 1.02
BOUND_LIMIT = 60.0

_KMEAN_BLOCKS = 8


def _kmean_kernel(k_ref, mean_ref, sqmax_ref):
    k = k_ref[...].reshape(_KMEAN_BLOCKS, MOBA_BLOCK, MOBA_K_WIDTH)
    mean_ref[...] = jnp.sum(k, axis=1) * (1.0 / MOBA_BLOCK)
    sqmax_ref[...] = jnp.max(k * k, axis=1)


def _moba_kmean(proj):
    n = proj.shape[0]
    rows = _KMEAN_BLOCKS * MOBA_BLOCK
    out = jax.ShapeDtypeStruct((n // MOBA_BLOCK, MOBA_K_WIDTH), F32)
    spec = pl.BlockSpec((_KMEAN_BLOCKS, MOBA_K_WIDTH), lambda i: (i, 0))
    return pl.pallas_call(
        _kmean_kernel,
        out_shape=(out, out),
        grid=(n // rows,),
        in_specs=[pl.BlockSpec((rows, MOBA_K_WIDTH), lambda i: (i, MOBA_Q_WIDTH // MOBA_K_WIDTH))],
        out_specs=(spec, spec),
        compiler_params=_params(("parallel",)),
        name="moba_kmean",
    )(proj)


def _moba_select_kernel(q_ref, km_ref, expand_ref, slope_ref, slot_ref, k_ref, v_ref,
                        qa_ref, ks_ref, vt_ref, qn_ref, *, nblk):
    n = pl.program_id(1)
    q = q_ref[...] * (HEAD_DIM ** -0.5)
    gate = _dot_split(q, km_ref[0])
    lane_i = lax.broadcasted_iota(jnp.int32, gate.shape, 1)
    blk = jnp.bitwise_and(lane_i, nblk - 1)
    head = jnp.right_shift(lane_i, int(math.log2(nblk)))
    lane = lane_i.astype(F32)
    gate = jnp.where(blk < n, gate, NEG_INF)
    sel = jnp.zeros(gate.shape, F32)
    for h in range(N_KV_HEADS):
        g = jnp.where(head == h, gate, NEG_INF)
        for _ in range(MOBA_TOP_K):
            mx = jnp.max(g, axis=-1, keepdims=True)
            is_max = (g == mx) & (mx > NEG_INF)
            first = jnp.min(jnp.where(is_max, lane, float(4 * LANES)), axis=-1, keepdims=True)
            pick = lane == first
            sel = jnp.where(pick, 1.0, sel)
            g = jnp.where(pick, NEG_INF, g)
    unselected = (1.0 - sel).astype(BF16)
    qs = q * LOG2E
    norm2 = _dot((qs * qs).astype(BF16), slot_ref[...])
    qn_ref[...] = (jnp.sqrt(norm2) * NORM_MARGIN).T[:N_Q_HEADS, :]
    q_feat = qs + slope_ref[...]
    for var in range(qa_ref.shape[0]):
        penalty = _dot(unselected, expand_ref[var])
        qa_ref[var] = (q_feat + penalty).T.astype(BF16)
    ks_ref[0] = k_ref[...].astype(BF16)
    vt_ref[0] = v_ref[...].T.astype(BF16)


def _moba_select(proj, km_mat, expand, slope_feat, batch, seq):
    n = proj.shape[0]
    nblk = seq // MOBA_BLOCK
    width = N_KV_HEADS * nblk
    nvar = expand.shape[0]
    vcol = (MOBA_Q_WIDTH + MOBA_K_WIDTH) // ATTN_KV_DIM
    slot = jnp.arange(MOBA_Q_WIDTH) // MOBA_SLOT
    lane = jnp.arange(MOBA_Q_WIDTH) % MOBA_SLOT
    slot_sum = ((slot[:, None] == jnp.arange(LANES)[None, :]) & (lane[:, None] < HEAD_DIM)).astype(BF16)
    return pl.pallas_call(
        functools.partial(_moba_select_kernel, nblk=nblk),
        out_shape=(jax.ShapeDtypeStruct((nvar, MOBA_Q_WIDTH, n), BF16),
                   jax.ShapeDtypeStruct((n // MOBA_BLOCK, MOBA_BLOCK, MOBA_K_WIDTH), BF16),
                   jax.ShapeDtypeStruct((n // MOBA_BLOCK, ATTN_KV_DIM, MOBA_BLOCK), BF16),
                   jax.ShapeDtypeStruct((N_Q_HEADS, n), F32)),
        grid=(batch, nblk),
        in_specs=[
            pl.BlockSpec((MOBA_BLOCK, MOBA_Q_WIDTH), lambda b, i: (b * nblk + i, 0)),
            pl.BlockSpec((1, MOBA_Q_WIDTH, width), lambda b, i: (b, 0, 0)),
            pl.BlockSpec((nvar, width, MOBA_Q_WIDTH), lambda b, i: (0, 0, 0)),
            pl.BlockSpec((1, MOBA_Q_WIDTH), lambda b, i: (0, 0)),
            pl.BlockSpec((MOBA_Q_WIDTH, LANES), lambda b, i: (0, 0)),
            pl.BlockSpec((MOBA_BLOCK, MOBA_K_WIDTH), lambda b, i: (b * nblk + i, MOBA_Q_WIDTH // MOBA_K_WIDTH)),
            pl.BlockSpec((MOBA_BLOCK, ATTN_KV_DIM), lambda b, i: (b * nblk + i, vcol)),
        ],
        out_specs=(pl.BlockSpec((nvar, MOBA_Q_WIDTH, MOBA_BLOCK), lambda b, i: (0, 0, b * nblk + i)),
                   pl.BlockSpec((1, MOBA_BLOCK, MOBA_K_WIDTH), lambda b, i: (b * nblk + i, 0, 0)),
                   pl.BlockSpec((1, ATTN_KV_DIM, MOBA_BLOCK), lambda b, i: (b * nblk + i, 0, 0)),
                   pl.BlockSpec((N_Q_HEADS, MOBA_BLOCK), lambda b, i: (0, b * nblk + i))),
        compiler_params=_params(("parallel", "parallel")),
        name="moba_select",
    )(proj, km_mat, expand, slope_feat, slot_sum, proj, proj)


def _moba_scores(ka, qa_ref, var, kv):
    heads = [kv * GQA_GROUP + g for g in range(GQA_GROUP)]
    qt = jnp.concatenate([qa_ref[var, hd * MOBA_SLOT:(hd + 1) * MOBA_SLOT, :] for hd in heads], axis=1)
    return _dot(ka, qt)


def _moba_blocks_update(blocks, qa_ref, var, m_ref, l_ref, acc_ref, causal=None, qn_ref=None):
    keys = lambda kv: jnp.concatenate([blk[0](kv) for blk in blocks], axis=0)
    s_next = _moba_scores(keys(0), qa_ref, var, 0)
    for kv in range(N_KV_HEADS):
        s_all = s_next
        if kv + 1 < N_KV_HEADS:
            s_next = _moba_scores(keys(kv + 1), qa_ref, var, kv + 1)
        vt = jnp.concatenate([blk[1](kv) for blk in blocks], axis=1)
        vt = jnp.concatenate([vt, jnp.ones((SUM_ROWS, vt.shape[1]), BF16)], axis=0)
        k_norm = None
        if qn_ref is not None:
            k_norm = functools.reduce(jnp.maximum, [blk[2](kv) for blk in blocks])
        _moba_softmax_pv(s_all, vt, m_ref, l_ref, acc_ref, kv, causal, qn_ref, k_norm)


def _moba_softmax_pv(s_all, vt, m_ref, l_ref, acc_ref, kv, causal, qn_ref, k_norm):
    Lb = MOBA_BLOCK
    heads = [kv * GQA_GROUP + g for g in range(GQA_GROUP)]
    probs, alphas = [], []
    for g, hd in enumerate(heads):
        s = s_all[:, g * Lb:(g + 1) * Lb]
        if causal is not None:
            s = jnp.where(causal, s, MASKED)
        m_old = m_ref[hd:hd + 1, :]
        if k_norm is None:
            m_new = jnp.maximum(m_old, jnp.max(s, axis=0, keepdims=True))
        else:
            qry = lax.broadcasted_iota(jnp.int32, (1, Lb), 1).astype(F32)
            alibi_top = (qry - (Lb - 1)) * (LOG2E * ALIBI_SLOPES[hd])
            m_new = jnp.maximum(m_old, qn_ref[hd:hd + 1, :] * k_norm + alibi_top)
        alphas.append(jnp.exp2(m_old - m_new))
        probs.append(jnp.exp2(s - m_new).astype(BF16))
        m_ref[hd:hd + 1, :] = m_new
    pv_all = _dot(vt, jnp.concatenate(probs, axis=1))
    for g, hd in enumerate(heads):
        pv = pv_all[:, g * Lb:(g + 1) * Lb]
        rows = slice(hd * HEAD_DIM, (hd + 1) * HEAD_DIM)
        l_ref[hd:hd + 1, :] = alphas[g] * l_ref[hd:hd + 1, :] + pv[HEAD_DIM:HEAD_DIM + 1, :]
        acc_ref[rows, :] = alphas[g] * acc_ref[rows, :] + pv[:HEAD_DIM, :]


def _moba_attn_kernel(nq_ref, jg_ref, bounded_ref, qa_ref, k_ref, vt_ref, qn_ref, kn_ref, o_ref,
                      m_ref, l_ref, acc_ref, *, group, nblk):
    Lb = MOBA_BLOCK
    step = pl.program_id(1)
    n = nq_ref[step]
    first = jg_ref[step] * group
    var = first // PEN_LANES
    bounded = bounded_ref[pl.program_id(0) * nblk + n]

    @pl.when(first == 0)
    def _():
        m_ref[...] = jnp.full(m_ref.shape, NEG_INF, F32)
        l_ref[...] = jnp.zeros(l_ref.shape, F32)
        acc_ref[...] = jnp.zeros(acc_ref.shape, F32)

    lane = lax.broadcasted_iota(jnp.int32, (Lb, MOBA_SLOT), 1)
    key_off = (lax.broadcasted_iota(jnp.int32, (Lb, MOBA_SLOT), 0) - (Lb - 1)).astype(F32)
    slope_lanes = (lane == SLOPE_LANES[0]) | (lane == SLOPE_LANES[1])
    stride_lanes = (lane == STRIDE_LANES[0]) | (lane == STRIDE_LANES[1])
    key_feat = jnp.where(slope_lanes, key_off, 0.0)

    def past_block(i):
        j = first + i
        extra = jnp.where(stride_lanes, (j - n).astype(F32), key_feat)
        extra = jnp.where(lane == HEAD_DIM + (j - var * PEN_LANES), 1.0, extra).astype(BF16)
        return (lambda kv: k_ref[i, :, kv * MOBA_SLOT:(kv + 1) * MOBA_SLOT] + extra,
                lambda kv: vt_ref[i, kv * HEAD_DIM:(kv + 1) * HEAD_DIM, :],
                lambda kv: kn_ref[0, pl.ds(i * N_KV_HEADS + kv, 1), :])

    def past_blocks(count, shift_ref):
        def body(it, carry):
            blocks = [past_block(carry + u) for u in range(count)]
            _moba_blocks_update(blocks, qa_ref, var, m_ref, l_ref, acc_ref, None, shift_ref)
            return carry + count
        return body

    n_past = jnp.minimum(n - first, group)
    n_multi = n_past // MOBA_UNROLL

    def run_past(shift_ref):
        done = lax.fori_loop(0, n_multi, past_blocks(MOBA_UNROLL, shift_ref), 0)
        lax.fori_loop(0, n_past - done, past_blocks(1, shift_ref), done)

    @pl.when(bounded == 1)
    def _():
        run_past(qn_ref)

    @pl.when(bounded != 1)
    def _():
        run_past(None)

    @pl.when(n - first < group)
    def _():
        i = n - first
        key = lax.broadcasted_iota(jnp.int32, (Lb, Lb), 0)
        qry = lax.broadcasted_iota(jnp.int32, (Lb, Lb), 1)
        causal = key <= qry
        extra = key_feat.astype(BF16)
        own = (lambda kv: k_ref[i, :, kv * MOBA_SLOT:(kv + 1) * MOBA_SLOT] + extra,
               lambda kv: vt_ref[i, kv * HEAD_DIM:(kv + 1) * HEAD_DIM, :])
        _moba_blocks_update([own], qa_ref, var, m_ref, l_ref, acc_ref, causal)
        inv = 1.0 / l_ref[...]
        acc = acc_ref[...].reshape(N_Q_HEADS, HEAD_DIM, Lb) * inv[:, None, :]
        o_ref[...] = acc.reshape(N_Q_HEADS * HEAD_DIM, Lb).T


def _moba_attn(qaug, kslot, vt, qnorm, knorm, batch, seq):
    n = batch * seq
    nblk = seq // MOBA_BLOCK
    group = min(MOBA_KEY_GROUP, nblk)
    assert nblk % group == 0
    pairs = [(i, jg) for i in range(nblk) for jg in range(i // group + 1)]
    nq = jnp.asarray([p[0] for p in pairs], jnp.int32)
    jk = jnp.asarray([p[1] for p in pairs], jnp.int32)
    ngrp = nblk // group
    nvar = qaug.shape[0]
    assert PEN_LANES % group == 0, "a key group must not straddle two query variants"
    q_top = jnp.max(qnorm.reshape(N_Q_HEADS, batch, nblk, MOBA_BLOCK), axis=(0, 3))
    bounded = (q_top * jnp.max(knorm, axis=(1, 2))[:, None] <= BOUND_LIMIT).astype(jnp.int32).reshape(-1)
    kn_rows = jnp.broadcast_to(knorm.reshape(batch * ngrp, group * N_KV_HEADS, 1),
                               (batch * ngrp, group * N_KV_HEADS, MOBA_BLOCK))
    qmap = lambda b, s, nq_r, jk_r, bd_r: (b * nblk + nq_r[s], 0)
    kmap = lambda b, s, nq_r, jk_r, bd_r: (b * ngrp + jk_r[s], 0, 0)
    grid_spec = pltpu.PrefetchScalarGridSpec(
        num_scalar_prefetch=3,
        grid=(batch, len(pairs)),
        in_specs=[
            pl.BlockSpec((nvar, MOBA_Q_WIDTH, MOBA_BLOCK), lambda b, s, nq_r, jk_r, bd_r: (0, 0, b * nblk + nq_r[s])),
            pl.BlockSpec((group, MOBA_BLOCK, MOBA_K_WIDTH), kmap),
            pl.BlockSpec((group, ATTN_KV_DIM, MOBA_BLOCK), kmap),
            pl.BlockSpec((N_Q_HEADS, MOBA_BLOCK), lambda b, s, nq_r, jk_r, bd_r: (0, b * nblk + nq_r[s])),
            pl.BlockSpec((1, group * N_KV_HEADS, MOBA_BLOCK), kmap),
        ],
        out_specs=pl.BlockSpec((MOBA_BLOCK, ATTN_Q_DIM), qmap),
        scratch_shapes=[
            pltpu.VMEM((N_Q_HEADS, MOBA_BLOCK), F32),
            pltpu.VMEM((N_Q_HEADS, MOBA_BLOCK), F32),
            pltpu.VMEM((N_Q_HEADS * HEAD_DIM, MOBA_BLOCK), F32),
        ],
    )
    return pl.pallas_call(
        functools.partial(_moba_attn_kernel, group=group, nblk=nblk),
        out_shape=jax.ShapeDtypeStruct((n, ATTN_Q_DIM), F32),
        grid_spec=grid_spec,
        compiler_params=_params(("parallel", "arbitrary")),
        name="moba_attn",
    )(nq, jk, bounded, qaug, kslot, vt, qnorm, kn_rows)


def _moba_slot_weights(w_in):
    d = w_in.shape[0]
    fill = MOBA_SLOT - HEAD_DIM
    wq = w_in[:, :ATTN_Q_DIM].reshape(d, N_Q_HEADS, HEAD_DIM)
    wk = w_in[:, ATTN_Q_DIM:ATTN_Q_DIM + ATTN_KV_DIM].reshape(d, N_KV_HEADS, HEAD_DIM)
    wq = jnp.pad(wq, ((0, 0), (0, 0), (0, fill))).reshape(d, MOBA_Q_WIDTH)
    wk = jnp.pad(wk, ((0, 0), (0, 0), (0, fill))).reshape(d, MOBA_K_WIDTH)
    return jnp.concatenate([wq, wk, w_in[:, ATTN_Q_DIM + ATTN_KV_DIM:]], axis=1)


def _moba_core(proj, batch, seq):
    nblk = seq // MOBA_BLOCK
    nvar = -(-nblk // PEN_LANES)
    width = N_KV_HEADS * nblk
    kmean, ksqmax = _moba_kmean(proj)
    knorm = jnp.sqrt(jnp.sum(ksqmax.reshape(batch, nblk, N_KV_HEADS, MOBA_SLOT), axis=-1))
    km = kmean.reshape(batch, nblk, N_KV_HEADS, MOBA_SLOT).transpose(0, 2, 3, 1)
    eye = jnp.eye(N_KV_HEADS, dtype=F32)
    km_mat = jnp.einsum('bhej,hH->bheHj', km, eye)
    km_mat = jnp.broadcast_to(km_mat[:, :, None], (batch, N_KV_HEADS, GQA_GROUP, MOBA_SLOT, N_KV_HEADS, nblk))
    km_mat = km_mat.reshape(batch, MOBA_Q_WIDTH, width)
    h_of = jnp.arange(width) // nblk
    c_of = jnp.arange(width) % nblk
    slot = jnp.arange(MOBA_Q_WIDTH) // MOBA_SLOT
    lane = jnp.arange(MOBA_Q_WIDTH) % MOBA_SLOT
    hit = (h_of[:, None] == (slot // GQA_GROUP)[None, :]) & (lane[None, :] == HEAD_DIM + (c_of % PEN_LANES)[:, None])
    in_var = (c_of // PEN_LANES)[None, :, None] == jnp.arange(nvar)[:, None, None]
    expand = jnp.where(hit[None] & in_var, MASKED, 0.0).astype(BF16)
    sigma = LOG2E * jnp.asarray(ALIBI_SLOPES, F32)
    slope_feat = jnp.zeros((MOBA_Q_WIDTH,), F32)
    for lanes, value in ((SLOPE_LANES, sigma), (STRIDE_LANES, sigma * MOBA_BLOCK)):
        hi = value.astype(BF16).astype(F32)
        lo = (value - hi).astype(BF16).astype(F32)
        slope_feat = slope_feat + jnp.where(lane == lanes[0], hi[slot], 0.0) + jnp.where(lane == lanes[1], lo[slot], 0.0)
    slope_feat = slope_feat.reshape(1, MOBA_Q_WIDTH)
    qaug, kslot, vt, qnorm = _moba_select(proj, km_mat, expand, slope_feat, batch, seq)
    return _moba_attn(qaug, kslot, vt, qnorm, knorm, batch, seq)


LIN_CHUNKS = LIN_ROWS // LIN_CHUNK


def _lin_masks(key_dim):
    R = LIN_ROWS
    chunk_bits = int(math.log2(LIN_CHUNK))
    key_bits = int(math.log2(key_dim))
    assert (1 << chunk_bits) == LIN_CHUNK and (1 << key_bits) == key_dim
    r = lax.broadcasted_iota(jnp.int32, (R, R), 0)
    c = lax.broadcasted_iota(jnp.int32, (R, R), 1)
    same = jnp.right_shift(r, chunk_bits) == jnp.right_shift(c, chunk_bits)
    causal = same & (r >= c)
    sums = jnp.concatenate([jnp.where(causal, 1.0, 0.0), jnp.where(same, 1.0, 0.0)], axis=0).astype(BF16)
    rr = lax.broadcasted_iota(jnp.int32, (R, LIN_CHUNKS * key_dim), 0)
    cc = lax.broadcasted_iota(jnp.int32, (R, LIN_CHUNKS * key_dim), 1)
    own = jnp.right_shift(rr, chunk_bits) == jnp.right_shift(cc, key_bits)
    return sums, causal, own


def _lin_tile(q, k, v, log_g, st_ref, h, masks):
    sums, causal, own = masks
    R, K = q.shape
    hi = log_g.astype(BF16)
    r1 = log_g - hi.astype(F32)
    mid = r1.astype(BF16)
    lo = (r1 - mid.astype(F32)).astype(BF16)
    acc = _dot(sums, jnp.concatenate([hi, mid, lo], axis=1))
    acc = acc[:, :K] + (acc[:, K:2 * K] + acc[:, 2 * K:])
    b, b_chunk = acc[:R], acc[R:]
    q_dec = (q * jnp.exp(b)).astype(BF16)
    k_dec = (k * jnp.exp(-b)).astype(BF16)
    k_tail = (k * jnp.exp(b_chunk - b)).astype(BF16)
    vb = v.astype(BF16)
    attn = jnp.where(causal, _dot_nt(q_dec, k_dec), 0.0).astype(BF16)
    o = _dot(attn, vb)
    tile_lanes = lambda x: jnp.concatenate([x] * LIN_CHUNKS, axis=1)
    zero = jnp.zeros((), BF16)
    ds = _dot_tn(vb, jnp.where(own, tile_lanes(k_tail), zero))
    st = st_ref[h]
    states = []
    for c in range(LIN_CHUNKS):
        states.append(st.astype(BF16))
        decay = jnp.exp(b_chunk[c * LIN_CHUNK:c * LIN_CHUNK + 1, :])
        st = st * decay + ds[:, c * K:(c + 1) * K]
    st_ref[h] = st
    o = o + _dot_nt(jnp.where(own, tile_lanes(q_dec), zero), jnp.concatenate(states, axis=1))
    return o


def _head_norm_gate(o, gain, g):
    o = o * lax.rsqrt(jnp.mean(o * o, axis=-1, keepdims=True) + RMS_EPS) * gain
    return o * _silu(g)


def _gla_kernel(q_ref, k_ref, v_ref, g_ref, a_ref, wd_ref, bd_ref, gain_ref, o_ref, st_ref):
    @pl.when(pl.program_id(1) == 0)
    def _():
        st_ref[...] = jnp.zeros(st_ref.shape, F32)

    K, V = GLA_KEY_DIM, GLA_VAL_DIM
    masks = _lin_masks(K)
    z = _dot_split(a_ref[...], wd_ref[...]) + bd_ref[...]
    log_alpha = (jnp.minimum(z, 0.0) - jnp.log(1.0 + jnp.exp(-jnp.abs(z)))) * (1.0 / GLA_GATE_TEMP)
    for h in range(GLA_HEADS):
        q = q_ref[:, h * K:(h + 1) * K] * (K ** -0.5)
        k = k_ref[:, h * K:(h + 1) * K]
        v = v_ref[:, h * V:(h + 1) * V]
        o = _lin_tile(q, k, v, log_alpha[:, h * K:(h + 1) * K], st_ref, h, masks)
        o_ref[:, h * V:(h + 1) * V] = _head_norm_gate(o, gain_ref[...], g_ref[:, h * V:(h + 1) * V])


def _gla_core(proj, wd_pad, bd, gain, batch, seq):
    n = proj.shape[0]
    nt = seq // LIN_ROWS
    dk = GLA_HEADS * GLA_KEY_DIM
    dv = GLA_HEADS * GLA_VAL_DIM
    rows = lambda c: (lambda b, t: (b * nt + t, c))
    const = lambda b, t: (0, 0)
    return pl.pallas_call(
        _gla_kernel,
        out_shape=jax.ShapeDtypeStruct((n, dv), F32),
        grid=(batch, nt),
        in_specs=[
            pl.BlockSpec((LIN_ROWS, dk), rows(0)),
            pl.BlockSpec((LIN_ROWS, dk), rows(1)),
            pl.BlockSpec((LIN_ROWS, dv), rows(1)),
            pl.BlockSpec((LIN_ROWS, dv), rows(2)),
            pl.BlockSpec((LIN_ROWS, LANES), rows((2 * dk + 2 * dv) // LANES)),
            pl.BlockSpec((LANES, dk), const),
            pl.BlockSpec((1, dk), const),
            pl.BlockSpec((1, GLA_VAL_DIM), const),
        ],
        out_specs=pl.BlockSpec((LIN_ROWS, dv), rows(0)),
        scratch_shapes=[pltpu.VMEM((GLA_HEADS, GLA_VAL_DIM, GLA_KEY_DIM), F32)],
        compiler_params=_params(("parallel", "arbitrary")),
        name="gla_core",
    )(proj, proj, proj, proj, proj, wd_pad, bd.reshape(1, dk), gain.reshape(1, GLA_VAL_DIM))


def _hgrn_kernel(q_ref, f_ref, i_ref, g_ref, lbl_ref, gain_ref, o_ref, st_ref, *, layer):
    @pl.when(pl.program_id(1) == 0)
    def _():
        st_ref[...] = jnp.zeros(st_ref.shape, F32)

    K, V = HGRN_KEY_DIM, HGRN_VAL_DIM
    masks = _lin_masks(K)
    logits = lbl_ref[...]
    e = jnp.exp(logits - jnp.max(logits, axis=0, keepdims=True))
    p = e / jnp.sum(e, axis=0, keepdims=True)
    lb = jnp.zeros((1, logits.shape[1]), F32)
    for l in range(1, layer + 1):
        lb = lb + p[l:l + 1, :]

    for h in range(HGRN_HEADS):
        cols = slice(h * K, (h + 1) * K)
        q = _silu(q_ref[:, cols]) * (K ** -0.5)
        lbh = lb[:, cols]
        f = lbh + (1.0 - lbh) * _sigmoid(f_ref[:, cols])
        o = _lin_tile(q, 1.0 - f, i_ref[:, cols], jnp.log(f), st_ref, h, masks)
        o_ref[:, cols] = _head_norm_gate(o, gain_ref[...], g_ref[:, cols])


def _hgrn_core(proj, lb_logits, gain, layer, batch, seq):
    n = proj.shape[0]
    nt = seq // LIN_ROWS
    d = D_MODEL
    rows = lambda c: (lambda b, t: (b * nt + t, c))
    const = lambda b, t: (0, 0)
    return pl.pallas_call(
        functools.partial(_hgrn_kernel, layer=layer),
        out_shape=jax.ShapeDtypeStruct((n, d), F32),
        grid=(batch, nt),
        in_specs=[
            pl.BlockSpec((LIN_ROWS, d), rows(0)),
            pl.BlockSpec((LIN_ROWS, d), rows(1)),
            pl.BlockSpec((LIN_ROWS, d), rows(2)),
            pl.BlockSpec((LIN_ROWS, d), rows(3)),
            pl.BlockSpec(lb_logits.shape, const),
            pl.BlockSpec((1, HGRN_VAL_DIM), const),
        ],
        out_specs=pl.BlockSpec((LIN_ROWS, d), rows(0)),
        scratch_shapes=[pltpu.VMEM((HGRN_HEADS, HGRN_VAL_DIM, HGRN_KEY_DIM), F32)],
        compiler_params=_params(("parallel", "arbitrary")),
        name="hgrn_core",
    )(proj, proj, proj, proj, lb_logits, gain.reshape(1, HGRN_VAL_DIM))


def kernel(x, norm_mix, norm_ffn, swa_w_in, swa_sinks, swa_w_out, moba_w_in, moba_w_out, gla_w_in, gla_w_decay_up, gla_b_decay, gla_out_norm, gla_w_out, hgrn_w_in, hgrn_lb_logits, hgrn_out_norm, hgrn_w_out, ffn_w_gate_up, ffn_w_down, final_norm):
    batch, seq, d = x.shape
    depth = norm_mix.shape[0]
    xf = x.reshape(batch * seq, d)
    for i in range(depth):
        kind, j = i % N_MIXERS, i // N_MIXERS
        if kind == 0:
            proj = _norm_proj(xf, norm_mix[i], _moba_slot_weights(swa_w_in[j]).astype(BF16), MOBA_WIDTH // 2)
            o = _swa_core(proj, swa_sinks[j], batch, seq)
            w_out = swa_w_out[j]
        elif kind == 1:
            proj = _norm_proj(xf, norm_mix[i], _moba_slot_weights(moba_w_in[j]).astype(BF16), MOBA_WIDTH // 2)
            o = _moba_core(proj, batch, seq)
            w_out = moba_w_out[j]
        elif kind == 2:
            pad = LANES - GLA_GATE_RANK
            w_in = jnp.pad(gla_w_in[j], ((0, 0), (0, pad))).astype(BF16)
            wd_pad = jnp.pad(gla_w_decay_up[j], ((0, pad), (0, 0)))
            proj = _norm_proj(xf, norm_mix[i], w_in, 640)
            o = _gla_core(proj, wd_pad, gla_b_decay[j], gla_out_norm[j], batch, seq)
            w_out = gla_w_out[j]
        else:
            proj = _norm_proj(xf, norm_mix[i], hgrn_w_in[j].astype(BF16), 1024)
            o = _hgrn_core(proj, hgrn_lb_logits, hgrn_out_norm[j], i, batch, seq)
            w_out = hgrn_w_out[j]
        last = i == depth - 1
        xf = _ffn(xf, o, w_out.astype(BF16), norm_ffn[i], ffn_w_gate_up[i].astype(BF16),
                  ffn_w_down[i].astype(BF16), final_norm, last)
    return xf.reshape(batch, seq, d)
```

```python
import functools
import math

import jax
import jax.numpy as jnp
from jax import lax
from jax.experimental import pallas as pl
from jax.experimental.pallas import tpu as pltpu

F32 = jnp.float32
BF16 = jnp.bfloat16

D_MODEL = 1024
HEAD_DIM = 64
N_Q_HEADS = 16
N_KV_HEADS = 4
GQA_GROUP = 4
ATTN_Q_DIM = 1024
ATTN_KV_DIM = 256
SWA_BLOCK = 128
SWA_WINDOW = 128
MOBA_BLOCK = 256
MOBA_TOP_K = 3
GLA_HEADS = 4
GLA_KEY_DIM = 128
GLA_VAL_DIM = 256
GLA_GATE_RANK = 16
GLA_GATE_TEMP = 16.0
HGRN_HEADS = 8
HGRN_KEY_DIM = 128
HGRN_VAL_DIM = 128
LIN_CHUNK = 64
D_FF = 2816
RMS_EPS = 1e-6
N_MIXERS = 4

LANES = 128
VMEM_LIMIT = 56 * 1024 * 1024
FFN_CHUNK = 256
ROW_TILE = 512
PROJ_ROWS = 1024
LIN_ROWS = 256

ALIBI_SLOPES = tuple(2.0 ** (-8.0 * (i + 1) / N_Q_HEADS) for i in range(N_Q_HEADS))
NEG_INF = float("-inf")


def _params(sem):
    return pltpu.CompilerParams(dimension_semantics=sem, vmem_limit_bytes=VMEM_LIMIT)


def _dot(a, b):
    return jnp.dot(a, b, preferred_element_type=F32)


def _dot_nt(a, b):
    return lax.dot_general(a, b, (((1,), (1,)), ((), ())), preferred_element_type=F32)


def _dot_tn(a, b):
    return lax.dot_general(a, b, (((0,), (0,)), ((), ())), preferred_element_type=F32)


def _split(x):
    hi = x.astype(BF16)
    lo = (x - hi.astype(F32)).astype(BF16)
    return hi, lo


def _dot_split(a, b):
    ah, al = _split(a)
    bh, bl = _split(b)
    return _dot(ah, bh) + (_dot(ah, bl) + _dot(al, bh))


def _rms(x, gain):
    return x * lax.rsqrt(jnp.mean(x * x, axis=-1, keepdims=True) + RMS_EPS) * gain


def _sigmoid(x):
    return 1.0 / (1.0 + jnp.exp(-x))


def _silu(x):
    return x * _sigmoid(x)


def _norm_proj_kernel(x_ref, g_ref, w_ref, o_ref, h_ref):
    @pl.when(pl.program_id(1) == 0)
    def _():
        h_ref[...] = _rms(x_ref[...], g_ref[...]).astype(BF16)

    o_ref[...] = _dot(h_ref[...], w_ref[...])


def _norm_proj(x, gain, w, tn):
    n, d = x.shape
    dout = w.shape[1]
    return pl.pallas_call(
        _norm_proj_kernel,
        out_shape=jax.ShapeDtypeStruct((n, dout), F32),
        grid=(n // PROJ_ROWS, dout // tn),
        in_specs=[
            pl.BlockSpec((PROJ_ROWS, d), lambda i, j: (i, 0)),
            pl.BlockSpec((1, d), lambda i, j: (0, 0)),
            pl.BlockSpec((d, tn), lambda i, j: (0, j)),
        ],
        out_specs=pl.BlockSpec((PROJ_ROWS, tn), lambda i, j: (i, j)),
        scratch_shapes=[pltpu.VMEM((PROJ_ROWS, d), BF16)],
        compiler_params=_params(("parallel", "arbitrary")),
        name="norm_proj",
    )(x, gain.reshape(1, d), w)


def _ffn_kernel(x_ref, o_ref, wo_ref, g_ref, wgu_ref, wd_ref, fg_ref, out_ref,
                x1_ref, h_ref, acc_ref, *, final_norm):
    x1 = x_ref[...] + _dot(o_ref[...].astype(BF16), wo_ref[...])
    x1_ref[...] = x1
    h_ref[...] = _rms(x1, g_ref[...]).astype(BF16)
    for c in range(D_FF // FFN_CHUNK):
        h = h_ref[...]
        gate = _dot(h, wgu_ref[:, c * FFN_CHUNK:(c + 1) * FFN_CHUNK])
        up = _dot(h, wgu_ref[:, D_FF + c * FFN_CHUNK:D_FF + (c + 1) * FFN_CHUNK])
        act = (_silu(gate) * up).astype(BF16)
        part = _dot(act, wd_ref[c * FFN_CHUNK:(c + 1) * FFN_CHUNK, :])
        if c == 0:
            acc_ref[...] = part
        else:
            acc_ref[...] += part
    y = x1_ref[...] + acc_ref[...]
    if final_norm:
        y = _rms(y, fg_ref[...])
    out_ref[...] = y


def _ffn(x, o, wo, gain, wgu, wd, final_gain, final_norm):
    n, d = x.shape
    const = lambda i: (0, 0)
    row = lambda i: (i, 0)
    return pl.pallas_call(
        functools.partial(_ffn_kernel, final_norm=final_norm),
        out_shape=jax.ShapeDtypeStruct((n, d), F32),
        grid=(n // ROW_TILE,),
        in_specs=[
            pl.BlockSpec((ROW_TILE, d), row),
            pl.BlockSpec((ROW_TILE, d), row),
            pl.BlockSpec((d, d), const, pipeline_mode=pl.Buffered(1)),
            pl.BlockSpec((1, d), const),
            pl.BlockSpec((d, 2 * D_FF), const, pipeline_mode=pl.Buffered(1)),
            pl.BlockSpec((D_FF, d), const, pipeline_mode=pl.Buffered(1)),
            pl.BlockSpec((1, d), const),
        ],
        out_specs=pl.BlockSpec((ROW_TILE, d), row),
        scratch_shapes=[
            pltpu.VMEM((ROW_TILE, d), F32),
            pltpu.VMEM((ROW_TILE, d), BF16),
            pltpu.VMEM((ROW_TILE, d), F32),
        ],
        compiler_params=_params(("parallel",)),
        name="outproj_ffn",
    )(x, o, wo, gain.reshape(1, d), wgu, wd, final_gain.reshape(1, d))


def _swa_kernel(q_ref, kp_ref, ko_ref, vp_ref, vo_ref, sink_ref, tab_ref, o_ref, ot_ref):
    L = SWA_BLOCK
    n = pl.program_id(1)
    first = jnp.where(n == 0, 1, 0)
    qt = (q_ref[...] * (LOG2E * HEAD_DIM ** -0.5)).T.astype(BF16)
    k = jnp.concatenate([kp_ref[...], ko_ref[...]], axis=0).astype(BF16)
    vt = jnp.concatenate([vp_ref[...], vo_ref[...]], axis=0).T.astype(BF16)
    ones = jnp.ones((SUM_ROWS, 2 * L), BF16)
    sinks = sink_ref[...] * LOG2E
    for kv in range(N_KV_HEADS):
        heads = [kv * GQA_GROUP + g for g in range(GQA_GROUP)]
        qg = jnp.concatenate([qt[hd * MOBA_SLOT:(hd + 1) * MOBA_SLOT, :] for hd in heads], axis=1)
        s_all = _dot(k[:, kv * MOBA_SLOT:(kv + 1) * MOBA_SLOT], qg)
        probs, extras = [], []
        for g, hd in enumerate(heads):
            s = s_all[:, g * L:(g + 1) * L] + tab_ref[first, hd]
            sink = sinks[:, hd:hd + 1]
            m = jnp.maximum(jnp.max(s, axis=0, keepdims=True), sink)
            probs.append(jnp.exp2(s - m).astype(BF16))
            extras.append(jnp.exp2(sink - m))
        vaug = jnp.concatenate([vt[kv * HEAD_DIM:(kv + 1) * HEAD_DIM, :], ones], axis=0)
        pv_all = _dot(vaug, jnp.concatenate(probs, axis=1))
        for g, hd in enumerate(heads):
            pv = pv_all[:, g * L:(g + 1) * L]
            denom = pv[HEAD_DIM:HEAD_DIM + 1, :] + extras[g]
            ot_ref[hd * HEAD_DIM:(hd + 1) * HEAD_DIM, :] = pv[:HEAD_DIM, :] / denom
    o_ref[...] = ot_ref[...].T


def _swa_core(proj, sinks, batch, seq):
    n = proj.shape[0]
    L = SWA_BLOCK
    nb = seq // L
    kcol = MOBA_Q_WIDTH // MOBA_K_WIDTH
    vcol = (MOBA_Q_WIDTH + MOBA_K_WIDTH) // ATTN_KV_DIM
    own = lambda c: (lambda b, i: (b * nb + i, c))
    prev = lambda c: (lambda b, i: (b * nb + jnp.maximum(i - 1, 0), c))
    key = jnp.arange(2 * L)[:, None]
    qry = jnp.arange(L)[None, :]
    dist = (L + qry) - key
    band = (dist >= 0) & (dist < SWA_WINDOW)
    slopes = (LOG2E * jnp.asarray(ALIBI_SLOPES, F32))[:, None, None]
    bias = -slopes * dist.astype(F32)[None]
    tab = jnp.stack([jnp.where(band[None], bias, MASKED),
                     jnp.where((band & (key >= L))[None], bias, MASKED)])
    return pl.pallas_call(
        _swa_kernel,
        out_shape=jax.ShapeDtypeStruct((n, ATTN_Q_DIM), F32),
        grid=(batch, nb),
        in_specs=[
            pl.BlockSpec((L, MOBA_Q_WIDTH), own(0)),
            pl.BlockSpec((L, MOBA_K_WIDTH), prev(kcol)),
            pl.BlockSpec((L, MOBA_K_WIDTH), own(kcol)),
            pl.BlockSpec((L, ATTN_KV_DIM), prev(vcol)),
            pl.BlockSpec((L, ATTN_KV_DIM), own(vcol)),
            pl.BlockSpec((1, N_Q_HEADS), lambda b, i: (0, 0)),
            pl.BlockSpec((2, N_Q_HEADS, 2 * L, L), lambda b, i: (0, 0, 0, 0), pipeline_mode=pl.Buffered(1)),
        ],
        out_specs=pl.BlockSpec((L, ATTN_Q_DIM), own(0)),
        scratch_shapes=[pltpu.VMEM((ATTN_Q_DIM, L), F32)],
        compiler_params=_params(("parallel", "parallel")),
        name="swa_core",
    )(proj, proj, proj, proj, proj, sinks.reshape(1, N_Q_HEADS), tab)


MOBA_SLOT = 2 * HEAD_DIM
PEN_LANES = 32
SLOPE_LANES = (HEAD_DIM + PEN_LANES, HEAD_DIM + PEN_LANES + 1)
STRIDE_LANES = (HEAD_DIM + PEN_LANES + 2, HEAD_DIM + PEN_LANES + 3)
MOBA_Q_WIDTH = N_Q_HEADS * MOBA_SLOT
MOBA_K_WIDTH = N_KV_HEADS * MOBA_SLOT
MOBA_WIDTH = MOBA_Q_WIDTH + MOBA_K_WIDTH + ATTN_KV_DIM
MASKED = -1e30
LOG2E = math.log2(math.e)
SUM_ROWS = 16
MOBA_KEY_GROUP = 16
MOBA_UNROLL = 4
NORM_MARGIN = 1.02
BOUND_LIMIT = 60.0

_KMEAN_BLOCKS = 8


def _kmean_kernel(k_ref, mean_ref, sqmax_ref):
    k = k_ref[...].reshape(_KMEAN_BLOCKS, MOBA_BLOCK, MOBA_K_WIDTH)
    mean_ref[...] = jnp.sum(k, axis=1) * (1.0 / MOBA_BLOCK)
    sqmax_ref[...] = jnp.max(k * k, axis=1)


def _moba_kmean(proj):
    n = proj.shape[0]
    rows = _KMEAN_BLOCKS * MOBA_BLOCK
    out = jax.ShapeDtypeStruct((n // MOBA_BLOCK, MOBA_K_WIDTH), F32)
    spec = pl.BlockSpec((_KMEAN_BLOCKS, MOBA_K_WIDTH), lambda i: (i, 0))
    return pl.pallas_call(
        _kmean_kernel,
        out_shape=(out, out),
        grid=(n // rows,),
        in_specs=[pl.BlockSpec((rows, MOBA_K_WIDTH), lambda i: (i, MOBA_Q_WIDTH // MOBA_K_WIDTH))],
        out_specs=(spec, spec),
        compiler_params=_params(("parallel",)),
        name="moba_kmean",
    )(proj)


def _moba_select_kernel(q_ref, km_ref, expand_ref, slope_ref, slot_ref, k_ref, v_ref,
                        qa_ref, ks_ref, vt_ref, qn_ref, *, nblk):
    n = pl.program_id(1)
    q = q_ref[...] * (HEAD_DIM ** -0.5)
    gate = _dot_split(q, km_ref[0])
    lane_i = lax.broadcasted_iota(jnp.int32, gate.shape, 1)
    blk = jnp.bitwise_and(lane_i, nblk - 1)
    head = jnp.right_shift(lane_i, int(math.log2(nblk)))
    lane = lane_i.astype(F32)
    gate = jnp.where(blk < n, gate, NEG_INF)
    sel = jnp.zeros(gate.shape, F32)
    for h in range(N_KV_HEADS):
        g = jnp.where(head == h, gate, NEG_INF)
        for _ in range(MOBA_TOP_K):
            mx = jnp.max(g, axis=-1, keepdims=True)
            is_max = (g == mx) & (mx > NEG_INF)
            first = jnp.min(jnp.where(is_max, lane, float(4 * LANES)), axis=-1, keepdims=True)
            pick = lane == first
            sel = jnp.where(pick, 1.0, sel)
            g = jnp.where(pick, NEG_INF, g)
    unselected = (1.0 - sel).astype(BF16)
    qs = q * LOG2E
    norm2 = _dot((qs * qs).astype(BF16), slot_ref[...])
    qn_ref[...] = (jnp.sqrt(norm2) * NORM_MARGIN).T[:N_Q_HEADS, :]
    q_feat = qs + slope_ref[...]
    for var in range(qa_ref.shape[0]):
        penalty = _dot(unselected, expand_ref[var])
        qa_ref[var] = (q_feat + penalty).T.astype(BF16)
    ks_ref[0] = k_ref[...].astype(BF16)
    vt_ref[0] = v_ref[...].T.astype(BF16)


def _moba_select(proj, km_mat, expand, slope_feat, batch, seq):
    n = proj.shape[0]
    nblk = seq // MOBA_BLOCK
    width = N_KV_HEADS * nblk
    nvar = expand.shape[0]
    vcol = (MOBA_Q_WIDTH + MOBA_K_WIDTH) // ATTN_KV_DIM
    slot = jnp.arange(MOBA_Q_WIDTH) // MOBA_SLOT
    lane = jnp.arange(MOBA_Q_WIDTH) % MOBA_SLOT
    slot_sum = ((slot[:, None] == jnp.arange(LANES)[None, :]) & (lane[:, None] < HEAD_DIM)).astype(BF16)
    return pl.pallas_call(
        functools.partial(_moba_select_kernel, nblk=nblk),
        out_shape=(jax.ShapeDtypeStruct((nvar, MOBA_Q_WIDTH, n), BF16),
                   jax.ShapeDtypeStruct((n // MOBA_BLOCK, MOBA_BLOCK, MOBA_K_WIDTH), BF16),
                   jax.ShapeDtypeStruct((n // MOBA_BLOCK, ATTN_KV_DIM, MOBA_BLOCK), BF16),
                   jax.ShapeDtypeStruct((N_Q_HEADS, n), F32)),
        grid=(batch, nblk),
        in_specs=[
            pl.BlockSpec((MOBA_BLOCK, MOBA_Q_WIDTH), lambda b, i: (b * nblk + i, 0)),
            pl.BlockSpec((1, MOBA_Q_WIDTH, width), lambda b, i: (b, 0, 0)),
            pl.BlockSpec((nvar, width, MOBA_Q_WIDTH), lambda b, i: (0, 0, 0)),
            pl.BlockSpec((1, MOBA_Q_WIDTH), lambda b, i: (0, 0)),
            pl.BlockSpec((MOBA_Q_WIDTH, LANES), lambda b, i: (0, 0)),
            pl.BlockSpec((MOBA_BLOCK, MOBA_K_WIDTH), lambda b, i: (b * nblk + i, MOBA_Q_WIDTH // MOBA_K_WIDTH)),
            pl.BlockSpec((MOBA_BLOCK, ATTN_KV_DIM), lambda b, i: (b * nblk + i, vcol)),
        ],
        out_specs=(pl.BlockSpec((nvar, MOBA_Q_WIDTH, MOBA_BLOCK), lambda b, i: (0, 0, b * nblk + i)),
                   pl.BlockSpec((1, MOBA_BLOCK, MOBA_K_WIDTH), lambda b, i: (b * nblk + i, 0, 0)),
                   pl.BlockSpec((1, ATTN_KV_DIM, MOBA_BLOCK), lambda b, i: (b * nblk + i, 0, 0)),
                   pl.BlockSpec((N_Q_HEADS, MOBA_BLOCK), lambda b, i: (0, b * nblk + i))),
        compiler_params=_params(("parallel", "parallel")),
        name="moba_select",
    )(proj, km_mat, expand, slope_feat, slot_sum, proj, proj)


def _moba_scores(ka, qa_ref, var, kv):
    heads = [kv * GQA_GROUP + g for g in range(GQA_GROUP)]
    qt = jnp.concatenate([qa_ref[var, hd * MOBA_SLOT:(hd + 1) * MOBA_SLOT, :] for hd in heads], axis=1)
    return _dot(ka, qt)


def _moba_blocks_update(blocks, qa_ref, var, m_ref, l_ref, acc_ref, causal=None, qn_ref=None):
    keys = lambda kv: jnp.concatenate([blk[0](kv) for blk in blocks], axis=0)
    s_next = _moba_scores(keys(0), qa_ref, var, 0)
    for kv in range(N_KV_HEADS):
        s_all = s_next
        if kv + 1 < N_KV_HEADS:
            s_next = _moba_scores(keys(kv + 1), qa_ref, var, kv + 1)
        vt = jnp.concatenate([blk[1](kv) for blk in blocks], axis=1)
        vt = jnp.concatenate([vt, jnp.ones((SUM_ROWS, vt.shape[1]), BF16)], axis=0)
        k_norm = None
        if qn_ref is not None:
            k_norm = functools.reduce(jnp.maximum, [blk[2](kv) for blk in blocks])
        _moba_softmax_pv(s_all, vt, m_ref, l_ref, acc_ref, kv, causal, qn_ref, k_norm)


def _moba_softmax_pv(s_all, vt, m_ref, l_ref, acc_ref, kv, causal, qn_ref, k_norm):
    Lb = MOBA_BLOCK
    heads = [kv * GQA_GROUP + g for g in range(GQA_GROUP)]
    probs, alphas = [], []
    for g, hd in enumerate(heads):
        s = s_all[:, g * Lb:(g + 1) * Lb]
        if causal is not None:
            s = jnp.where(causal, s, MASKED)
        m_old = m_ref[hd:hd + 1, :]
        if k_norm is None:
            m_new = jnp.maximum(m_old, jnp.max(s, axis=0, keepdims=True))
        else:
            qry = lax.broadcasted_iota(jnp.int32, (1, Lb), 1).astype(F32)
            alibi_top = (qry - (Lb - 1)) * (LOG2E * ALIBI_SLOPES[hd])
            m_new = jnp.maximum(m_old, qn_ref[hd:hd + 1, :] * k_norm + alibi_top)
        alphas.append(jnp.exp2(m_old - m_new))
        probs.append(jnp.exp2(s - m_new).astype(BF16))
        m_ref[hd:hd + 1, :] = m_new
    pv_all = _dot(vt, jnp.concatenate(probs, axis=1))
    for g, hd in enumerate(heads):
        pv = pv_all[:, g * Lb:(g + 1) * Lb]
        rows = slice(hd * HEAD_DIM, (hd + 1) * HEAD_DIM)
        l_ref[hd:hd + 1, :] = alphas[g] * l_ref[hd:hd + 1, :] + pv[HEAD_DIM:HEAD_DIM + 1, :]
        acc_ref[rows, :] = alphas[g] * acc_ref[rows, :] + pv[:HEAD_DIM, :]


def _moba_attn_kernel(nq_ref, jg_ref, bounded_ref, qa_ref, k_ref, vt_ref, qn_ref, kn_ref, o_ref,
                      m_ref, l_ref, acc_ref, *, group, nblk):
    Lb = MOBA_BLOCK
    step = pl.program_id(1)
    n = nq_ref[step]
    first = jg_ref[step] * group
    var = first // PEN_LANES
    bounded = bounded_ref[pl.program_id(0) * nblk + n]

    @pl.when(first == 0)
    def _():
        m_ref[...] = jnp.full(m_ref.shape, NEG_INF, F32)
        l_ref[...] = jnp.zeros(l_ref.shape, F32)
        acc_ref[...] = jnp.zeros(acc_ref.shape, F32)

    lane = lax.broadcasted_iota(jnp.int32, (Lb, MOBA_SLOT), 1)
    key_off = (lax.broadcasted_iota(jnp.int32, (Lb, MOBA_SLOT), 0) - (Lb - 1)).astype(F32)
    slope_lanes = (lane == SLOPE_LANES[0]) | (lane == SLOPE_LANES[1])
    stride_lanes = (lane == STRIDE_LANES[0]) | (lane == STRIDE_LANES[1])
    key_feat = jnp.where(slope_lanes, key_off, 0.0)

    def past_block(i):
        j = first + i
        extra = jnp.where(stride_lanes, (j - n).astype(F32), key_feat)
        extra = jnp.where(lane == HEAD_DIM + (j - var * PEN_LANES), 1.0, extra).astype(BF16)
        return (lambda kv: k_ref[i, :, kv * MOBA_SLOT:(kv + 1) * MOBA_SLOT] + extra,
                lambda kv: vt_ref[i, kv * HEAD_DIM:(kv + 1) * HEAD_DIM, :],
                lambda kv: kn_ref[0, pl.ds(i * N_KV_HEADS + kv, 1), :])

    def past_blocks(count, shift_ref):
        def body(it, carry):
            blocks = [past_block(carry + u) for u in range(count)]
            _moba_blocks_update(blocks, qa_ref, var, m_ref, l_ref, acc_ref, None, shift_ref)
            return carry + count
        return body

    n_past = jnp.minimum(n - first, group)
    n_multi = n_past // MOBA_UNROLL

    def run_past(shift_ref):
        done = lax.fori_loop(0, n_multi, past_blocks(MOBA_UNROLL, shift_ref), 0)
        lax.fori_loop(0, n_past - done, past_blocks(1, shift_ref), done)

    @pl.when(bounded == 1)
    def _():
        run_past(qn_ref)

    @pl.when(bounded != 1)
    def _():
        run_past(None)

    @pl.when(n - first < group)
    def _():
        i = n - first
        key = lax.broadcasted_iota(jnp.int32, (Lb, Lb), 0)
        qry = lax.broadcasted_iota(jnp.int32, (Lb, Lb), 1)
        causal = key <= qry
        extra = key_feat.astype(BF16)
        own = (lambda kv: k_ref[i, :, kv * MOBA_SLOT:(kv + 1) * MOBA_SLOT] + extra,
               lambda kv: vt_ref[i, kv * HEAD_DIM:(kv + 1) * HEAD_DIM, :])
        _moba_blocks_update([own], qa_ref, var, m_ref, l_ref, acc_ref, causal)
        inv = 1.0 / l_ref[...]
        acc = acc_ref[...].reshape(N_Q_HEADS, HEAD_DIM, Lb) * inv[:, None, :]
        o_ref[...] = acc.reshape(N_Q_HEADS * HEAD_DIM, Lb).T


def _moba_attn(qaug, kslot, vt, qnorm, knorm, batch, seq):
    n = batch * seq
    nblk = seq // MOBA_BLOCK
    group = min(MOBA_KEY_GROUP, nblk)
    assert nblk % group == 0
    pairs = [(i, jg) for i in range(nblk) for jg in range(i // group + 1)]
    nq = jnp.asarray([p[0] for p in pairs], jnp.int32)
    jk = jnp.asarray([p[1] for p in pairs], jnp.int32)
    ngrp = nblk // group
    nvar = qaug.shape[0]
    assert PEN_LANES % group == 0, "a key group must not straddle two query variants"
    q_top = jnp.max(qnorm.reshape(N_Q_HEADS, batch, nblk, MOBA_BLOCK), axis=(0, 3))
    bounded = (q_top * jnp.max(knorm, axis=(1, 2))[:, None] <= BOUND_LIMIT).astype(jnp.int32).reshape(-1)
    kn_rows = jnp.broadcast_to(knorm.reshape(batch * ngrp, group * N_KV_HEADS, 1),
                               (batch * ngrp, group * N_KV_HEADS, MOBA_BLOCK))
    qmap = lambda b, s, nq_r, jk_r, bd_r: (b * nblk + nq_r[s], 0)
    kmap = lambda b, s, nq_r, jk_r, bd_r: (b * ngrp + jk_r[s], 0, 0)
    grid_spec = pltpu.PrefetchScalarGridSpec(
        num_scalar_prefetch=3,
        grid=(batch, len(pairs)),
        in_specs=[
            pl.BlockSpec((nvar, MOBA_Q_WIDTH, MOBA_BLOCK), lambda b, s, nq_r, jk_r, bd_r: (0, 0, b * nblk + nq_r[s])),
            pl.BlockSpec((group, MOBA_BLOCK, MOBA_K_WIDTH), kmap),
            pl.BlockSpec((group, ATTN_KV_DIM, MOBA_BLOCK), kmap),
            pl.BlockSpec((N_Q_HEADS, MOBA_BLOCK), lambda b, s, nq_r, jk_r, bd_r: (0, b * nblk + nq_r[s])),
            pl.BlockSpec((1, group * N_KV_HEADS, MOBA_BLOCK), kmap),
        ],
        out_specs=pl.BlockSpec((MOBA_BLOCK, ATTN_Q_DIM), qmap),
        scratch_shapes=[
            pltpu.VMEM((N_Q_HEADS, MOBA_BLOCK), F32),
            pltpu.VMEM((N_Q_HEADS, MOBA_BLOCK), F32),
            pltpu.VMEM((N_Q_HEADS * HEAD_DIM, MOBA_BLOCK), F32),
        ],
    )
    return pl.pallas_call(
        functools.partial(_moba_attn_kernel, group=group, nblk=nblk),
        out_shape=jax.ShapeDtypeStruct((n, ATTN_Q_DIM), F32),
        grid_spec=grid_spec,
        compiler_params=_params(("parallel", "arbitrary")),
        name="moba_attn",
    )(nq, jk, bounded, qaug, kslot, vt, qnorm, kn_rows)


def _moba_slot_weights(w_in):
    d = w_in.shape[0]
    fill = MOBA_SLOT - HEAD_DIM
    wq = w_in[:, :ATTN_Q_DIM].reshape(d, N_Q_HEADS, HEAD_DIM)
    wk = w_in[:, ATTN_Q_DIM:ATTN_Q_DIM + ATTN_KV_DIM].reshape(d, N_KV_HEADS, HEAD_DIM)
    wq = jnp.pad(wq, ((0, 0), (0, 0), (0, fill))).reshape(d, MOBA_Q_WIDTH)
    wk = jnp.pad(wk, ((0, 0), (0, 0), (0, fill))).reshape(d, MOBA_K_WIDTH)
    return jnp.concatenate([wq, wk, w_in[:, ATTN_Q_DIM + ATTN_KV_DIM:]], axis=1)


def _moba_core(proj, batch, seq):
    nblk = seq // MOBA_BLOCK
    nvar = -(-nblk // PEN_LANES)
    width = N_KV_HEADS * nblk
    kmean, ksqmax = _moba_kmean(proj)
    knorm = jnp.sqrt(jnp.sum(ksqmax.reshape(batch, nblk, N_KV_HEADS, MOBA_SLOT), axis=-1))
    km = kmean.reshape(batch, nblk, N_KV_HEADS, MOBA_SLOT).transpose(0, 2, 3, 1)
    eye = jnp.eye(N_KV_HEADS, dtype=F32)
    km_mat = jnp.einsum('bhej,hH->bheHj', km, eye)
    km_mat = jnp.broadcast_to(km_mat[:, :, None], (batch, N_KV_HEADS, GQA_GROUP, MOBA_SLOT, N_KV_HEADS, nblk))
    km_mat = km_mat.reshape(batch, MOBA_Q_WIDTH, width)
    h_of = jnp.arange(width) // nblk
    c_of = jnp.arange(width) % nblk
    slot = jnp.arange(MOBA_Q_WIDTH) // MOBA_SLOT
    lane = jnp.arange(MOBA_Q_WIDTH) % MOBA_SLOT
    hit = (h_of[:, None] == (slot // GQA_GROUP)[None, :]) & (lane[None, :] == HEAD_DIM + (c_of % PEN_LANES)[:, None])
    in_var = (c_of // PEN_LANES)[None, :, None] == jnp.arange(nvar)[:, None, None]
    expand = jnp.where(hit[None] & in_var, MASKED, 0.0).astype(BF16)
    sigma = LOG2E * jnp.asarray(ALIBI_SLOPES, F32)
    slope_feat = jnp.zeros((MOBA_Q_WIDTH,), F32)
    for lanes, value in ((SLOPE_LANES, sigma), (STRIDE_LANES, sigma * MOBA_BLOCK)):
        hi = value.astype(BF16).astype(F32)
        lo = (value - hi).astype(BF16).astype(F32)
        slope_feat = slope_feat + jnp.where(lane == lanes[0], hi[slot], 0.0) + jnp.where(lane == lanes[1], lo[slot], 0.0)
    slope_feat = slope_feat.reshape(1, MOBA_Q_WIDTH)
    qaug, kslot, vt, qnorm = _moba_select(proj, km_mat, expand, slope_feat, batch, seq)
    return _moba_attn(qaug, kslot, vt, qnorm, knorm, batch, seq)


LIN_CHUNKS = LIN_ROWS // LIN_CHUNK


def _lin_masks(key_dim):
    R = LIN_ROWS
    chunk_bits = int(math.log2(LIN_CHUNK))
    key_bits = int(math.log2(key_dim))
    assert (1 << chunk_bits) == LIN_CHUNK and (1 << key_bits) == key_dim
    r = lax.broadcasted_iota(jnp.int32, (R, R), 0)
    c = lax.broadcasted_iota(jnp.int32, (R, R), 1)
    same = jnp.right_shift(r, chunk_bits) == jnp.right_shift(c, chunk_bits)
    causal = same & (r >= c)
    sums = jnp.concatenate([jnp.where(causal, 1.0, 0.0), jnp.where(same, 1.0, 0.0)], axis=0).astype(BF16)
    rr = lax.broadcasted_iota(jnp.int32, (R, LIN_CHUNKS * key_dim), 0)
    cc = lax.broadcasted_iota(jnp.int32, (R, LIN_CHUNKS * key_dim), 1)
    own = jnp.right_shift(rr, chunk_bits) == jnp.right_shift(cc, key_bits)
    return sums, causal, own


def _lin_tile(q, k, v, log_g, st_ref, h, masks):
    sums, causal, own = masks
    R, K = q.shape
    hi = log_g.astype(BF16)
    r1 = log_g - hi.astype(F32)
    mid = r1.astype(BF16)
    lo = (r1 - mid.astype(F32)).astype(BF16)
    acc = _dot(sums, jnp.concatenate([hi, mid, lo], axis=1))
    acc = acc[:, :K] + (acc[:, K:2 * K] + acc[:, 2 * K:])
    b, b_chunk = acc[:R], acc[R:]
    q_dec = (q * jnp.exp(b)).astype(BF16)
    k_dec = (k * jnp.exp(-b)).astype(BF16)
    k_tail = (k * jnp.exp(b_chunk - b)).astype(BF16)
    vb = v.astype(BF16)
    attn = jnp.where(causal, _dot_nt(q_dec, k_dec), 0.0).astype(BF16)
    o = _dot(attn, vb)
    tile_lanes = lambda x: jnp.concatenate([x] * LIN_CHUNKS, axis=1)
    zero = jnp.zeros((), BF16)
    ds = _dot_tn(vb, jnp.where(own, tile_lanes(k_tail), zero))
    st = st_ref[h]
    states = []
    for c in range(LIN_CHUNKS):
        states.append(st.astype(BF16))
        decay = jnp.exp(b_chunk[c * LIN_CHUNK:c * LIN_CHUNK + 1, :])
        st = st * decay + ds[:, c * K:(c + 1) * K]
    st_ref[h] = st
    o = o + _dot_nt(jnp.where(own, tile_lanes(q_dec), zero), jnp.concatenate(states, axis=1))
    return o


def _head_norm_gate(o, gain, g):
    o = o * lax.rsqrt(jnp.mean(o * o, axis=-1, keepdims=True) + RMS_EPS) * gain
    return o * _silu(g)


def _gla_kernel(q_ref, k_ref, v_ref, g_ref, a_ref, wd_ref, bd_ref, gain_ref, o_ref, st_ref):
    @pl.when(pl.program_id(1) == 0)
    def _():
        st_ref[...] = jnp.zeros(st_ref.shape, F32)

    K, V = GLA_KEY_DIM, GLA_VAL_DIM
    masks = _lin_masks(K)
    z = _dot_split(a_ref[...], wd_ref[...]) + bd_ref[...]
    log_alpha = (jnp.minimum(z, 0.0) - jnp.log(1.0 + jnp.exp(-jnp.abs(z)))) * (1.0 / GLA_GATE_TEMP)
    for h in range(GLA_HEADS):
        q = q_ref[:, h * K:(h + 1) * K] * (K ** -0.5)
        k = k_ref[:, h * K:(h + 1) * K]
        v = v_ref[:, h * V:(h + 1) * V]
        o = _lin_tile(q, k, v, log_alpha[:, h * K:(h + 1) * K], st_ref, h, masks)
        o_ref[:, h * V:(h + 1) * V] = _head_norm_gate(o, gain_ref[...], g_ref[:, h * V:(h + 1) * V])


def _gla_core(proj, wd_pad, bd, gain, batch, seq):
    n = proj.shape[0]
    nt = seq // LIN_ROWS
    dk = GLA_HEADS * GLA_KEY_DIM
    dv = GLA_HEADS * GLA_VAL_DIM
    rows = lambda c: (lambda b, t: (b * nt + t, c))
    const = lambda b, t: (0, 0)
    return pl.pallas_call(
        _gla_kernel,
        out_shape=jax.ShapeDtypeStruct((n, dv), F32),
        grid=(batch, nt),
        in_specs=[
            pl.BlockSpec((LIN_ROWS, dk), rows(0)),
            pl.BlockSpec((LIN_ROWS, dk), rows(1)),
            pl.BlockSpec((LIN_ROWS, dv), rows(1)),
            pl.BlockSpec((LIN_ROWS, dv), rows(2)),
            pl.BlockSpec((LIN_ROWS, LANES), rows((2 * dk + 2 * dv) // LANES)),
            pl.BlockSpec((LANES, dk), const),
            pl.BlockSpec((1, dk), const),
            pl.BlockSpec((1, GLA_VAL_DIM), const),
        ],
        out_specs=pl.BlockSpec((LIN_ROWS, dv), rows(0)),
        scratch_shapes=[pltpu.VMEM((GLA_HEADS, GLA_VAL_DIM, GLA_KEY_DIM), F32)],
        compiler_params=_params(("parallel", "arbitrary")),
        name="gla_core",
    )(proj, proj, proj, proj, proj, wd_pad, bd.reshape(1, dk), gain.reshape(1, GLA_VAL_DIM))


def _hgrn_kernel(q_ref, f_ref, i_ref, g_ref, lbl_ref, gain_ref, o_ref, st_ref, *, layer):
    @pl.when(pl.program_id(1) == 0)
    def _():
        st_ref[...] = jnp.zeros(st_ref.shape, F32)

    K, V = HGRN_KEY_DIM, HGRN_VAL_DIM
    masks = _lin_masks(K)
    logits = lbl_ref[...]
    e = jnp.exp(logits - jnp.max(logits, axis=0, keepdims=True))
    p = e / jnp.sum(e, axis=0, keepdims=True)
    lb = jnp.zeros((1, logits.shape[1]), F32)
    for l in range(1, layer + 1):
        lb = lb + p[l:l + 1, :]

    for h in range(HGRN_HEADS):
        cols = slice(h * K, (h + 1) * K)
        q = _silu(q_ref[:, cols]) * (K ** -0.5)
        lbh = lb[:, cols]
        f = lbh + (1.0 - lbh) * _sigmoid(f_ref[:, cols])
        o = _lin_tile(q, 1.0 - f, i_ref[:, cols], jnp.log(f), st_ref, h, masks)
        o_ref[:, cols] = _head_norm_gate(o, gain_ref[...], g_ref[:, cols])


def _hgrn_core(proj, lb_logits, gain, layer, batch, seq):
    n = proj.shape[0]
    nt = seq // LIN_ROWS
    d = D_MODEL
    rows = lambda c: (lambda b, t: (b * nt + t, c))
    const = lambda b, t: (0, 0)
    return pl.pallas_call(
        functools.partial(_hgrn_kernel, layer=layer),
        out_shape=jax.ShapeDtypeStruct((n, d), F32),
        grid=(batch, nt),
        in_specs=[
            pl.BlockSpec((LIN_ROWS, d), rows(0)),
            pl.BlockSpec((LIN_ROWS, d), rows(1)),
            pl.BlockSpec((LIN_ROWS, d), rows(2)),
            pl.BlockSpec((LIN_ROWS, d), rows(3)),
            pl.BlockSpec(lb_logits.shape, const),
            pl.BlockSpec((1, HGRN_VAL_DIM), const),
        ],
        out_specs=pl.BlockSpec((LIN_ROWS, d), rows(0)),
        scratch_shapes=[pltpu.VMEM((HGRN_HEADS, HGRN_VAL_DIM, HGRN_KEY_DIM), F32)],
        compiler_params=_params(("parallel", "arbitrary")),
        name="hgrn_core",
    )(proj, proj, proj, proj, lb_logits, gain.reshape(1, HGRN_VAL_DIM))


def kernel(x, norm_mix, norm_ffn, swa_w_in, swa_sinks, swa_w_out, moba_w_in, moba_w_out, gla_w_in, gla_w_decay_up, gla_b_decay, gla_out_norm, gla_w_out, hgrn_w_in, hgrn_lb_logits, hgrn_out_norm, hgrn_w_out, ffn_w_gate_up, ffn_w_down, final_norm):
    batch, seq, d = x.shape
    depth = norm_mix.shape[0]
    xf = x.reshape(batch * seq, d)
    for i in range(depth):
        kind, j = i % N_MIXERS, i // N_MIXERS
        if kind == 0:
            proj = _norm_proj(xf, norm_mix[i], _moba_slot_weights(swa_w_in[j]).astype(BF16), MOBA_WIDTH // 2)
            o = _swa_core(proj, swa_sinks[j], batch, seq)
            w_out = swa_w_out[j]
        elif kind == 1:
            proj = _norm_proj(xf, norm_mix[i], _moba_slot_weights(moba_w_in[j]).astype(BF16), MOBA_WIDTH // 2)
            o = _moba_core(proj, batch, seq)
            w_out = moba_w_out[j]
        elif kind == 2:
            pad = LANES - GLA_GATE_RANK
            w_in = jnp.pad(gla_w_in[j], ((0, 0), (0, pad))).astype(BF16)
            wd_pad = jnp.pad(gla_w_decay_up[j], ((0, pad), (0, 0)))
            proj = _norm_proj(xf, norm_mix[i], w_in, 640)
            o = _gla_core(proj, wd_pad, gla_b_decay[j], gla_out_norm[j], batch, seq)
            w_out = gla_w_out[j]
        else:
            proj = _norm_proj(xf, norm_mix[i], hgrn_w_in[j].astype(BF16), 1024)
            o = _hgrn_core(proj, hgrn_lb_logits, hgrn_out_norm[j], i, batch, seq)
            w_out = hgrn_w_out[j]
        last = i == depth - 1
        xf = _ffn(xf, o, w_out.astype(BF16), norm_ffn[i], ffn_w_gate_up[i].astype(BF16),
                  ffn_w_down[i].astype(BF16), final_norm, last)
    return xf.reshape(batch, seq, d)
```

```python
import functools
import math

import jax
import jax.numpy as jnp
from jax import lax
from jax.experimental import pallas as pl
from jax.experimental.pallas import tpu as pltpu

F32 = jnp.float32
BF16 = jnp.bfloat16

D_MODEL = 1024
HEAD_DIM = 64
N_Q_HEADS = 16
N_KV_HEADS = 4
GQA_GROUP = 4
ATTN_Q_DIM = 1024
ATTN_KV_DIM = 256
SWA_BLOCK = 128
SWA_WINDOW = 128
MOBA_BLOCK = 256
MOBA_TOP_K = 3
GLA_HEADS = 4
GLA_KEY_DIM = 128
GLA_VAL_DIM = 256
GLA_GATE_RANK = 16
GLA_GATE_TEMP = 16.0
HGRN_HEADS = 8
HGRN_KEY_DIM = 128
HGRN_VAL_DIM = 128
LIN_CHUNK = 64
D_FF = 2816
RMS_EPS = 1e-6
N_MIXERS = 4

LANES = 128
VMEM_LIMIT = 56 * 1024 * 1024
FFN_CHUNK = 256
ROW_TILE = 512
PROJ_ROWS = 1024
LIN_ROWS = 256

ALIBI_SLOPES = tuple(2.0 ** (-8.0 * (i + 1) / N_Q_HEADS) for i in range(N_Q_HEADS))
NEG_INF = float("-inf")


def _params(sem):
    return pltpu.CompilerParams(dimension_semantics=sem, vmem_limit_bytes=VMEM_LIMIT)


def _dot(a, b):
    return jnp.dot(a, b, preferred_element_type=F32)


def _dot_nt(a, b):
    return lax.dot_general(a, b, (((1,), (1,)), ((), ())), preferred_element_type=F32)


def _dot_tn(a, b):
    return lax.dot_general(a, b, (((0,), (0,)), ((), ())), preferred_element_type=F32)


def _split(x):
    hi = x.astype(BF16)
    lo = (x - hi.astype(F32)).astype(BF16)
    return hi, lo


def _dot_split(a, b):
    ah, al = _split(a)
    bh, bl = _split(b)
    return _dot(ah, bh) + (_dot(ah, bl) + _dot(al, bh))


def _rms(x, gain):
    return x * lax.rsqrt(jnp.mean(x * x, axis=-1, keepdims=True) + RMS_EPS) * gain


def _sigmoid(x):
    return 1.0 / (1.0 + jnp.exp(-x))


def _silu(x):
    return x * _sigmoid(x)


def _norm_proj_kernel(x_ref, g_ref, w_ref, o_ref, h_ref):
    @pl.when(pl.program_id(1) == 0)
    def _():
        h_ref[...] = _rms(x_ref[...], g_ref[...]).astype(BF16)

    o_ref[...] = _dot(h_ref[...], w_ref[...])


def _norm_proj(x, gain, w, tn):
    n, d = x.shape
    dout = w.shape[1]
    return pl.pallas_call(
        _norm_proj_kernel,
        out_shape=jax.ShapeDtypeStruct((n, dout), F32),
        grid=(n // PROJ_ROWS, dout // tn),
        in_specs=[
            pl.BlockSpec((PROJ_ROWS, d), lambda i, j: (i, 0)),
            pl.BlockSpec((1, d), lambda i, j: (0, 0)),
            pl.BlockSpec((d, tn), lambda i, j: (0, j)),
        ],
        out_specs=pl.BlockSpec((PROJ_ROWS, tn), lambda i, j: (i, j)),
        scratch_shapes=[pltpu.VMEM((PROJ_ROWS, d), BF16)],
        compiler_params=_params(("parallel", "arbitrary")),
        name="norm_proj",
    )(x, gain.reshape(1, d), w)


def _ffn_kernel(x_ref, o_ref, wo_ref, g_ref, wgu_ref, wd_ref, fg_ref, out_ref,
                x1_ref, h_ref, acc_ref, *, final_norm):
    x1 = x_ref[...] + _dot(o_ref[...].astype(BF16), wo_ref[...])
    x1_ref[...] = x1
    h_ref[...] = _rms(x1, g_ref[...]).astype(BF16)
    for c in range(D_FF // FFN_CHUNK):
        h = h_ref[...]
        gate = _dot(h, wgu_ref[:, c * FFN_CHUNK:(c + 1) * FFN_CHUNK])
        up = _dot(h, wgu_ref[:, D_FF + c * FFN_CHUNK:D_FF + (c + 1) * FFN_CHUNK])
        act = (_silu(gate) * up).astype(BF16)
        part = _dot(act, wd_ref[c * FFN_CHUNK:(c + 1) * FFN_CHUNK, :])
        if c == 0:
            acc_ref[...] = part
        else:
            acc_ref[...] += part
    y = x1_ref[...] + acc_ref[...]
    if final_norm:
        y = _rms(y, fg_ref[...])
    out_ref[...] = y


def _ffn(x, o, wo, gain, wgu, wd, final_gain, final_norm):
    n, d = x.shape
    const = lambda i: (0, 0)
    row = lambda i: (i, 0)
    return pl.pallas_call(
        functools.partial(_ffn_kernel, final_norm=final_norm),
        out_shape=jax.ShapeDtypeStruct((n, d), F32),
        grid=(n // ROW_TILE,),
        in_specs=[
            pl.BlockSpec((ROW_TILE, d), row),
            pl.BlockSpec((ROW_TILE, d), row),
            pl.BlockSpec((d, d), const, pipeline_mode=pl.Buffered(1)),
            pl.BlockSpec((1, d), const),
            pl.BlockSpec((d, 2 * D_FF), const, pipeline_mode=pl.Buffered(1)),
            pl.BlockSpec((D_FF, d), const, pipeline_mode=pl.Buffered(1)),
            pl.BlockSpec((1, d), const),
        ],
        out_specs=pl.BlockSpec((ROW_TILE, d), row),
        scratch_shapes=[
            pltpu.VMEM((ROW_TILE, d), F32),
            pltpu.VMEM((ROW_TILE, d), BF16),
            pltpu.VMEM((ROW_TILE, d), F32),
        ],
        compiler_params=_params(("parallel",)),
        name="outproj_ffn",
    )(x, o, wo, gain.reshape(1, d), wgu, wd, final_gain.reshape(1, d))


def _swa_kernel(q_ref, kp_ref, ko_ref, vp_ref, vo_ref, sink_ref, tab_ref, o_ref, ot_ref):
    L = SWA_BLOCK
    n = pl.program_id(1)
    first = jnp.where(n == 0, 1, 0)
    qt = (q_ref[...] * (LOG2E * HEAD_DIM ** -0.5)).T.astype(BF16)
    k = jnp.concatenate([kp_ref[...], ko_ref[...]], axis=0).astype(BF16)
    vt = jnp.concatenate([vp_ref[...], vo_ref[...]], axis=0).T.astype(BF16)
    ones = jnp.ones((SUM_ROWS, 2 * L), BF16)
    sinks = sink_ref[...] * LOG2E
    for kv in range(N_KV_HEADS):
        heads = [kv * GQA_GROUP + g for g in range(GQA_GROUP)]
        qg = jnp.concatenate([qt[hd * HEAD_DIM:(hd + 1) * HEAD_DIM, :] for hd in heads], axis=1)
        s_all = _dot(k[:, kv * HEAD_DIM:(kv + 1) * HEAD_DIM], qg)
        probs, extras = [], []
        for g, hd in enumerate(heads):
            s = s_all[:, g * L:(g + 1) * L] + tab_ref[first, hd]
            sink = sinks[:, hd:hd + 1]
            m = jnp.maximum(jnp.max(s, axis=0, keepdims=True), sink)
            probs.append(jnp.exp2(s - m).astype(BF16))
            extras.append(jnp.exp2(sink - m))
        vaug = jnp.concatenate([vt[kv * HEAD_DIM:(kv + 1) * HEAD_DIM, :], ones], axis=0)
        pv_all = _dot(vaug, jnp.concatenate(probs, axis=1))
        for g, hd in enumerate(heads):
            pv = pv_all[:, g * L:(g + 1) * L]
            denom = pv[HEAD_DIM:HEAD_DIM + 1, :] + extras[g]
            ot_ref[hd * HEAD_DIM:(hd + 1) * HEAD_DIM, :] = pv[:HEAD_DIM, :] / denom
    o_ref[...] = ot_ref[...].T


def _swa_core(proj, sinks, batch, seq):
    n = proj.shape[0]
    L = SWA_BLOCK
    nb = seq // L
    kcol = ATTN_Q_DIM // ATTN_KV_DIM
    vcol = kcol + 1
    own = lambda c: (lambda b, i: (b * nb + i, c))
    prev = lambda c: (lambda b, i: (b * nb + jnp.maximum(i - 1, 0), c))
    key = jnp.arange(2 * L)[:, None]
    qry = jnp.arange(L)[None, :]
    dist = (L + qry) - key
    band = (dist >= 0) & (dist < SWA_WINDOW)
    slopes = (LOG2E * jnp.asarray(ALIBI_SLOPES, F32))[:, None, None]
    bias = -slopes * dist.astype(F32)[None]
    tab = jnp.stack([jnp.where(band[None], bias, MASKED),
                     jnp.where((band & (key >= L))[None], bias, MASKED)])
    return pl.pallas_call(
        _swa_kernel,
        out_shape=jax.ShapeDtypeStruct((n, ATTN_Q_DIM), F32),
        grid=(batch, nb),
        in_specs=[
            pl.BlockSpec((L, ATTN_Q_DIM), own(0)),
            pl.BlockSpec((L, ATTN_KV_DIM), prev(kcol)),
            pl.BlockSpec((L, ATTN_KV_DIM), own(kcol)),
            pl.BlockSpec((L, ATTN_KV_DIM), prev(vcol)),
            pl.BlockSpec((L, ATTN_KV_DIM), own(vcol)),
            pl.BlockSpec((1, N_Q_HEADS), lambda b, i: (0, 0)),
            pl.BlockSpec((2, N_Q_HEADS, 2 * L, L), lambda b, i: (0, 0, 0, 0), pipeline_mode=pl.Buffered(1)),
        ],
        out_specs=pl.BlockSpec((L, ATTN_Q_DIM), own(0)),
        scratch_shapes=[pltpu.VMEM((ATTN_Q_DIM, L), F32)],
        compiler_params=_params(("parallel", "parallel")),
        name="swa_core",
    )(proj, proj, proj, proj, proj, sinks.reshape(1, N_Q_HEADS), tab)


MOBA_SLOT = 2 * HEAD_DIM
PEN_LANES = 32
SLOPE_LANES = (HEAD_DIM + PEN_LANES, HEAD_DIM + PEN_LANES + 1)
STRIDE_LANES = (HEAD_DIM + PEN_LANES + 2, HEAD_DIM + PEN_LANES + 3)
MOBA_Q_WIDTH = N_Q_HEADS * MOBA_SLOT
MOBA_K_WIDTH = N_KV_HEADS * MOBA_SLOT
MOBA_WIDTH = MOBA_Q_WIDTH + MOBA_K_WIDTH + ATTN_KV_DIM
MASKED = -1e30
LOG2E = math.log2(math.e)
SUM_ROWS = 16
MOBA_KEY_GROUP = 16
MOBA_UNROLL = 4
NORM_MARGIN = 1.02
BOUND_LIMIT = 60.0

_KMEAN_BLOCKS = 8


def _kmean_kernel(k_ref, mean_ref, sqmax_ref):
    k = k_ref[...].reshape(_KMEAN_BLOCKS, MOBA_BLOCK, MOBA_K_WIDTH)
    mean_ref[...] = jnp.sum(k, axis=1) * (1.0 / MOBA_BLOCK)
    sqmax_ref[...] = jnp.max(k * k, axis=1)


def _moba_kmean(proj):
    n = proj.shape[0]
    rows = _KMEAN_BLOCKS * MOBA_BLOCK
    out = jax.ShapeDtypeStruct((n // MOBA_BLOCK, MOBA_K_WIDTH), F32)
    spec = pl.BlockSpec((_KMEAN_BLOCKS, MOBA_K_WIDTH), lambda i: (i, 0))
    return pl.pallas_call(
        _kmean_kernel,
        out_shape=(out, out),
        grid=(n // rows,),
        in_specs=[pl.BlockSpec((rows, MOBA_K_WIDTH), lambda i: (i, MOBA_Q_WIDTH // MOBA_K_WIDTH))],
        out_specs=(spec, spec),
        compiler_params=_params(("parallel",)),
        name="moba_kmean",
    )(proj)


def _moba_select_kernel(q_ref, km_ref, expand_ref, slope_ref, slot_ref, k_ref, v_ref,
                        qa_ref, ks_ref, vt_ref, qn_ref, *, nblk):
    n = pl.program_id(1)
    q = q_ref[...] * (HEAD_DIM ** -0.5)
    gate = _dot_split(q, km_ref[0])
    lane_i = lax.broadcasted_iota(jnp.int32, gate.shape, 1)
    blk = jnp.bitwise_and(lane_i, nblk - 1)
    head = jnp.right_shift(lane_i, int(math.log2(nblk)))
    lane = lane_i.astype(F32)
    gate = jnp.where(blk < n, gate, NEG_INF)
    sel = jnp.zeros(gate.shape, F32)
    for h in range(N_KV_HEADS):
        g = jnp.where(head == h, gate, NEG_INF)
        for _ in range(MOBA_TOP_K):
            mx = jnp.max(g, axis=-1, keepdims=True)
            is_max = (g == mx) & (mx > NEG_INF)
            first = jnp.min(jnp.where(is_max, lane, float(4 * LANES)), axis=-1, keepdims=True)
            pick = lane == first
            sel = jnp.where(pick, 1.0, sel)
            g = jnp.where(pick, NEG_INF, g)
    unselected = (1.0 - sel).astype(BF16)
    qs = q * LOG2E
    norm2 = _dot((qs * qs).astype(BF16), slot_ref[...])
    qn_ref[...] = (jnp.sqrt(norm2) * NORM_MARGIN).T[:N_Q_HEADS, :]
    q_feat = qs + slope_ref[...]
    for var in range(qa_ref.shape[0]):
        penalty = _dot(unselected, expand_ref[var])
        qa_ref[var] = (q_feat + penalty).T.astype(BF16)
    ks_ref[0] = k_ref[...].astype(BF16)
    vt_ref[0] = v_ref[...].T.astype(BF16)


def _moba_select(proj, km_mat, expand, slope_feat, batch, seq):
    n = proj.shape[0]
    nblk = seq // MOBA_BLOCK
    width = N_KV_HEADS * nblk
    nvar = expand.shape[0]
    vcol = (MOBA_Q_WIDTH + MOBA_K_WIDTH) // ATTN_KV_DIM
    slot = jnp.arange(MOBA_Q_WIDTH) // MOBA_SLOT
    lane = jnp.arange(MOBA_Q_WIDTH) % MOBA_SLOT
    slot_sum = ((slot[:, None] == jnp.arange(LANES)[None, :]) & (lane[:, None] < HEAD_DIM)).astype(BF16)
    return pl.pallas_call(
        functools.partial(_moba_select_kernel, nblk=nblk),
        out_shape=(jax.ShapeDtypeStruct((nvar, MOBA_Q_WIDTH, n), BF16),
                   jax.ShapeDtypeStruct((n // MOBA_BLOCK, MOBA_BLOCK, MOBA_K_WIDTH), BF16),
                   jax.ShapeDtypeStruct((n // MOBA_BLOCK, ATTN_KV_DIM, MOBA_BLOCK), BF16),
                   jax.ShapeDtypeStruct((N_Q_HEADS, n), F32)),
        grid=(batch, nblk),
        in_specs=[
            pl.BlockSpec((MOBA_BLOCK, MOBA_Q_WIDTH), lambda b, i: (b * nblk + i, 0)),
            pl.BlockSpec((1, MOBA_Q_WIDTH, width), lambda b, i: (b, 0, 0)),
            pl.BlockSpec((nvar, width, MOBA_Q_WIDTH), lambda b, i: (0, 0, 0)),
            pl.BlockSpec((1, MOBA_Q_WIDTH), lambda b, i: (0, 0)),
            pl.BlockSpec((MOBA_Q_WIDTH, LANES), lambda b, i: (0, 0)),
            pl.BlockSpec((MOBA_BLOCK, MOBA_K_WIDTH), lambda b, i: (b * nblk + i, MOBA_Q_WIDTH // MOBA_K_WIDTH)),
            pl.BlockSpec((MOBA_BLOCK, ATTN_KV_DIM), lambda b, i: (b * nblk + i, vcol)),
        ],
        out_specs=(pl.BlockSpec((nvar, MOBA_Q_WIDTH, MOBA_BLOCK), lambda b, i: (0, 0, b * nblk + i)),
                   pl.BlockSpec((1, MOBA_BLOCK, MOBA_K_WIDTH), lambda b, i: (b * nblk + i, 0, 0)),
                   pl.BlockSpec((1, ATTN_KV_DIM, MOBA_BLOCK), lambda b, i: (b * nblk + i, 0, 0)),
                   pl.BlockSpec((N_Q_HEADS, MOBA_BLOCK), lambda b, i: (0, b * nblk + i))),
        compiler_params=_params(("parallel", "parallel")),
        name="moba_select",
    )(proj, km_mat, expand, slope_feat, slot_sum, proj, proj)


def _moba_scores(ka, qa_ref, var, kv):
    heads = [kv * GQA_GROUP + g for g in range(GQA_GROUP)]
    qt = jnp.concatenate([qa_ref[var, hd * MOBA_SLOT:(hd + 1) * MOBA_SLOT, :] for hd in heads], axis=1)
    return _dot(ka, qt)


def _moba_blocks_update(blocks, qa_ref, var, m_ref, l_ref, acc_ref, causal=None, qn_ref=None):
    keys = lambda kv: jnp.concatenate([blk[0](kv) for blk in blocks], axis=0)
    s_next = _moba_scores(keys(0), qa_ref, var, 0)
    for kv in range(N_KV_HEADS):
        s_all = s_next
        if kv + 1 < N_KV_HEADS:
            s_next = _moba_scores(keys(kv + 1), qa_ref, var, kv + 1)
        vt = jnp.concatenate([blk[1](kv) for blk in blocks], axis=1)
        vt = jnp.concatenate([vt, jnp.ones((SUM_ROWS, vt.shape[1]), BF16)], axis=0)
        k_norm = None
        if qn_ref is not None:
            k_norm = functools.reduce(jnp.maximum, [blk[2](kv) for blk in blocks])
        _moba_softmax_pv(s_all, vt, m_ref, l_ref, acc_ref, kv, causal, qn_ref, k_norm)


def _moba_softmax_pv(s_all, vt, m_ref, l_ref, acc_ref, kv, causal, qn_ref, k_norm):
    Lb = MOBA_BLOCK
    heads = [kv * GQA_GROUP + g for g in range(GQA_GROUP)]
    probs, alphas = [], []
    for g, hd in enumerate(heads):
        s = s_all[:, g * Lb:(g + 1) * Lb]
        if causal is not None:
            s = jnp.where(causal, s, MASKED)
        m_old = m_ref[hd:hd + 1, :]
        if k_norm is None:
            m_new = jnp.maximum(m_old, jnp.max(s, axis=0, keepdims=True))
        else:
            qry = lax.broadcasted_iota(jnp.int32, (1, Lb), 1).astype(F32)
            alibi_top = (qry - (Lb - 1)) * (LOG2E * ALIBI_SLOPES[hd])
            m_new = jnp.maximum(m_old, qn_ref[hd:hd + 1, :] * k_norm + alibi_top)
        alphas.append(jnp.exp2(m_old - m_new))
        probs.append(jnp.exp2(s - m_new).astype(BF16))
        m_ref[hd:hd + 1, :] = m_new
    pv_all = _dot(vt, jnp.concatenate(probs, axis=1))
    for g, hd in enumerate(heads):
        pv = pv_all[:, g * Lb:(g + 1) * Lb]
        rows = slice(hd * HEAD_DIM, (hd + 1) * HEAD_DIM)
        l_ref[hd:hd + 1, :] = alphas[g] * l_ref[hd:hd + 1, :] + pv[HEAD_DIM:HEAD_DIM + 1, :]
        acc_ref[rows, :] = alphas[g] * acc_ref[rows, :] + pv[:HEAD_DIM, :]


def _moba_attn_kernel(nq_ref, jg_ref, bounded_ref, qa_ref, k_ref, vt_ref, qn_ref, kn_ref, o_ref,
                      m_ref, l_ref, acc_ref, *, group, nblk):
    Lb = MOBA_BLOCK
    step = pl.program_id(1)
    n = nq_ref[step]
    first = jg_ref[step] * group
    var = first // PEN_LANES
    bounded = bounded_ref[pl.program_id(0) * nblk + n]

    @pl.when(first == 0)
    def _():
        m_ref[...] = jnp.full(m_ref.shape, NEG_INF, F32)
        l_ref[...] = jnp.zeros(l_ref.shape, F32)
        acc_ref[...] = jnp.zeros(acc_ref.shape, F32)

    lane = lax.broadcasted_iota(jnp.int32, (Lb, MOBA_SLOT), 1)
    key_off = (lax.broadcasted_iota(jnp.int32, (Lb, MOBA_SLOT), 0) - (Lb - 1)).astype(F32)
    slope_lanes = (lane == SLOPE_LANES[0]) | (lane == SLOPE_LANES[1])
    stride_lanes = (lane == STRIDE_LANES[0]) | (lane == STRIDE_LANES[1])
    key_feat = jnp.where(slope_lanes, key_off, 0.0)

    def past_block(i):
        j = first + i
        extra = jnp.where(stride_lanes, (j - n).astype(F32), key_feat)
        extra = jnp.where(lane == HEAD_DIM + (j - var * PEN_LANES), 1.0, extra).astype(BF16)
        return (lambda kv: k_ref[i, :, kv * MOBA_SLOT:(kv + 1) * MOBA_SLOT] + extra,
                lambda kv: vt_ref[i, kv * HEAD_DIM:(kv + 1) * HEAD_DIM, :],
                lambda kv: kn_ref[0, pl.ds(i * N_KV_HEADS + kv, 1), :])

    def past_blocks(count, shift_ref):
        def body(it, carry):
            blocks = [past_block(carry + u) for u in range(count)]
            _moba_blocks_update(blocks, qa_ref, var, m_ref, l_ref, acc_ref, None, shift_ref)
            return carry + count
        return body

    n_past = jnp.minimum(n - first, group)
    n_multi = n_past // MOBA_UNROLL

    def run_past(shift_ref):
        done = lax.fori_loop(0, n_multi, past_blocks(MOBA_UNROLL, shift_ref), 0)
        lax.fori_loop(0, n_past - done, past_blocks(1, shift_ref), done)

    @pl.when(bounded == 1)
    def _():
        run_past(qn_ref)

    @pl.when(bounded != 1)
    def _():
        run_past(None)

    @pl.when(n - first < group)
    def _():
        i = n - first
        key = lax.broadcasted_iota(jnp.int32, (Lb, Lb), 0)
        qry = lax.broadcasted_iota(jnp.int32, (Lb, Lb), 1)
        causal = key <= qry
        extra = key_feat.astype(BF16)
        own = (lambda kv: k_ref[i, :, kv * MOBA_SLOT:(kv + 1) * MOBA_SLOT] + extra,
               lambda kv: vt_ref[i, kv * HEAD_DIM:(kv + 1) * HEAD_DIM, :])
        _moba_blocks_update([own], qa_ref, var, m_ref, l_ref, acc_ref, causal)
        inv = 1.0 / l_ref[...]
        acc = acc_ref[...].reshape(N_Q_HEADS, HEAD_DIM, Lb) * inv[:, None, :]
        o_ref[...] = acc.reshape(N_Q_HEADS * HEAD_DIM, Lb).T


def _moba_attn(qaug, kslot, vt, qnorm, knorm, batch, seq):
    n = batch * seq
    nblk = seq // MOBA_BLOCK
    group = min(MOBA_KEY_GROUP, nblk)
    assert nblk % group == 0
    pairs = [(i, jg) for i in range(nblk) for jg in range(i // group + 1)]
    nq = jnp.asarray([p[0] for p in pairs], jnp.int32)
    jk = jnp.asarray([p[1] for p in pairs], jnp.int32)
    ngrp = nblk // group
    nvar = qaug.shape[0]
    assert PEN_LANES % group == 0, "a key group must not straddle two query variants"
    q_top = jnp.max(qnorm.reshape(N_Q_HEADS, batch, nblk, MOBA_BLOCK), axis=(0, 3))
    bounded = (q_top * jnp.max(knorm, axis=(1, 2))[:, None] <= BOUND_LIMIT).astype(jnp.int32).reshape(-1)
    kn_rows = jnp.broadcast_to(knorm.reshape(batch * ngrp, group * N_KV_HEADS, 1),
                               (batch * ngrp, group * N_KV_HEADS, MOBA_BLOCK))
    qmap = lambda b, s, nq_r, jk_r, bd_r: (b * nblk + nq_r[s], 0)
    kmap = lambda b, s, nq_r, jk_r, bd_r: (b * ngrp + jk_r[s], 0, 0)
    grid_spec = pltpu.PrefetchScalarGridSpec(
        num_scalar_prefetch=3,
        grid=(batch, len(pairs)),
        in_specs=[
            pl.BlockSpec((nvar, MOBA_Q_WIDTH, MOBA_BLOCK), lambda b, s, nq_r, jk_r, bd_r: (0, 0, b * nblk + nq_r[s])),
            pl.BlockSpec((group, MOBA_BLOCK, MOBA_K_WIDTH), kmap),
            pl.BlockSpec((group, ATTN_KV_DIM, MOBA_BLOCK), kmap),
            pl.BlockSpec((N_Q_HEADS, MOBA_BLOCK), lambda b, s, nq_r, jk_r, bd_r: (0, b * nblk + nq_r[s])),
            pl.BlockSpec((1, group * N_KV_HEADS, MOBA_BLOCK), kmap),
        ],
        out_specs=pl.BlockSpec((MOBA_BLOCK, ATTN_Q_DIM), qmap),
        scratch_shapes=[
            pltpu.VMEM((N_Q_HEADS, MOBA_BLOCK), F32),
            pltpu.VMEM((N_Q_HEADS, MOBA_BLOCK), F32),
            pltpu.VMEM((N_Q_HEADS * HEAD_DIM, MOBA_BLOCK), F32),
        ],
    )
    return pl.pallas_call(
        functools.partial(_moba_attn_kernel, group=group, nblk=nblk),
        out_shape=jax.ShapeDtypeStruct((n, ATTN_Q_DIM), F32),
        grid_spec=grid_spec,
        compiler_params=_params(("parallel", "arbitrary")),
        name="moba_attn",
    )(nq, jk, bounded, qaug, kslot, vt, qnorm, kn_rows)


def _moba_slot_weights(w_in):
    d = w_in.shape[0]
    fill = MOBA_SLOT - HEAD_DIM
    wq = w_in[:, :ATTN_Q_DIM].reshape(d, N_Q_HEADS, HEAD_DIM)
    wk = w_in[:, ATTN_Q_DIM:ATTN_Q_DIM + ATTN_KV_DIM].reshape(d, N_KV_HEADS, HEAD_DIM)
    wq = jnp.pad(wq, ((0, 0), (0, 0), (0, fill))).reshape(d, MOBA_Q_WIDTH)
    wk = jnp.pad(wk, ((0, 0), (0, 0), (0, fill))).reshape(d, MOBA_K_WIDTH)
    return jnp.concatenate([wq, wk, w_in[:, ATTN_Q_DIM + ATTN_KV_DIM:]], axis=1)


def _moba_core(proj, batch, seq):
    nblk = seq // MOBA_BLOCK
    nvar = -(-nblk // PEN_LANES)
    width = N_KV_HEADS * nblk
    kmean, ksqmax = _moba_kmean(proj)
    knorm = jnp.sqrt(jnp.sum(ksqmax.reshape(batch, nblk, N_KV_HEADS, MOBA_SLOT), axis=-1))
    km = kmean.reshape(batch, nblk, N_KV_HEADS, MOBA_SLOT).transpose(0, 2, 3, 1)
    eye = jnp.eye(N_KV_HEADS, dtype=F32)
    km_mat = jnp.einsum('bhej,hH->bheHj', km, eye)
    km_mat = jnp.broadcast_to(km_mat[:, :, None], (batch, N_KV_HEADS, GQA_GROUP, MOBA_SLOT, N_KV_HEADS, nblk))
    km_mat = km_mat.reshape(batch, MOBA_Q_WIDTH, width)
    h_of = jnp.arange(width) // nblk
    c_of = jnp.arange(width) % nblk
    slot = jnp.arange(MOBA_Q_WIDTH) // MOBA_SLOT
    lane = jnp.arange(MOBA_Q_WIDTH) % MOBA_SLOT
    hit = (h_of[:, None] == (slot // GQA_GROUP)[None, :]) & (lane[None, :] == HEAD_DIM + (c_of % PEN_LANES)[:, None])
    in_var = (c_of // PEN_LANES)[None, :, None] == jnp.arange(nvar)[:, None, None]
    expand = jnp.where(hit[None] & in_var, MASKED, 0.0).astype(BF16)
    sigma = LOG2E * jnp.asarray(ALIBI_SLOPES, F32)
    slope_feat = jnp.zeros((MOBA_Q_WIDTH,), F32)
    for lanes, value in ((SLOPE_LANES, sigma), (STRIDE_LANES, sigma * MOBA_BLOCK)):
        hi = value.astype(BF16).astype(F32)
        lo = (value - hi).astype(BF16).astype(F32)
        slope_feat = slope_feat + jnp.where(lane == lanes[0], hi[slot], 0.0) + jnp.where(lane == lanes[1], lo[slot], 0.0)
    slope_feat = slope_feat.reshape(1, MOBA_Q_WIDTH)
    qaug, kslot, vt, qnorm = _moba_select(proj, km_mat, expand, slope_feat, batch, seq)
    return _moba_attn(qaug, kslot, vt, qnorm, knorm, batch, seq)


LIN_CHUNKS = LIN_ROWS // LIN_CHUNK


def _lin_masks(key_dim):
    R = LIN_ROWS
    chunk_bits = int(math.log2(LIN_CHUNK))
    key_bits = int(math.log2(key_dim))
    assert (1 << chunk_bits) == LIN_CHUNK and (1 << key_bits) == key_dim
    r = lax.broadcasted_iota(jnp.int32, (R, R), 0)
    c = lax.broadcasted_iota(jnp.int32, (R, R), 1)
    same = jnp.right_shift(r, chunk_bits) == jnp.right_shift(c, chunk_bits)
    causal = same & (r >= c)
    sums = jnp.concatenate([jnp.where(causal, 1.0, 0.0), jnp.where(same, 1.0, 0.0)], axis=0).astype(BF16)
    rr = lax.broadcasted_iota(jnp.int32, (R, LIN_CHUNKS * key_dim), 0)
    cc = lax.broadcasted_iota(jnp.int32, (R, LIN_CHUNKS * key_dim), 1)
    own = jnp.right_shift(rr, chunk_bits) == jnp.right_shift(cc, key_bits)
    return sums, causal, own


def _lin_tile(q, k, v, log_g, st_ref, h, masks):
    sums, causal, own = masks
    R, K = q.shape
    hi = log_g.astype(BF16)
    r1 = log_g - hi.astype(F32)
    mid = r1.astype(BF16)
    lo = (r1 - mid.astype(F32)).astype(BF16)
    acc = _dot(sums, jnp.concatenate([hi, mid, lo], axis=1))
    acc = acc[:, :K] + (acc[:, K:2 * K] + acc[:, 2 * K:])
    b, b_chunk = acc[:R], acc[R:]
    q_dec = (q * jnp.exp(b)).astype(BF16)
    k_dec = (k * jnp.exp(-b)).astype(BF16)
    k_tail = (k * jnp.exp(b_chunk - b)).astype(BF16)
    vb = v.astype(BF16)
    attn = jnp.where(causal, _dot_nt(q_dec, k_dec), 0.0).astype(BF16)
    o = _dot(attn, vb)
    tile_lanes = lambda x: jnp.concatenate([x] * LIN_CHUNKS, axis=1)
    zero = jnp.zeros((), BF16)
    ds = _dot_tn(vb, jnp.where(own, tile_lanes(k_tail), zero))
    st = st_ref[h]
    states = []
    for c in range(LIN_CHUNKS):
        states.append(st.astype(BF16))
        decay = jnp.exp(b_chunk[c * LIN_CHUNK:c * LIN_CHUNK + 1, :])
        st = st * decay + ds[:, c * K:(c + 1) * K]
    st_ref[h] = st
    o = o + _dot_nt(jnp.where(own, tile_lanes(q_dec), zero), jnp.concatenate(states, axis=1))
    return o


def _head_norm_gate(o, gain, g):
    o = o * lax.rsqrt(jnp.mean(o * o, axis=-1, keepdims=True) + RMS_EPS) * gain
    return o * _silu(g)


def _gla_kernel(q_ref, k_ref, v_ref, g_ref, a_ref, wd_ref, bd_ref, gain_ref, o_ref, st_ref):
    @pl.when(pl.program_id(1) == 0)
    def _():
        st_ref[...] = jnp.zeros(st_ref.shape, F32)

    K, V = GLA_KEY_DIM, GLA_VAL_DIM
    masks = _lin_masks(K)
    z = _dot_split(a_ref[...], wd_ref[...]) + bd_ref[...]
    log_alpha = (jnp.minimum(z, 0.0) - jnp.log(1.0 + jnp.exp(-jnp.abs(z)))) * (1.0 / GLA_GATE_TEMP)
    for h in range(GLA_HEADS):
        q = q_ref[:, h * K:(h + 1) * K] * (K ** -0.5)
        k = k_ref[:, h * K:(h + 1) * K]
        v = v_ref[:, h * V:(h + 1) * V]
        o = _lin_tile(q, k, v, log_alpha[:, h * K:(h + 1) * K], st_ref, h, masks)
        o_ref[:, h * V:(h + 1) * V] = _head_norm_gate(o, gain_ref[...], g_ref[:, h * V:(h + 1) * V])


def _gla_core(proj, wd_pad, bd, gain, batch, seq):
    n = proj.shape[0]
    nt = seq // LIN_ROWS
    dk = GLA_HEADS * GLA_KEY_DIM
    dv = GLA_HEADS * GLA_VAL_DIM
    rows = lambda c: (lambda b, t: (b * nt + t, c))
    const = lambda b, t: (0, 0)
    return pl.pallas_call(
        _gla_kernel,
        out_shape=jax.ShapeDtypeStruct((n, dv), F32),
        grid=(batch, nt),
        in_specs=[
            pl.BlockSpec((LIN_ROWS, dk), rows(0)),
            pl.BlockSpec((LIN_ROWS, dk), rows(1)),
            pl.BlockSpec((LIN_ROWS, dv), rows(1)),
            pl.BlockSpec((LIN_ROWS, dv), rows(2)),
            pl.BlockSpec((LIN_ROWS, LANES), rows((2 * dk + 2 * dv) // LANES)),
            pl.BlockSpec((LANES, dk), const),
            pl.BlockSpec((1, dk), const),
            pl.BlockSpec((1, GLA_VAL_DIM), const),
        ],
        out_specs=pl.BlockSpec((LIN_ROWS, dv), rows(0)),
        scratch_shapes=[pltpu.VMEM((GLA_HEADS, GLA_VAL_DIM, GLA_KEY_DIM), F32)],
        compiler_params=_params(("parallel", "arbitrary")),
        name="gla_core",
    )(proj, proj, proj, proj, proj, wd_pad, bd.reshape(1, dk), gain.reshape(1, GLA_VAL_DIM))


def _hgrn_kernel(q_ref, f_ref, i_ref, g_ref, lbl_ref, gain_ref, o_ref, st_ref, *, layer):
    @pl.when(pl.program_id(1) == 0)
    def _():
        st_ref[...] = jnp.zeros(st_ref.shape, F32)

    K, V = HGRN_KEY_DIM, HGRN_VAL_DIM
    masks = _lin_masks(K)
    logits = lbl_ref[...]
    e = jnp.exp(logits - jnp.max(logits, axis=0, keepdims=True))
    p = e / jnp.sum(e, axis=0, keepdims=True)
    lb = jnp.zeros((1, logits.shape[1]), F32)
    for l in range(1, layer + 1):
        lb = lb + p[l:l + 1, :]

    for h in range(HGRN_HEADS):
        cols = slice(h * K, (h + 1) * K)
        q = _silu(q_ref[:, cols]) * (K ** -0.5)
        lbh = lb[:, cols]
        f = lbh + (1.0 - lbh) * _sigmoid(f_ref[:, cols])
        o = _lin_tile(q, 1.0 - f, i_ref[:, cols], jnp.log(f), st_ref, h, masks)
        o_ref[:, cols] = _head_norm_gate(o, gain_ref[...], g_ref[:, cols])


def _hgrn_core(proj, lb_logits, gain, layer, batch, seq):
    n = proj.shape[0]
    nt = seq // LIN_ROWS
    d = D_MODEL
    rows = lambda c: (lambda b, t: (b * nt + t, c))
    const = lambda b, t: (0, 0)
    return pl.pallas_call(
        functools.partial(_hgrn_kernel, layer=layer),
        out_shape=jax.ShapeDtypeStruct((n, d), F32),
        grid=(batch, nt),
        in_specs=[
            pl.BlockSpec((LIN_ROWS, d), rows(0)),
            pl.BlockSpec((LIN_ROWS, d), rows(1)),
            pl.BlockSpec((LIN_ROWS, d), rows(2)),
            pl.BlockSpec((LIN_ROWS, d), rows(3)),
            pl.BlockSpec(lb_logits.shape, const),
            pl.BlockSpec((1, HGRN_VAL_DIM), const),
        ],
        out_specs=pl.BlockSpec((LIN_ROWS, d), rows(0)),
        scratch_shapes=[pltpu.VMEM((HGRN_HEADS, HGRN_VAL_DIM, HGRN_KEY_DIM), F32)],
        compiler_params=_params(("parallel", "arbitrary")),
        name="hgrn_core",
    )(proj, proj, proj, proj, lb_logits, gain.reshape(1, HGRN_VAL_DIM))


def kernel(x, norm_mix, norm_ffn, swa_w_in, swa_sinks, swa_w_out, moba_w_in, moba_w_out, gla_w_in, gla_w_decay_up, gla_b_decay, gla_out_norm, gla_w_out, hgrn_w_in, hgrn_lb_logits, hgrn_out_norm, hgrn_w_out, ffn_w_gate_up, ffn_w_down, final_norm):
    batch, seq, d = x.shape
    depth = norm_mix.shape[0]
    xf = x.reshape(batch * seq, d)
    for i in range(depth):
        kind, j = i % N_MIXERS, i // N_MIXERS
        if kind == 0:
            proj = _norm_proj(xf, norm_mix[i], swa_w_in[j].astype(BF16), 768)
            o = _swa_core(proj, swa_sinks[j], batch, seq)
            w_out = swa_w_out[j]
        elif kind == 1:
            proj = _norm_proj(xf, norm_mix[i], _moba_slot_weights(moba_w_in[j]).astype(BF16), MOBA_WIDTH // 2)
            o = _moba_core(proj, batch, seq)
            w_out = moba_w_out[j]
        elif kind == 2:
            pad = LANES - GLA_GATE_RANK
            w_in = jnp.pad(gla_w_in[j], ((0, 0), (0, pad))).astype(BF16)
            wd_pad = jnp.pad(gla_w_decay_up[j], ((0, pad), (0, 0)))
            proj = _norm_proj(xf, norm_mix[i], w_in, 640)
            o = _gla_core(proj, wd_pad, gla_b_decay[j], gla_out_norm[j], batch, seq)
            w_out = gla_w_out[j]
        else:
            proj = _norm_proj(xf, norm_mix[i], hgrn_w_in[j].astype(BF16), 1024)
            o = _hgrn_core(proj, hgrn_lb_logits, hgrn_out_norm[j], i, batch, seq)
            w_out = hgrn_w_out[j]
        last = i == depth - 1
        xf = _ffn(xf, o, w_out.astype(BF16), norm_ffn[i], ffn_w_gate_up[i].astype(BF16),
                  ffn_w_down[i].astype(BF16), final_norm, last)
    return xf.reshape(batch, seq, d)
```

```python
import functools
import math

import jax
import jax.numpy as jnp
from jax import lax
from jax.experimental import pallas as pl
from jax.experimental.pallas import tpu as pltpu

F32 = jnp.float32
BF16 = jnp.bfloat16

D_MODEL = 1024
HEAD_DIM = 64
N_Q_HEADS = 16
N_KV_HEADS = 4
GQA_GROUP = 4
ATTN_Q_DIM = 1024
ATTN_KV_DIM = 256
SWA_BLOCK = 128
SWA_WINDOW = 128
MOBA_BLOCK = 256
MOBA_TOP_K = 3
GLA_HEADS = 4
GLA_KEY_DIM = 128
GLA_VAL_DIM = 256
GLA_GATE_RANK = 16
GLA_GATE_TEMP = 16.0
HGRN_HEADS = 8
HGRN_KEY_DIM = 128
HGRN_VAL_DIM = 128
LIN_CHUNK = 64
D_FF = 2816
RMS_EPS = 1e-6
N_MIXERS = 4

LANES = 128
VMEM_LIMIT = 56 * 1024 * 1024
FFN_CHUNK = 256
ROW_TILE = 512
PROJ_ROWS = 1024
LIN_ROWS = 256

ALIBI_SLOPES = tuple(2.0 ** (-8.0 * (i + 1) / N_Q_HEADS) for i in range(N_Q_HEADS))
NEG_INF = float("-inf")


def _params(sem):
    return pltpu.CompilerParams(dimension_semantics=sem, vmem_limit_bytes=VMEM_LIMIT)


def _dot(a, b):
    return jnp.dot(a, b, preferred_element_type=F32)


def _dot_nt(a, b):
    return lax.dot_general(a, b, (((1,), (1,)), ((), ())), preferred_element_type=F32)


def _dot_tn(a, b):
    return lax.dot_general(a, b, (((0,), (0,)), ((), ())), preferred_element_type=F32)


def _split(x):
    hi = x.astype(BF16)
    lo = (x - hi.astype(F32)).astype(BF16)
    return hi, lo


def _dot_split(a, b):
    ah, al = _split(a)
    bh, bl = _split(b)
    return _dot(ah, bh) + (_dot(ah, bl) + _dot(al, bh))


def _rms(x, gain):
    return x * lax.rsqrt(jnp.mean(x * x, axis=-1, keepdims=True) + RMS_EPS) * gain


def _sigmoid(x):
    return 1.0 / (1.0 + jnp.exp(-x))


def _silu(x):
    return x * _sigmoid(x)


def _norm_proj_kernel(x_ref, g_ref, w_ref, o_ref, h_ref):
    @pl.when(pl.program_id(1) == 0)
    def _():
        h_ref[...] = _rms(x_ref[...], g_ref[...]).astype(BF16)

    o_ref[...] = _dot(h_ref[...], w_ref[...])


def _norm_proj(x, gain, w, tn):
    n, d = x.shape
    dout = w.shape[1]
    return pl.pallas_call(
        _norm_proj_kernel,
        out_shape=jax.ShapeDtypeStruct((n, dout), F32),
        grid=(n // PROJ_ROWS, dout // tn),
        in_specs=[
            pl.BlockSpec((PROJ_ROWS, d), lambda i, j: (i, 0)),
            pl.BlockSpec((1, d), lambda i, j: (0, 0)),
            pl.BlockSpec((d, tn), lambda i, j: (0, j)),
        ],
        out_specs=pl.BlockSpec((PROJ_ROWS, tn), lambda i, j: (i, j)),
        scratch_shapes=[pltpu.VMEM((PROJ_ROWS, d), BF16)],
        compiler_params=_params(("parallel", "arbitrary")),
        name="norm_proj",
    )(x, gain.reshape(1, d), w)


def _ffn_kernel(x_ref, o_ref, wo_ref, g_ref, wgu_ref, wd_ref, fg_ref, out_ref,
                x1_ref, h_ref, acc_ref, *, final_norm):
    x1 = x_ref[...] + _dot(o_ref[...].astype(BF16), wo_ref[...])
    x1_ref[...] = x1
    h_ref[...] = _rms(x1, g_ref[...]).astype(BF16)
    for c in range(D_FF // FFN_CHUNK):
        h = h_ref[...]
        gate = _dot(h, wgu_ref[:, c * FFN_CHUNK:(c + 1) * FFN_CHUNK])
        up = _dot(h, wgu_ref[:, D_FF + c * FFN_CHUNK:D_FF + (c + 1) * FFN_CHUNK])
        act = (_silu(gate) * up).astype(BF16)
        part = _dot(act, wd_ref[c * FFN_CHUNK:(c + 1) * FFN_CHUNK, :])
        if c == 0:
            acc_ref[...] = part
        else:
            acc_ref[...] += part
    y = x1_ref[...] + acc_ref[...]
    if final_norm:
        y = _rms(y, fg_ref[...])
    out_ref[...] = y


def _ffn(x, o, wo, gain, wgu, wd, final_gain, final_norm):
    n, d = x.shape
    const = lambda i: (0, 0)
    row = lambda i: (i, 0)
    return pl.pallas_call(
        functools.partial(_ffn_kernel, final_norm=final_norm),
        out_shape=jax.ShapeDtypeStruct((n, d), F32),
        grid=(n // ROW_TILE,),
        in_specs=[
            pl.BlockSpec((ROW_TILE, d), row),
            pl.BlockSpec((ROW_TILE, d), row),
            pl.BlockSpec((d, d), const, pipeline_mode=pl.Buffered(1)),
            pl.BlockSpec((1, d), const),
            pl.BlockSpec((d, 2 * D_FF), const, pipeline_mode=pl.Buffered(1)),
            pl.BlockSpec((D_FF, d), const, pipeline_mode=pl.Buffered(1)),
            pl.BlockSpec((1, d), const),
        ],
        out_specs=pl.BlockSpec((ROW_TILE, d), row),
        scratch_shapes=[
            pltpu.VMEM((ROW_TILE, d), F32),
            pltpu.VMEM((ROW_TILE, d), BF16),
            pltpu.VMEM((ROW_TILE, d), F32),
        ],
        compiler_params=_params(("parallel",)),
        name="outproj_ffn",
    )(x, o, wo, gain.reshape(1, d), wgu, wd, final_gain.reshape(1, d))


def _swa_kernel(q_ref, kp_ref, ko_ref, vp_ref, vo_ref, sink_ref, tab_ref, o_ref, ot_ref):
    L = SWA_BLOCK
    n = pl.program_id(1)
    first = jnp.where(n == 0, 1, 0)
    qt = (q_ref[...] * (LOG2E * HEAD_DIM ** -0.5)).T.astype(BF16)
    k = jnp.concatenate([kp_ref[...], ko_ref[...]], axis=0).astype(BF16)
    vt = jnp.concatenate([vp_ref[...], vo_ref[...]], axis=0).T.astype(BF16)
    ones = jnp.ones((SUM_ROWS, 2 * L), BF16)
    sinks = sink_ref[...] * LOG2E
    def scores(kv):
        rows = [qt[hd * HEAD_DIM:(hd + 1) * HEAD_DIM, :] for hd in range(kv * GQA_GROUP, (kv + 1) * GQA_GROUP)]
        return _dot(k[:, kv * HEAD_DIM:(kv + 1) * HEAD_DIM], jnp.concatenate(rows, axis=1))

    s_next = scores(0)
    for kv in range(N_KV_HEADS):
        heads = [kv * GQA_GROUP + g for g in range(GQA_GROUP)]
        s_all = s_next
        if kv + 1 < N_KV_HEADS:
            s_next = scores(kv + 1)
        probs, extras = [], []
        for g, hd in enumerate(heads):
            s = s_all[:, g * L:(g + 1) * L] + tab_ref[first, hd]
            sink = sinks[:, hd:hd + 1]
            m = jnp.maximum(jnp.max(s, axis=0, keepdims=True), sink)
            probs.append(jnp.exp2(s - m).astype(BF16))
            extras.append(jnp.exp2(sink - m))
        vaug = jnp.concatenate([vt[kv * HEAD_DIM:(kv + 1) * HEAD_DIM, :], ones], axis=0)
        pv_all = _dot(vaug, jnp.concatenate(probs, axis=1))
        for g, hd in enumerate(heads):
            pv = pv_all[:, g * L:(g + 1) * L]
            denom = pv[HEAD_DIM:HEAD_DIM + 1, :] + extras[g]
            ot_ref[hd * HEAD_DIM:(hd + 1) * HEAD_DIM, :] = pv[:HEAD_DIM, :] / denom
    o_ref[...] = ot_ref[...].T


def _swa_core(proj, sinks, batch, seq):
    n = proj.shape[0]
    L = SWA_BLOCK
    nb = seq // L
    kcol = ATTN_Q_DIM // ATTN_KV_DIM
    vcol = kcol + 1
    own = lambda c: (lambda b, i: (b * nb + i, c))
    prev = lambda c: (lambda b, i: (b * nb + jnp.maximum(i - 1, 0), c))
    key = jnp.arange(2 * L)[:, None]
    qry = jnp.arange(L)[None, :]
    dist = (L + qry) - key
    band = (dist >= 0) & (dist < SWA_WINDOW)
    slopes = (LOG2E * jnp.asarray(ALIBI_SLOPES, F32))[:, None, None]
    bias = -slopes * dist.astype(F32)[None]
    tab = jnp.stack([jnp.where(band[None], bias, MASKED),
                     jnp.where((band & (key >= L))[None], bias, MASKED)])
    return pl.pallas_call(
        _swa_kernel,
        out_shape=jax.ShapeDtypeStruct((n, ATTN_Q_DIM), F32),
        grid=(batch, nb),
        in_specs=[
            pl.BlockSpec((L, ATTN_Q_DIM), own(0)),
            pl.BlockSpec((L, ATTN_KV_DIM), prev(kcol)),
            pl.BlockSpec((L, ATTN_KV_DIM), own(kcol)),
            pl.BlockSpec((L, ATTN_KV_DIM), prev(vcol)),
            pl.BlockSpec((L, ATTN_KV_DIM), own(vcol)),
            pl.BlockSpec((1, N_Q_HEADS), lambda b, i: (0, 0)),
            pl.BlockSpec((2, N_Q_HEADS, 2 * L, L), lambda b, i: (0, 0, 0, 0), pipeline_mode=pl.Buffered(1)),
        ],
        out_specs=pl.BlockSpec((L, ATTN_Q_DIM), own(0)),
        scratch_shapes=[pltpu.VMEM((ATTN_Q_DIM, L), F32)],
        compiler_params=_params(("parallel", "parallel")),
        name="swa_core",
    )(proj, proj, proj, proj, proj, sinks.reshape(1, N_Q_HEADS), tab)


MOBA_SLOT = 2 * HEAD_DIM
PEN_LANES = 32
SLOPE_LANES = (HEAD_DIM + PEN_LANES, HEAD_DIM + PEN_LANES + 1)
STRIDE_LANES = (HEAD_DIM + PEN_LANES + 2, HEAD_DIM + PEN_LANES + 3)
MOBA_Q_WIDTH = N_Q_HEADS * MOBA_SLOT
MOBA_K_WIDTH = N_KV_HEADS * MOBA_SLOT
MOBA_WIDTH = MOBA_Q_WIDTH + MOBA_K_WIDTH + ATTN_KV_DIM
MASKED = -1e30
LOG2E = math.log2(math.e)
SUM_ROWS = 16
MOBA_KEY_GROUP = 16
MOBA_UNROLL = 4
NORM_MARGIN = 1.02
BOUND_LIMIT = 60.0

_KMEAN_BLOCKS = 8


def _kmean_kernel(k_ref, mean_ref, sqmax_ref):
    k = k_ref[...].reshape(_KMEAN_BLOCKS, MOBA_BLOCK, MOBA_K_WIDTH)
    mean_ref[...] = jnp.sum(k, axis=1) * (1.0 / MOBA_BLOCK)
    sqmax_ref[...] = jnp.max(k * k, axis=1)


def _moba_kmean(proj):
    n = proj.shape[0]
    rows = _KMEAN_BLOCKS * MOBA_BLOCK
    out = jax.ShapeDtypeStruct((n // MOBA_BLOCK, MOBA_K_WIDTH), F32)
    spec = pl.BlockSpec((_KMEAN_BLOCKS, MOBA_K_WIDTH), lambda i: (i, 0))
    return pl.pallas_call(
        _kmean_kernel,
        out_shape=(out, out),
        grid=(n // rows,),
        in_specs=[pl.BlockSpec((rows, MOBA_K_WIDTH), lambda i: (i, MOBA_Q_WIDTH // MOBA_K_WIDTH))],
        out_specs=(spec, spec),
        compiler_params=_params(("parallel",)),
        name="moba_kmean",
    )(proj)


def _moba_select_kernel(q_ref, km_ref, expand_ref, slope_ref, slot_ref, k_ref, v_ref,
                        qa_ref, ks_ref, vt_ref, qn_ref, *, nblk):
    n = pl.program_id(1)
    q = q_ref[...] * (HEAD_DIM ** -0.5)
    gate = _dot_split(q, km_ref[0])
    lane_i = lax.broadcasted_iota(jnp.int32, gate.shape, 1)
    blk = jnp.bitwise_and(lane_i, nblk - 1)
    head = jnp.right_shift(lane_i, int(math.log2(nblk)))
    lane = lane_i.astype(F32)
    gate = jnp.where(blk < n, gate, NEG_INF)
    sel = jnp.zeros(gate.shape, F32)
    for h in range(N_KV_HEADS):
        g = jnp.where(head == h, gate, NEG_INF)
        for _ in range(MOBA_TOP_K):
            mx = jnp.max(g, axis=-1, keepdims=True)
            is_max = (g == mx) & (mx > NEG_INF)
            first = jnp.min(jnp.where(is_max, lane, float(4 * LANES)), axis=-1, keepdims=True)
            pick = lane == first
            sel = jnp.where(pick, 1.0, sel)
            g = jnp.where(pick, NEG_INF, g)
    unselected = (1.0 - sel).astype(BF16)
    qs = q * LOG2E
    norm2 = _dot((qs * qs).astype(BF16), slot_ref[...])
    qn_ref[...] = (jnp.sqrt(norm2) * NORM_MARGIN).T[:N_Q_HEADS, :]
    q_feat = qs + slope_ref[...]
    for var in range(qa_ref.shape[0]):
        penalty = _dot(unselected, expand_ref[var])
        qa_ref[var] = (q_feat + penalty).T.astype(BF16)
    ks_ref[0] = k_ref[...].astype(BF16)
    vt_ref[0] = v_ref[...].T.astype(BF16)


def _moba_select(proj, km_mat, expand, slope_feat, batch, seq):
    n = proj.shape[0]
    nblk = seq // MOBA_BLOCK
    width = N_KV_HEADS * nblk
    nvar = expand.shape[0]
    vcol = (MOBA_Q_WIDTH + MOBA_K_WIDTH) // ATTN_KV_DIM
    slot = jnp.arange(MOBA_Q_WIDTH) // MOBA_SLOT
    lane = jnp.arange(MOBA_Q_WIDTH) % MOBA_SLOT
    slot_sum = ((slot[:, None] == jnp.arange(LANES)[None, :]) & (lane[:, None] < HEAD_DIM)).astype(BF16)
    return pl.pallas_call(
        functools.partial(_moba_select_kernel, nblk=nblk),
        out_shape=(jax.ShapeDtypeStruct((nvar, MOBA_Q_WIDTH, n), BF16),
                   jax.ShapeDtypeStruct((n // MOBA_BLOCK, MOBA_BLOCK, MOBA_K_WIDTH), BF16),
                   jax.ShapeDtypeStruct((n // MOBA_BLOCK, ATTN_KV_DIM, MOBA_BLOCK), BF16),
                   jax.ShapeDtypeStruct((N_Q_HEADS, n), F32)),
        grid=(batch, nblk),
        in_specs=[
            pl.BlockSpec((MOBA_BLOCK, MOBA_Q_WIDTH), lambda b, i: (b * nblk + i, 0)),
            pl.BlockSpec((1, MOBA_Q_WIDTH, width), lambda b, i: (b, 0, 0)),
            pl.BlockSpec((nvar, width, MOBA_Q_WIDTH), lambda b, i: (0, 0, 0)),
            pl.BlockSpec((1, MOBA_Q_WIDTH), lambda b, i: (0, 0)),
            pl.BlockSpec((MOBA_Q_WIDTH, LANES), lambda b, i: (0, 0)),
            pl.BlockSpec((MOBA_BLOCK, MOBA_K_WIDTH), lambda b, i: (b * nblk + i, MOBA_Q_WIDTH // MOBA_K_WIDTH)),
            pl.BlockSpec((MOBA_BLOCK, ATTN_KV_DIM), lambda b, i: (b * nblk + i, vcol)),
        ],
        out_specs=(pl.BlockSpec((nvar, MOBA_Q_WIDTH, MOBA_BLOCK), lambda b, i: (0, 0, b * nblk + i)),
                   pl.BlockSpec((1, MOBA_BLOCK, MOBA_K_WIDTH), lambda b, i: (b * nblk + i, 0, 0)),
                   pl.BlockSpec((1, ATTN_KV_DIM, MOBA_BLOCK), lambda b, i: (b * nblk + i, 0, 0)),
                   pl.BlockSpec((N_Q_HEADS, MOBA_BLOCK), lambda b, i: (0, b * nblk + i))),
        compiler_params=_params(("parallel", "parallel")),
        name="moba_select",
    )(proj, km_mat, expand, slope_feat, slot_sum, proj, proj)


def _moba_scores(ka, qa_ref, var, kv):
    heads = [kv * GQA_GROUP + g for g in range(GQA_GROUP)]
    qt = jnp.concatenate([qa_ref[var, hd * MOBA_SLOT:(hd + 1) * MOBA_SLOT, :] for hd in heads], axis=1)
    return _dot(ka, qt)


def _moba_blocks_update(blocks, qa_ref, var, m_ref, l_ref, acc_ref, causal=None, qn_ref=None):
    keys = lambda kv: jnp.concatenate([blk[0](kv) for blk in blocks], axis=0)
    s_next = _moba_scores(keys(0), qa_ref, var, 0)
    for kv in range(N_KV_HEADS):
        s_all = s_next
        if kv + 1 < N_KV_HEADS:
            s_next = _moba_scores(keys(kv + 1), qa_ref, var, kv + 1)
        vt = jnp.concatenate([blk[1](kv) for blk in blocks], axis=1)
        vt = jnp.concatenate([vt, jnp.ones((SUM_ROWS, vt.shape[1]), BF16)], axis=0)
        k_norm = None
        if qn_ref is not None:
            k_norm = functools.reduce(jnp.maximum, [blk[2](kv) for blk in blocks])
        _moba_softmax_pv(s_all, vt, m_ref, l_ref, acc_ref, kv, causal, qn_ref, k_norm)


def _moba_softmax_pv(s_all, vt, m_ref, l_ref, acc_ref, kv, causal, qn_ref, k_norm):
    Lb = MOBA_BLOCK
    heads = [kv * GQA_GROUP + g for g in range(GQA_GROUP)]
    probs, alphas = [], []
    for g, hd in enumerate(heads):
        s = s_all[:, g * Lb:(g + 1) * Lb]
        if causal is not None:
            s = jnp.where(causal, s, MASKED)
        m_old = m_ref[hd:hd + 1, :]
        if k_norm is None:
            m_new = jnp.maximum(m_old, jnp.max(s, axis=0, keepdims=True))
        else:
            qry = lax.broadcasted_iota(jnp.int32, (1, Lb), 1).astype(F32)
            alibi_top = (qry - (Lb - 1)) * (LOG2E * ALIBI_SLOPES[hd])
            m_new = jnp.maximum(m_old, qn_ref[hd:hd + 1, :] * k_norm + alibi_top)
        alphas.append(jnp.exp2(m_old - m_new))
        probs.append(jnp.exp2(s - m_new).astype(BF16))
        m_ref[hd:hd + 1, :] = m_new
    pv_all = _dot(vt, jnp.concatenate(probs, axis=1))
    for g, hd in enumerate(heads):
        pv = pv_all[:, g * Lb:(g + 1) * Lb]
        rows = slice(hd * HEAD_DIM, (hd + 1) * HEAD_DIM)
        l_ref[hd:hd + 1, :] = alphas[g] * l_ref[hd:hd + 1, :] + pv[HEAD_DIM:HEAD_DIM + 1, :]
        acc_ref[rows, :] = alphas[g] * acc_ref[rows, :] + pv[:HEAD_DIM, :]


def _moba_attn_kernel(nq_ref, jg_ref, bounded_ref, qa_ref, k_ref, vt_ref, qn_ref, kn_ref, o_ref,
                      m_ref, l_ref, acc_ref, *, group, nblk):
    Lb = MOBA_BLOCK
    step = pl.program_id(1)
    n = nq_ref[step]
    first = jg_ref[step] * group
    var = first // PEN_LANES
    bounded = bounded_ref[pl.program_id(0) * nblk + n]

    @pl.when(first == 0)
    def _():
        m_ref[...] = jnp.full(m_ref.shape, NEG_INF, F32)
        l_ref[...] = jnp.zeros(l_ref.shape, F32)
        acc_ref[...] = jnp.zeros(acc_ref.shape, F32)

    lane = lax.broadcasted_iota(jnp.int32, (Lb, MOBA_SLOT), 1)
    key_off = (lax.broadcasted_iota(jnp.int32, (Lb, MOBA_SLOT), 0) - (Lb - 1)).astype(F32)
    slope_lanes = (lane == SLOPE_LANES[0]) | (lane == SLOPE_LANES[1])
    stride_lanes = (lane == STRIDE_LANES[0]) | (lane == STRIDE_LANES[1])
    key_feat = jnp.where(slope_lanes, key_off, 0.0)

    def past_block(i):
        j = first + i
        extra = jnp.where(stride_lanes, (j - n).astype(F32), key_feat)
        extra = jnp.where(lane == HEAD_DIM + (j - var * PEN_LANES), 1.0, extra).astype(BF16)
        return (lambda kv: k_ref[i, :, kv * MOBA_SLOT:(kv + 1) * MOBA_SLOT] + extra,
                lambda kv: vt_ref[i, kv * HEAD_DIM:(kv + 1) * HEAD_DIM, :],
                lambda kv: kn_ref[0, pl.ds(i * N_KV_HEADS + kv, 1), :])

    def past_blocks(count, shift_ref):
        def body(it, carry):
            blocks = [past_block(carry + u) for u in range(count)]
            _moba_blocks_update(blocks, qa_ref, var, m_ref, l_ref, acc_ref, None, shift_ref)
            return carry + count
        return body

    n_past = jnp.minimum(n - first, group)
    n_multi = n_past // MOBA_UNROLL

    def run_past(shift_ref):
        done = lax.fori_loop(0, n_multi, past_blocks(MOBA_UNROLL, shift_ref), 0)
        lax.fori_loop(0, n_past - done, past_blocks(1, shift_ref), done)

    @pl.when(bounded == 1)
    def _():
        run_past(qn_ref)

    @pl.when(bounded != 1)
    def _():
        run_past(None)

    @pl.when(n - first < group)
    def _():
        i = n - first
        key = lax.broadcasted_iota(jnp.int32, (Lb, Lb), 0)
        qry = lax.broadcasted_iota(jnp.int32, (Lb, Lb), 1)
        causal = key <= qry
        extra = key_feat.astype(BF16)
        own = (lambda kv: k_ref[i, :, kv * MOBA_SLOT:(kv + 1) * MOBA_SLOT] + extra,
               lambda kv: vt_ref[i, kv * HEAD_DIM:(kv + 1) * HEAD_DIM, :])
        _moba_blocks_update([own], qa_ref, var, m_ref, l_ref, acc_ref, causal)
        inv = 1.0 / l_ref[...]
        acc = acc_ref[...].reshape(N_Q_HEADS, HEAD_DIM, Lb) * inv[:, None, :]
        o_ref[...] = acc.reshape(N_Q_HEADS * HEAD_DIM, Lb).T


def _moba_attn(qaug, kslot, vt, qnorm, knorm, batch, seq):
    n = batch * seq
    nblk = seq // MOBA_BLOCK
    group = min(MOBA_KEY_GROUP, nblk)
    assert nblk % group == 0
    pairs = [(i, jg) for i in range(nblk) for jg in range(i // group + 1)]
    nq = jnp.asarray([p[0] for p in pairs], jnp.int32)
    jk = jnp.asarray([p[1] for p in pairs], jnp.int32)
    ngrp = nblk // group
    nvar = qaug.shape[0]
    assert PEN_LANES % group == 0, "a key group must not straddle two query variants"
    q_top = jnp.max(qnorm.reshape(N_Q_HEADS, batch, nblk, MOBA_BLOCK), axis=(0, 3))
    bounded = (q_top * jnp.max(knorm, axis=(1, 2))[:, None] <= BOUND_LIMIT).astype(jnp.int32).reshape(-1)
    kn_rows = jnp.broadcast_to(knorm.reshape(batch * ngrp, group * N_KV_HEADS, 1),
                               (batch * ngrp, group * N_KV_HEADS, MOBA_BLOCK))
    qmap = lambda b, s, nq_r, jk_r, bd_r: (b * nblk + nq_r[s], 0)
    kmap = lambda b, s, nq_r, jk_r, bd_r: (b * ngrp + jk_r[s], 0, 0)
    grid_spec = pltpu.PrefetchScalarGridSpec(
        num_scalar_prefetch=3,
        grid=(batch, len(pairs)),
        in_specs=[
            pl.BlockSpec((nvar, MOBA_Q_WIDTH, MOBA_BLOCK), lambda b, s, nq_r, jk_r, bd_r: (0, 0, b * nblk + nq_r[s])),
            pl.BlockSpec((group, MOBA_BLOCK, MOBA_K_WIDTH), kmap),
            pl.BlockSpec((group, ATTN_KV_DIM, MOBA_BLOCK), kmap),
            pl.BlockSpec((N_Q_HEADS, MOBA_BLOCK), lambda b, s, nq_r, jk_r, bd_r: (0, b * nblk + nq_r[s])),
            pl.BlockSpec((1, group * N_KV_HEADS, MOBA_BLOCK), kmap),
        ],
        out_specs=pl.BlockSpec((MOBA_BLOCK, ATTN_Q_DIM), qmap),
        scratch_shapes=[
            pltpu.VMEM((N_Q_HEADS, MOBA_BLOCK), F32),
            pltpu.VMEM((N_Q_HEADS, MOBA_BLOCK), F32),
            pltpu.VMEM((N_Q_HEADS * HEAD_DIM, MOBA_BLOCK), F32),
        ],
    )
    return pl.pallas_call(
        functools.partial(_moba_attn_kernel, group=group, nblk=nblk),
        out_shape=jax.ShapeDtypeStruct((n, ATTN_Q_DIM), F32),
        grid_spec=grid_spec,
        compiler_params=_params(("parallel", "arbitrary")),
        name="moba_attn",
    )(nq, jk, bounded, qaug, kslot, vt, qnorm, kn_rows)


def _moba_slot_weights(w_in):
    d = w_in.shape[0]
    fill = MOBA_SLOT - HEAD_DIM
    wq = w_in[:, :ATTN_Q_DIM].reshape(d, N_Q_HEADS, HEAD_DIM)
    wk = w_in[:, ATTN_Q_DIM:ATTN_Q_DIM + ATTN_KV_DIM].reshape(d, N_KV_HEADS, HEAD_DIM)
    wq = jnp.pad(wq, ((0, 0), (0, 0), (0, fill))).reshape(d, MOBA_Q_WIDTH)
    wk = jnp.pad(wk, ((0, 0), (0, 0), (0, fill))).reshape(d, MOBA_K_WIDTH)
    return jnp.concatenate([wq, wk, w_in[:, ATTN_Q_DIM + ATTN_KV_DIM:]], axis=1)


def _moba_core(proj, batch, seq):
    nblk = seq // MOBA_BLOCK
    nvar = -(-nblk // PEN_LANES)
    width = N_KV_HEADS * nblk
    kmean, ksqmax = _moba_kmean(proj)
    knorm = jnp.sqrt(jnp.sum(ksqmax.reshape(batch, nblk, N_KV_HEADS, MOBA_SLOT), axis=-1))
    km = kmean.reshape(batch, nblk, N_KV_HEADS, MOBA_SLOT).transpose(0, 2, 3, 1)
    eye = jnp.eye(N_KV_HEADS, dtype=F32)
    km_mat = jnp.einsum('bhej,hH->bheHj', km, eye)
    km_mat = jnp.broadcast_to(km_mat[:, :, None], (batch, N_KV_HEADS, GQA_GROUP, MOBA_SLOT, N_KV_HEADS, nblk))
    km_mat = km_mat.reshape(batch, MOBA_Q_WIDTH, width)
    h_of = jnp.arange(width) // nblk
    c_of = jnp.arange(width) % nblk
    slot = jnp.arange(MOBA_Q_WIDTH) // MOBA_SLOT
    lane = jnp.arange(MOBA_Q_WIDTH) % MOBA_SLOT
    hit = (h_of[:, None] == (slot // GQA_GROUP)[None, :]) & (lane[None, :] == HEAD_DIM + (c_of % PEN_LANES)[:, None])
    in_var = (c_of // PEN_LANES)[None, :, None] == jnp.arange(nvar)[:, None, None]
    expand = jnp.where(hit[None] & in_var, MASKED, 0.0).astype(BF16)
    sigma = LOG2E * jnp.asarray(ALIBI_SLOPES, F32)
    slope_feat = jnp.zeros((MOBA_Q_WIDTH,), F32)
    for lanes, value in ((SLOPE_LANES, sigma), (STRIDE_LANES, sigma * MOBA_BLOCK)):
        hi = value.astype(BF16).astype(F32)
        lo = (value - hi).astype(BF16).astype(F32)
        slope_feat = slope_feat + jnp.where(lane == lanes[0], hi[slot], 0.0) + jnp.where(lane == lanes[1], lo[slot], 0.0)
    slope_feat = slope_feat.reshape(1, MOBA_Q_WIDTH)
    qaug, kslot, vt, qnorm = _moba_select(proj, km_mat, expand, slope_feat, batch, seq)
    return _moba_attn(qaug, kslot, vt, qnorm, knorm, batch, seq)


LIN_CHUNKS = LIN_ROWS // LIN_CHUNK


def _lin_masks(key_dim):
    R = LIN_ROWS
    chunk_bits = int(math.log2(LIN_CHUNK))
    key_bits = int(math.log2(key_dim))
    assert (1 << chunk_bits) == LIN_CHUNK and (1 << key_bits) == key_dim
    r = lax.broadcasted_iota(jnp.int32, (R, R), 0)
    c = lax.broadcasted_iota(jnp.int32, (R, R), 1)
    same = jnp.right_shift(r, chunk_bits) == jnp.right_shift(c, chunk_bits)
    causal = same & (r >= c)
    sums = jnp.concatenate([jnp.where(causal, 1.0, 0.0), jnp.where(same, 1.0, 0.0)], axis=0).astype(BF16)
    rr = lax.broadcasted_iota(jnp.int32, (R, LIN_CHUNKS * key_dim), 0)
    cc = lax.broadcasted_iota(jnp.int32, (R, LIN_CHUNKS * key_dim), 1)
    own = jnp.right_shift(rr, chunk_bits) == jnp.right_shift(cc, key_bits)
    return sums, causal, own


def _lin_prefix(log_g, sums):
    hi = log_g.astype(BF16)
    r1 = log_g - hi.astype(F32)
    mid = r1.astype(BF16)
    lo = (r1 - mid.astype(F32)).astype(BF16)
    return _dot(sums, jnp.concatenate([hi, mid, lo], axis=1))


def _lin_heads(n_heads, head_inputs, emit, st_ref, masks):
    cur = head_inputs(0)
    pre = _lin_prefix(cur[3], masks[0])
    for h in range(n_heads):
        nxt = pre_nxt = None
        if h + 1 < n_heads:
            nxt = head_inputs(h + 1)
            pre_nxt = _lin_prefix(nxt[3], masks[0])
        emit(h, _lin_tile(cur[0], cur[1], cur[2], pre, st_ref, h, masks))
        cur, pre = nxt, pre_nxt


def _lin_tile(q, k, v, acc, st_ref, h, masks):
    _, causal, own = masks
    R, K = q.shape
    acc = acc[:, :K] + (acc[:, K:2 * K] + acc[:, 2 * K:])
    b, b_chunk = acc[:R], acc[R:]
    q_dec = (q * jnp.exp(b)).astype(BF16)
    k_dec = (k * jnp.exp(-b)).astype(BF16)
    k_tail = (k * jnp.exp(b_chunk - b)).astype(BF16)
    vb = v.astype(BF16)
    attn = jnp.where(causal, _dot_nt(q_dec, k_dec), 0.0).astype(BF16)
    o = _dot(attn, vb)
    tile_lanes = lambda x: jnp.concatenate([x] * LIN_CHUNKS, axis=1)
    zero = jnp.zeros((), BF16)
    ds = _dot_tn(vb, jnp.where(own, tile_lanes(k_tail), zero))
    st = st_ref[h]
    states = []
    for c in range(LIN_CHUNKS):
        states.append(st.astype(BF16))
        decay = jnp.exp(b_chunk[c * LIN_CHUNK:c * LIN_CHUNK + 1, :])
        st = st * decay + ds[:, c * K:(c + 1) * K]
    st_ref[h] = st
    o = o + _dot_nt(jnp.where(own, tile_lanes(q_dec), zero), jnp.concatenate(states, axis=1))
    return o


def _head_norm_gate(o, gain, g):
    o = o * lax.rsqrt(jnp.mean(o * o, axis=-1, keepdims=True) + RMS_EPS) * gain
    return o * _silu(g)


def _gla_kernel(q_ref, k_ref, v_ref, g_ref, a_ref, wd_ref, bd_ref, gain_ref, o_ref, st_ref):
    @pl.when(pl.program_id(1) == 0)
    def _():
        st_ref[...] = jnp.zeros(st_ref.shape, F32)

    K, V = GLA_KEY_DIM, GLA_VAL_DIM
    masks = _lin_masks(K)
    z = _dot_split(a_ref[...], wd_ref[...]) + bd_ref[...]
    log_alpha = (jnp.minimum(z, 0.0) - jnp.log(1.0 + jnp.exp(-jnp.abs(z)))) * (1.0 / GLA_GATE_TEMP)
    def head_inputs(h):
        return (q_ref[:, h * K:(h + 1) * K] * (K ** -0.5), k_ref[:, h * K:(h + 1) * K],
                v_ref[:, h * V:(h + 1) * V], log_alpha[:, h * K:(h + 1) * K])

    def emit(h, o):
        o_ref[:, h * V:(h + 1) * V] = _head_norm_gate(o, gain_ref[...], g_ref[:, h * V:(h + 1) * V])

    _lin_heads(GLA_HEADS, head_inputs, emit, st_ref, masks)


def _gla_core(proj, wd_pad, bd, gain, batch, seq):
    n = proj.shape[0]
    nt = seq // LIN_ROWS
    dk = GLA_HEADS * GLA_KEY_DIM
    dv = GLA_HEADS * GLA_VAL_DIM
    rows = lambda c: (lambda b, t: (b * nt + t, c))
    const = lambda b, t: (0, 0)
    return pl.pallas_call(
        _gla_kernel,
        out_shape=jax.ShapeDtypeStruct((n, dv), F32),
        grid=(batch, nt),
        in_specs=[
            pl.BlockSpec((LIN_ROWS, dk), rows(0)),
            pl.BlockSpec((LIN_ROWS, dk), rows(1)),
            pl.BlockSpec((LIN_ROWS, dv), rows(1)),
            pl.BlockSpec((LIN_ROWS, dv), rows(2)),
            pl.BlockSpec((LIN_ROWS, LANES), rows((2 * dk + 2 * dv) // LANES)),
            pl.BlockSpec((LANES, dk), const),
            pl.BlockSpec((1, dk), const),
            pl.BlockSpec((1, GLA_VAL_DIM), const),
        ],
        out_specs=pl.BlockSpec((LIN_ROWS, dv), rows(0)),
        scratch_shapes=[pltpu.VMEM((GLA_HEADS, GLA_VAL_DIM, GLA_KEY_DIM), F32)],
        compiler_params=_params(("parallel", "arbitrary")),
        name="gla_core",
    )(proj, proj, proj, proj, proj, wd_pad, bd.reshape(1, dk), gain.reshape(1, GLA_VAL_DIM))


def _hgrn_kernel(q_ref, f_ref, i_ref, g_ref, lbl_ref, gain_ref, o_ref, st_ref, *, layer):
    @pl.when(pl.program_id(1) == 0)
    def _():
        st_ref[...] = jnp.zeros(st_ref.shape, F32)

    K, V = HGRN_KEY_DIM, HGRN_VAL_DIM
    masks = _lin_masks(K)
    logits = lbl_ref[...]
    e = jnp.exp(logits - jnp.max(logits, axis=0, keepdims=True))
    p = e / jnp.sum(e, axis=0, keepdims=True)
    lb = jnp.zeros((1, logits.shape[1]), F32)
    for l in range(1, layer + 1):
        lb = lb + p[l:l + 1, :]

    def head_inputs(h):
        cols = slice(h * K, (h + 1) * K)
        lbh = lb[:, cols]
        f = lbh + (1.0 - lbh) * _sigmoid(f_ref[:, cols])
        return _silu(q_ref[:, cols]) * (K ** -0.5), 1.0 - f, i_ref[:, cols], jnp.log(f)

    def emit(h, o):
        cols = slice(h * K, (h + 1) * K)
        o_ref[:, cols] = _head_norm_gate(o, gain_ref[...], g_ref[:, cols])

    _lin_heads(HGRN_HEADS, head_inputs, emit, st_ref, masks)


def _hgrn_core(proj, lb_logits, gain, layer, batch, seq):
    n = proj.shape[0]
    nt = seq // LIN_ROWS
    d = D_MODEL
    rows = lambda c: (lambda b, t: (b * nt + t, c))
    const = lambda b, t: (0, 0)
    return pl.pallas_call(
        functools.partial(_hgrn_kernel, layer=layer),
        out_shape=jax.ShapeDtypeStruct((n, d), F32),
        grid=(batch, nt),
        in_specs=[
            pl.BlockSpec((LIN_ROWS, d), rows(0)),
            pl.BlockSpec((LIN_ROWS, d), rows(1)),
            pl.BlockSpec((LIN_ROWS, d), rows(2)),
            pl.BlockSpec((LIN_ROWS, d), rows(3)),
            pl.BlockSpec(lb_logits.shape, const),
            pl.BlockSpec((1, HGRN_VAL_DIM), const),
        ],
        out_specs=pl.BlockSpec((LIN_ROWS, d), rows(0)),
        scratch_shapes=[pltpu.VMEM((HGRN_HEADS, HGRN_VAL_DIM, HGRN_KEY_DIM), F32)],
        compiler_params=_params(("parallel", "arbitrary")),
        name="hgrn_core",
    )(proj, proj, proj, proj, lb_logits, gain.reshape(1, HGRN_VAL_DIM))


def kernel(x, norm_mix, norm_ffn, swa_w_in, swa_sinks, swa_w_out, moba_w_in, moba_w_out, gla_w_in, gla_w_decay_up, gla_b_decay, gla_out_norm, gla_w_out, hgrn_w_in, hgrn_lb_logits, hgrn_out_norm, hgrn_w_out, ffn_w_gate_up, ffn_w_down, final_norm):
    batch, seq, d = x.shape
    depth = norm_mix.shape[0]
    xf = x.reshape(batch * seq, d)
    for i in range(depth):
        kind, j = i % N_MIXERS, i // N_MIXERS
        if kind == 0:
            proj = _norm_proj(xf, norm_mix[i], swa_w_in[j].astype(BF16), 768)
            o = _swa_core(proj, swa_sinks[j], batch, seq)
            w_out = swa_w_out[j]
        elif kind == 1:
            proj = _norm_proj(xf, norm_mix[i], _moba_slot_weights(moba_w_in[j]).astype(BF16), MOBA_WIDTH // 2)
            o = _moba_core(proj, batch, seq)
            w_out = moba_w_out[j]
        elif kind == 2:
            pad = LANES - GLA_GATE_RANK
            w_in = jnp.pad(gla_w_in[j], ((0, 0), (0, pad))).astype(BF16)
            wd_pad = jnp.pad(gla_w_decay_up[j], ((0, pad), (0, 0)))
            proj = _norm_proj(xf, norm_mix[i], w_in, 640)
            o = _gla_core(proj, wd_pad, gla_b_decay[j], gla_out_norm[j], batch, seq)
            w_out = gla_w_out[j]
        else:
            proj = _norm_proj(xf, norm_mix[i], hgrn_w_in[j].astype(BF16), 1024)
            o = _hgrn_core(proj, hgrn_lb_logits, hgrn_out_norm[j], i, batch, seq)
            w_out = hgrn_w_out[j]
        last = i == depth - 1
        xf = _ffn(xf, o, w_out.astype(BF16), norm_ffn[i], ffn_w_gate_up[i].astype(BF16),
                  ffn_w_down[i].astype(BF16), final_norm, last)
    return xf.reshape(batch, seq, d)
```

```python
import functools
import math

import jax
import jax.numpy as jnp
from jax import lax
from jax.experimental import pallas as pl
from jax.experimental.pallas import tpu as pltpu

F32 = jnp.float32
BF16 = jnp.bfloat16

D_MODEL = 1024
HEAD_DIM = 64
N_Q_HEADS = 16
N_KV_HEADS = 4
GQA_GROUP = 4
ATTN_Q_DIM = 1024
ATTN_KV_DIM = 256
SWA_BLOCK = 128
SWA_WINDOW = 128
MOBA_BLOCK = 256
MOBA_TOP_K = 3
GLA_HEADS = 4
GLA_KEY_DIM = 128
GLA_VAL_DIM = 256
GLA_GATE_RANK = 16
GLA_GATE_TEMP = 16.0
HGRN_HEADS = 8
HGRN_KEY_DIM = 128
HGRN_VAL_DIM = 128
LIN_CHUNK = 64
D_FF = 2816
RMS_EPS = 1e-6
N_MIXERS = 4

LANES = 128
VMEM_LIMIT = 56 * 1024 * 1024
FFN_CHUNK = 256
ROW_TILE = 512
PROJ_ROWS = 1024
LIN_ROWS = 256

ALIBI_SLOPES = tuple(2.0 ** (-8.0 * (i + 1) / N_Q_HEADS) for i in range(N_Q_HEADS))
NEG_INF = float("-inf")


def _params(sem):
    return pltpu.CompilerParams(dimension_semantics=sem, vmem_limit_bytes=VMEM_LIMIT)


def _dot(a, b):
    return jnp.dot(a, b, preferred_element_type=F32)


def _dot_nt(a, b):
    return lax.dot_general(a, b, (((1,), (1,)), ((), ())), preferred_element_type=F32)


def _dot_tn(a, b):
    return lax.dot_general(a, b, (((0,), (0,)), ((), ())), preferred_element_type=F32)


def _split(x):
    hi = x.astype(BF16)
    lo = (x - hi.astype(F32)).astype(BF16)
    return hi, lo


def _dot_split(a, b):
    ah, al = _split(a)
    bh, bl = _split(b)
    return _dot(ah, bh) + (_dot(ah, bl) + _dot(al, bh))


def _rms(x, gain):
    return x * lax.rsqrt(jnp.mean(x * x, axis=-1, keepdims=True) + RMS_EPS) * gain


def _sigmoid(x):
    return 1.0 / (1.0 + jnp.exp(-x))


def _silu(x):
    return x * _sigmoid(x)


def _norm_proj_kernel(x_ref, g_ref, w_ref, o_ref, h_ref):
    @pl.when(pl.program_id(1) == 0)
    def _():
        h_ref[...] = _rms(x_ref[...], g_ref[...]).astype(BF16)

    o_ref[...] = _dot(h_ref[...], w_ref[...])


def _norm_proj(x, gain, w, tn):
    n, d = x.shape
    dout = w.shape[1]
    return pl.pallas_call(
        _norm_proj_kernel,
        out_shape=jax.ShapeDtypeStruct((n, dout), F32),
        grid=(n // PROJ_ROWS, dout // tn),
        in_specs=[
            pl.BlockSpec((PROJ_ROWS, d), lambda i, j: (i, 0)),
            pl.BlockSpec((1, d), lambda i, j: (0, 0)),
            pl.BlockSpec((d, tn), lambda i, j: (0, j)),
        ],
        out_specs=pl.BlockSpec((PROJ_ROWS, tn), lambda i, j: (i, j)),
        scratch_shapes=[pltpu.VMEM((PROJ_ROWS, d), BF16)],
        compiler_params=_params(("parallel", "arbitrary")),
        name="norm_proj",
    )(x, gain.reshape(1, d), w)


def _ffn_kernel(x_ref, o_ref, wo_ref, g_ref, wgu_ref, wd_ref, fg_ref, out_ref,
                x1_ref, h_ref, acc_ref, *, final_norm):
    x1 = x_ref[...] + _dot(o_ref[...].astype(BF16), wo_ref[...])
    x1_ref[...] = x1
    h_ref[...] = _rms(x1, g_ref[...]).astype(BF16)
    for c in range(D_FF // FFN_CHUNK):
        h = h_ref[...]
        gate = _dot(h, wgu_ref[:, c * FFN_CHUNK:(c + 1) * FFN_CHUNK])
        up = _dot(h, wgu_ref[:, D_FF + c * FFN_CHUNK:D_FF + (c + 1) * FFN_CHUNK])
        act = (_silu(gate) * up).astype(BF16)
        part = _dot(act, wd_ref[c * FFN_CHUNK:(c + 1) * FFN_CHUNK, :])
        if c == 0:
            acc_ref[...] = part
        else:
            acc_ref[...] += part
    y = x1_ref[...] + acc_ref[...]
    if final_norm:
        y = _rms(y, fg_ref[...])
    out_ref[...] = y


def _ffn(x, o, wo, gain, wgu, wd, final_gain, final_norm):
    n, d = x.shape
    const = lambda i: (0, 0)
    row = lambda i: (i, 0)
    return pl.pallas_call(
        functools.partial(_ffn_kernel, final_norm=final_norm),
        out_shape=jax.ShapeDtypeStruct((n, d), F32),
        grid=(n // ROW_TILE,),
        in_specs=[
            pl.BlockSpec((ROW_TILE, d), row),
            pl.BlockSpec((ROW_TILE, d), row),
            pl.BlockSpec((d, d), const, pipeline_mode=pl.Buffered(1)),
            pl.BlockSpec((1, d), const),
            pl.BlockSpec((d, 2 * D_FF), const, pipeline_mode=pl.Buffered(1)),
            pl.BlockSpec((D_FF, d), const, pipeline_mode=pl.Buffered(1)),
            pl.BlockSpec((1, d), const),
        ],
        out_specs=pl.BlockSpec((ROW_TILE, d), row),
        scratch_shapes=[
            pltpu.VMEM((ROW_TILE, d), F32),
            pltpu.VMEM((ROW_TILE, d), BF16),
            pltpu.VMEM((ROW_TILE, d), F32),
        ],
        compiler_params=_params(("parallel",)),
        name="outproj_ffn",
    )(x, o, wo, gain.reshape(1, d), wgu, wd, final_gain.reshape(1, d))


def _swa_kernel(q_ref, kp_ref, ko_ref, vp_ref, vo_ref, sink_ref, tab_ref, o_ref, ot_ref):
    L = SWA_BLOCK
    n = pl.program_id(1)
    first = jnp.where(n == 0, 1, 0)
    qt = (q_ref[...] * (LOG2E * HEAD_DIM ** -0.5)).T.astype(BF16)
    k = jnp.concatenate([kp_ref[...], ko_ref[...]], axis=0).astype(BF16)
    vt = jnp.concatenate([vp_ref[...], vo_ref[...]], axis=0).T.astype(BF16)
    ones = jnp.ones((SUM_ROWS, 2 * L), BF16)
    sinks = sink_ref[...] * LOG2E
    def scores(kv):
        rows = [qt[hd * HEAD_DIM:(hd + 1) * HEAD_DIM, :] for hd in range(kv * GQA_GROUP, (kv + 1) * GQA_GROUP)]
        return _dot(k[:, kv * HEAD_DIM:(kv + 1) * HEAD_DIM], jnp.concatenate(rows, axis=1))

    s_next = scores(0)
    for kv in range(N_KV_HEADS):
        heads = [kv * GQA_GROUP + g for g in range(GQA_GROUP)]
        s_all = s_next
        if kv + 1 < N_KV_HEADS:
            s_next = scores(kv + 1)
        probs, extras = [], []
        for g, hd in enumerate(heads):
            s = s_all[:, g * L:(g + 1) * L] + tab_ref[first, hd]
            sink = sinks[:, hd:hd + 1]
            m = jnp.maximum(jnp.max(s, axis=0, keepdims=True), sink)
            probs.append(jnp.exp2(s - m).astype(BF16))
            extras.append(jnp.exp2(sink - m))
        vaug = jnp.concatenate([vt[kv * HEAD_DIM:(kv + 1) * HEAD_DIM, :], ones], axis=0)
        pv_all = _dot(vaug, jnp.concatenate(probs, axis=1))
        for g, hd in enumerate(heads):
            pv = pv_all[:, g * L:(g + 1) * L]
            denom = pv[HEAD_DIM:HEAD_DIM + 1, :] + extras[g]
            ot_ref[hd * HEAD_DIM:(hd + 1) * HEAD_DIM, :] = pv[:HEAD_DIM, :] / denom
    o_ref[...] = ot_ref[...].T


def _swa_core(proj, sinks, batch, seq):
    n = proj.shape[0]
    L = SWA_BLOCK
    nb = seq // L
    kcol = ATTN_Q_DIM // ATTN_KV_DIM
    vcol = kcol + 1
    own = lambda c: (lambda b, i: (b * nb + i, c))
    prev = lambda c: (lambda b, i: (b * nb + jnp.maximum(i - 1, 0), c))
    key = jnp.arange(2 * L)[:, None]
    qry = jnp.arange(L)[None, :]
    dist = (L + qry) - key
    band = (dist >= 0) & (dist < SWA_WINDOW)
    slopes = (LOG2E * jnp.asarray(ALIBI_SLOPES, F32))[:, None, None]
    bias = -slopes * dist.astype(F32)[None]
    tab = jnp.stack([jnp.where(band[None], bias, MASKED),
                     jnp.where((band & (key >= L))[None], bias, MASKED)])
    return pl.pallas_call(
        _swa_kernel,
        out_shape=jax.ShapeDtypeStruct((n, ATTN_Q_DIM), F32),
        grid=(batch, nb),
        in_specs=[
            pl.BlockSpec((L, ATTN_Q_DIM), own(0)),
            pl.BlockSpec((L, ATTN_KV_DIM), prev(kcol)),
            pl.BlockSpec((L, ATTN_KV_DIM), own(kcol)),
            pl.BlockSpec((L, ATTN_KV_DIM), prev(vcol)),
            pl.BlockSpec((L, ATTN_KV_DIM), own(vcol)),
            pl.BlockSpec((1, N_Q_HEADS), lambda b, i: (0, 0)),
            pl.BlockSpec((2, N_Q_HEADS, 2 * L, L), lambda b, i: (0, 0, 0, 0), pipeline_mode=pl.Buffered(1)),
        ],
        out_specs=pl.BlockSpec((L, ATTN_Q_DIM), own(0)),
        scratch_shapes=[pltpu.VMEM((ATTN_Q_DIM, L), F32)],
        compiler_params=_params(("parallel", "parallel")),
        name="swa_core",
    )(proj, proj, proj, proj, proj, sinks.reshape(1, N_Q_HEADS), tab)


MOBA_SLOT = 2 * HEAD_DIM
PEN_LANES = 32
SLOPE_LANES = (HEAD_DIM + PEN_LANES, HEAD_DIM + PEN_LANES + 1)
STRIDE_LANES = (HEAD_DIM + PEN_LANES + 2, HEAD_DIM + PEN_LANES + 3)
MOBA_Q_WIDTH = N_Q_HEADS * MOBA_SLOT
MOBA_K_WIDTH = N_KV_HEADS * MOBA_SLOT
MOBA_WIDTH = MOBA_Q_WIDTH + MOBA_K_WIDTH + ATTN_KV_DIM
MASKED = -1e30
LOG2E = math.log2(math.e)
SUM_ROWS = 16
MOBA_KEY_GROUP = 16
MOBA_UNROLL = 4
NORM_MARGIN = 1.02
BOUND_LIMIT = 60.0

_KMEAN_BLOCKS = 8


def _kmean_kernel(k_ref, mean_ref, sqmax_ref):
    k = k_ref[...].reshape(_KMEAN_BLOCKS, MOBA_BLOCK, MOBA_K_WIDTH)
    mean_ref[...] = jnp.sum(k, axis=1) * (1.0 / MOBA_BLOCK)
    sqmax_ref[...] = jnp.max(k * k, axis=1)


def _moba_kmean(proj):
    n = proj.shape[0]
    rows = _KMEAN_BLOCKS * MOBA_BLOCK
    out = jax.ShapeDtypeStruct((n // MOBA_BLOCK, MOBA_K_WIDTH), F32)
    spec = pl.BlockSpec((_KMEAN_BLOCKS, MOBA_K_WIDTH), lambda i: (i, 0))
    return pl.pallas_call(
        _kmean_kernel,
        out_shape=(out, out),
        grid=(n // rows,),
        in_specs=[pl.BlockSpec((rows, MOBA_K_WIDTH), lambda i: (i, MOBA_Q_WIDTH // MOBA_K_WIDTH))],
        out_specs=(spec, spec),
        compiler_params=_params(("parallel",)),
        name="moba_kmean",
    )(proj)


def _moba_select_kernel(q_ref, km_ref, expand_ref, slope_ref, slot_ref, k_ref, v_ref,
                        qa_ref, ks_ref, vt_ref, qn_ref, *, nblk):
    n = pl.program_id(1)
    Lb = MOBA_BLOCK
    qt = (q_ref[...] * (HEAD_DIM ** -0.5)).T
    gate = _dot_split(km_ref[0], qt)
    blk = lax.broadcasted_iota(jnp.int32, (nblk, Lb), 0)
    row = blk.astype(F32)
    unselected = []
    for h in range(N_KV_HEADS):
        g = jnp.where(blk < n, gate[h * nblk:(h + 1) * nblk, :], NEG_INF)
        sel = jnp.zeros((nblk, Lb), F32)
        for _ in range(MOBA_TOP_K):
            mx = jnp.max(g, axis=0, keepdims=True)
            is_max = (g == mx) & (mx > NEG_INF)
            first = jnp.min(jnp.where(is_max, row, float(nblk)), axis=0, keepdims=True)
            pick = row == first
            sel = jnp.where(pick, 1.0, sel)
            g = jnp.where(pick, NEG_INF, g)
        unselected.append(1.0 - sel)
    unselected = jnp.concatenate(unselected, axis=0).astype(BF16)
    qs = qt * LOG2E
    norm2 = _dot(slot_ref[...], (qs * qs).astype(BF16))
    qn_ref[...] = jnp.sqrt(norm2[:N_Q_HEADS, :]) * NORM_MARGIN
    q_feat = qs + slope_ref[...]
    for var in range(qa_ref.shape[0]):
        penalty = _dot(expand_ref[var], unselected)
        qa_ref[var] = (q_feat + penalty).astype(BF16)
    ks_ref[0] = k_ref[...].astype(BF16)
    vt_ref[0] = v_ref[...].T.astype(BF16)


def _moba_select(proj, km_mat, expand, slope_feat, batch, seq):
    n = proj.shape[0]
    nblk = seq // MOBA_BLOCK
    width = N_KV_HEADS * nblk
    nvar = expand.shape[0]
    vcol = (MOBA_Q_WIDTH + MOBA_K_WIDTH) // ATTN_KV_DIM
    slot = jnp.arange(MOBA_Q_WIDTH) // MOBA_SLOT
    lane = jnp.arange(MOBA_Q_WIDTH) % MOBA_SLOT
    slot_sum = ((jnp.arange(LANES)[:, None] == slot[None, :]) & (lane[None, :] < HEAD_DIM)).astype(BF16)
    return pl.pallas_call(
        functools.partial(_moba_select_kernel, nblk=nblk),
        out_shape=(jax.ShapeDtypeStruct((nvar, MOBA_Q_WIDTH, n), BF16),
                   jax.ShapeDtypeStruct((n // MOBA_BLOCK, MOBA_BLOCK, MOBA_K_WIDTH), BF16),
                   jax.ShapeDtypeStruct((n // MOBA_BLOCK, ATTN_KV_DIM, MOBA_BLOCK), BF16),
                   jax.ShapeDtypeStruct((N_Q_HEADS, n), F32)),
        grid=(batch, nblk),
        in_specs=[
            pl.BlockSpec((MOBA_BLOCK, MOBA_Q_WIDTH), lambda b, i: (b * nblk + i, 0)),
            pl.BlockSpec((1, width, MOBA_Q_WIDTH), lambda b, i: (b, 0, 0)),
            pl.BlockSpec((nvar, MOBA_Q_WIDTH, width), lambda b, i: (0, 0, 0)),
            pl.BlockSpec((MOBA_Q_WIDTH, MOBA_BLOCK), lambda b, i: (0, 0)),
            pl.BlockSpec((LANES, MOBA_Q_WIDTH), lambda b, i: (0, 0)),
            pl.BlockSpec((MOBA_BLOCK, MOBA_K_WIDTH), lambda b, i: (b * nblk + i, MOBA_Q_WIDTH // MOBA_K_WIDTH)),
            pl.BlockSpec((MOBA_BLOCK, ATTN_KV_DIM), lambda b, i: (b * nblk + i, vcol)),
        ],
        out_specs=(pl.BlockSpec((nvar, MOBA_Q_WIDTH, MOBA_BLOCK), lambda b, i: (0, 0, b * nblk + i)),
                   pl.BlockSpec((1, MOBA_BLOCK, MOBA_K_WIDTH), lambda b, i: (b * nblk + i, 0, 0)),
                   pl.BlockSpec((1, ATTN_KV_DIM, MOBA_BLOCK), lambda b, i: (b * nblk + i, 0, 0)),
                   pl.BlockSpec((N_Q_HEADS, MOBA_BLOCK), lambda b, i: (0, b * nblk + i))),
        compiler_params=_params(("parallel", "parallel")),
        name="moba_select",
    )(proj, km_mat, expand, slope_feat, slot_sum, proj, proj)


def _moba_scores(ka, qa_ref, var, kv):
    heads = [kv * GQA_GROUP + g for g in range(GQA_GROUP)]
    qt = jnp.concatenate([qa_ref[var, hd * MOBA_SLOT:(hd + 1) * MOBA_SLOT, :] for hd in heads], axis=1)
    return _dot(ka, qt)


def _moba_blocks_update(blocks, qa_ref, var, m_ref, l_ref, acc_ref, causal=None, qn_ref=None):
    keys = lambda kv: jnp.concatenate([blk[0](kv) for blk in blocks], axis=0)
    s_next = _moba_scores(keys(0), qa_ref, var, 0)
    for kv in range(N_KV_HEADS):
        s_all = s_next
        if kv + 1 < N_KV_HEADS:
            s_next = _moba_scores(keys(kv + 1), qa_ref, var, kv + 1)
        vt = jnp.concatenate([blk[1](kv) for blk in blocks], axis=1)
        vt = jnp.concatenate([vt, jnp.ones((SUM_ROWS, vt.shape[1]), BF16)], axis=0)
        k_norm = None
        if qn_ref is not None:
            k_norm = functools.reduce(jnp.maximum, [blk[2](kv) for blk in blocks])
        _moba_softmax_pv(s_all, vt, m_ref, l_ref, acc_ref, kv, causal, qn_ref, k_norm)


def _moba_softmax_pv(s_all, vt, m_ref, l_ref, acc_ref, kv, causal, qn_ref, k_norm):
    Lb = MOBA_BLOCK
    heads = [kv * GQA_GROUP + g for g in range(GQA_GROUP)]
    probs, alphas = [], []
    for g, hd in enumerate(heads):
        s = s_all[:, g * Lb:(g + 1) * Lb]
        if causal is not None:
            s = jnp.where(causal, s, MASKED)
        m_old = m_ref[hd:hd + 1, :]
        if k_norm is None:
            m_new = jnp.maximum(m_old, jnp.max(s, axis=0, keepdims=True))
        else:
            qry = lax.broadcasted_iota(jnp.int32, (1, Lb), 1).astype(F32)
            alibi_top = (qry - (Lb - 1)) * (LOG2E * ALIBI_SLOPES[hd])
            m_new = jnp.maximum(m_old, qn_ref[hd:hd + 1, :] * k_norm + alibi_top)
        alphas.append(jnp.exp2(m_old - m_new))
        probs.append(jnp.exp2(s - m_new).astype(BF16))
        m_ref[hd:hd + 1, :] = m_new
    pv_all = _dot(vt, jnp.concatenate(probs, axis=1))
    for g, hd in enumerate(heads):
        pv = pv_all[:, g * Lb:(g + 1) * Lb]
        rows = slice(hd * HEAD_DIM, (hd + 1) * HEAD_DIM)
        l_ref[hd:hd + 1, :] = alphas[g] * l_ref[hd:hd + 1, :] + pv[HEAD_DIM:HEAD_DIM + 1, :]
        acc_ref[rows, :] = alphas[g] * acc_ref[rows, :] + pv[:HEAD_DIM, :]


def _moba_attn_kernel(nq_ref, jg_ref, bounded_ref, qa_ref, k_ref, vt_ref, qn_ref, kn_ref, o_ref,
                      m_ref, l_ref, acc_ref, *, group, nblk):
    Lb = MOBA_BLOCK
    step = pl.program_id(1)
    n = nq_ref[step]
    first = jg_ref[step] * group
    var = first // PEN_LANES
    bounded = bounded_ref[pl.program_id(0) * nblk + n]

    @pl.when(first == 0)
    def _():
        m_ref[...] = jnp.full(m_ref.shape, NEG_INF, F32)
        l_ref[...] = jnp.zeros(l_ref.shape, F32)
        acc_ref[...] = jnp.zeros(acc_ref.shape, F32)

    lane = lax.broadcasted_iota(jnp.int32, (Lb, MOBA_SLOT), 1)
    key_off = (lax.broadcasted_iota(jnp.int32, (Lb, MOBA_SLOT), 0) - (Lb - 1)).astype(F32)
    slope_lanes = (lane == SLOPE_LANES[0]) | (lane == SLOPE_LANES[1])
    stride_lanes = (lane == STRIDE_LANES[0]) | (lane == STRIDE_LANES[1])
    key_feat = jnp.where(slope_lanes, key_off, 0.0)

    def past_block(i):
        j = first + i
        extra = jnp.where(stride_lanes, (j - n).astype(F32), key_feat)
        extra = jnp.where(lane == HEAD_DIM + (j - var * PEN_LANES), 1.0, extra).astype(BF16)
        return (lambda kv: k_ref[i, :, kv * MOBA_SLOT:(kv + 1) * MOBA_SLOT] + extra,
                lambda kv: vt_ref[i, kv * HEAD_DIM:(kv + 1) * HEAD_DIM, :],
                lambda kv: kn_ref[0, pl.ds(i * N_KV_HEADS + kv, 1), :])

    def past_blocks(count, shift_ref):
        def body(it, carry):
            blocks = [past_block(carry + u) for u in range(count)]
            _moba_blocks_update(blocks, qa_ref, var, m_ref, l_ref, acc_ref, None, shift_ref)
            return carry + count
        return body

    n_past = jnp.minimum(n - first, group)
    n_multi = n_past // MOBA_UNROLL

    def run_past(shift_ref):
        done = lax.fori_loop(0, n_multi, past_blocks(MOBA_UNROLL, shift_ref), 0)
        lax.fori_loop(0, n_past - done, past_blocks(1, shift_ref), done)

    @pl.when(bounded == 1)
    def _():
        run_past(qn_ref)

    @pl.when(bounded != 1)
    def _():
        run_past(None)

    @pl.when(n - first < group)
    def _():
        i = n - first
        key = lax.broadcasted_iota(jnp.int32, (Lb, Lb), 0)
        qry = lax.broadcasted_iota(jnp.int32, (Lb, Lb), 1)
        causal = key <= qry
        extra = key_feat.astype(BF16)
        own = (lambda kv: k_ref[i, :, kv * MOBA_SLOT:(kv + 1) * MOBA_SLOT] + extra,
               lambda kv: vt_ref[i, kv * HEAD_DIM:(kv + 1) * HEAD_DIM, :])
        _moba_blocks_update([own], qa_ref, var, m_ref, l_ref, acc_ref, causal)
        inv = 1.0 / l_ref[...]
        acc = acc_ref[...].reshape(N_Q_HEADS, HEAD_DIM, Lb) * inv[:, None, :]
        o_ref[...] = acc.reshape(N_Q_HEADS * HEAD_DIM, Lb).T


def _moba_attn(qaug, kslot, vt, qnorm, knorm, batch, seq):
    n = batch * seq
    nblk = seq // MOBA_BLOCK
    group = min(MOBA_KEY_GROUP, nblk)
    assert nblk % group == 0
    pairs = [(i, jg) for i in range(nblk) for jg in range(i // group + 1)]
    nq = jnp.asarray([p[0] for p in pairs], jnp.int32)
    jk = jnp.asarray([p[1] for p in pairs], jnp.int32)
    ngrp = nblk // group
    nvar = qaug.shape[0]
    assert PEN_LANES % group == 0, "a key group must not straddle two query variants"
    q_top = jnp.max(qnorm.reshape(N_Q_HEADS, batch, nblk, MOBA_BLOCK), axis=(0, 3))
    bounded = (q_top * jnp.max(knorm, axis=(1, 2))[:, None] <= BOUND_LIMIT).astype(jnp.int32).reshape(-1)
    kn_rows = jnp.broadcast_to(knorm.reshape(batch * ngrp, group * N_KV_HEADS, 1),
                               (batch * ngrp, group * N_KV_HEADS, MOBA_BLOCK))
    qmap = lambda b, s, nq_r, jk_r, bd_r: (b * nblk + nq_r[s], 0)
    kmap = lambda b, s, nq_r, jk_r, bd_r: (b * ngrp + jk_r[s], 0, 0)
    grid_spec = pltpu.PrefetchScalarGridSpec(
        num_scalar_prefetch=3,
        grid=(batch, len(pairs)),
        in_specs=[
            pl.BlockSpec((nvar, MOBA_Q_WIDTH, MOBA_BLOCK), lambda b, s, nq_r, jk_r, bd_r: (0, 0, b * nblk + nq_r[s])),
            pl.BlockSpec((group, MOBA_BLOCK, MOBA_K_WIDTH), kmap),
            pl.BlockSpec((group, ATTN_KV_DIM, MOBA_BLOCK), kmap),
            pl.BlockSpec((N_Q_HEADS, MOBA_BLOCK), lambda b, s, nq_r, jk_r, bd_r: (0, b * nblk + nq_r[s])),
            pl.BlockSpec((1, group * N_KV_HEADS, MOBA_BLOCK), kmap),
        ],
        out_specs=pl.BlockSpec((MOBA_BLOCK, ATTN_Q_DIM), qmap),
        scratch_shapes=[
            pltpu.VMEM((N_Q_HEADS, MOBA_BLOCK), F32),
            pltpu.VMEM((N_Q_HEADS, MOBA_BLOCK), F32),
            pltpu.VMEM((N_Q_HEADS * HEAD_DIM, MOBA_BLOCK), F32),
        ],
    )
    return pl.pallas_call(
        functools.partial(_moba_attn_kernel, group=group, nblk=nblk),
        out_shape=jax.ShapeDtypeStruct((n, ATTN_Q_DIM), F32),
        grid_spec=grid_spec,
        compiler_params=_params(("parallel", "arbitrary")),
        name="moba_attn",
    )(nq, jk, bounded, qaug, kslot, vt, qnorm, kn_rows)


def _moba_slot_weights(w_in):
    d = w_in.shape[0]
    fill = MOBA_SLOT - HEAD_DIM
    wq = w_in[:, :ATTN_Q_DIM].reshape(d, N_Q_HEADS, HEAD_DIM)
    wk = w_in[:, ATTN_Q_DIM:ATTN_Q_DIM + ATTN_KV_DIM].reshape(d, N_KV_HEADS, HEAD_DIM)
    wq = jnp.pad(wq, ((0, 0), (0, 0), (0, fill))).reshape(d, MOBA_Q_WIDTH)
    wk = jnp.pad(wk, ((0, 0), (0, 0), (0, fill))).reshape(d, MOBA_K_WIDTH)
    return jnp.concatenate([wq, wk, w_in[:, ATTN_Q_DIM + ATTN_KV_DIM:]], axis=1)


def _moba_core(proj, batch, seq):
    nblk = seq // MOBA_BLOCK
    nvar = -(-nblk // PEN_LANES)
    width = N_KV_HEADS * nblk
    kmean, ksqmax = _moba_kmean(proj)
    knorm = jnp.sqrt(jnp.sum(ksqmax.reshape(batch, nblk, N_KV_HEADS, MOBA_SLOT), axis=-1))
    km = kmean.reshape(batch, nblk, N_KV_HEADS, MOBA_SLOT).transpose(0, 2, 3, 1)
    eye = jnp.eye(N_KV_HEADS, dtype=F32)
    km_mat = jnp.einsum('bhej,hH->bheHj', km, eye)
    km_mat = jnp.broadcast_to(km_mat[:, :, None], (batch, N_KV_HEADS, GQA_GROUP, MOBA_SLOT, N_KV_HEADS, nblk))
    km_mat = km_mat.reshape(batch, MOBA_Q_WIDTH, width).transpose(0, 2, 1)
    h_of = jnp.arange(width) // nblk
    c_of = jnp.arange(width) % nblk
    slot = jnp.arange(MOBA_Q_WIDTH) // MOBA_SLOT
    lane = jnp.arange(MOBA_Q_WIDTH) % MOBA_SLOT
    hit = (h_of[:, None] == (slot // GQA_GROUP)[None, :]) & (lane[None, :] == HEAD_DIM + (c_of % PEN_LANES)[:, None])
    in_var = (c_of // PEN_LANES)[None, :, None] == jnp.arange(nvar)[:, None, None]
    expand = jnp.where(hit[None] & in_var, MASKED, 0.0).astype(BF16).transpose(0, 2, 1)
    sigma = LOG2E * jnp.asarray(ALIBI_SLOPES, F32)
    slope_feat = jnp.zeros((MOBA_Q_WIDTH,), F32)
    for lanes, value in ((SLOPE_LANES, sigma), (STRIDE_LANES, sigma * MOBA_BLOCK)):
        hi = value.astype(BF16).astype(F32)
        lo = (value - hi).astype(BF16).astype(F32)
        slope_feat = slope_feat + jnp.where(lane == lanes[0], hi[slot], 0.0) + jnp.where(lane == lanes[1], lo[slot], 0.0)
    slope_feat = jnp.broadcast_to(slope_feat[:, None], (MOBA_Q_WIDTH, MOBA_BLOCK))
    qaug, kslot, vt, qnorm = _moba_select(proj, km_mat, expand, slope_feat, batch, seq)
    return _moba_attn(qaug, kslot, vt, qnorm, knorm, batch, seq)


LIN_CHUNKS = LIN_ROWS // LIN_CHUNK


def _lin_masks(key_dim):
    R = LIN_ROWS
    chunk_bits = int(math.log2(LIN_CHUNK))
    key_bits = int(math.log2(key_dim))
    assert (1 << chunk_bits) == LIN_CHUNK and (1 << key_bits) == key_dim
    r = lax.broadcasted_iota(jnp.int32, (R, R), 0)
    c = lax.broadcasted_iota(jnp.int32, (R, R), 1)
    same = jnp.right_shift(r, chunk_bits) == jnp.right_shift(c, chunk_bits)
    causal = same & (r >= c)
    sums = jnp.concatenate([jnp.where(causal, 1.0, 0.0), jnp.where(same, 1.0, 0.0)], axis=0).astype(BF16)
    rr = lax.broadcasted_iota(jnp.int32, (R, LIN_CHUNKS * key_dim), 0)
    cc = lax.broadcasted_iota(jnp.int32, (R, LIN_CHUNKS * key_dim), 1)
    own = jnp.right_shift(rr, chunk_bits) == jnp.right_shift(cc, key_bits)
    return sums, causal, own


def _lin_prefix(log_g, sums):
    hi = log_g.astype(BF16)
    r1 = log_g - hi.astype(F32)
    mid = r1.astype(BF16)
    lo = (r1 - mid.astype(F32)).astype(BF16)
    return _dot(sums, jnp.concatenate([hi, mid, lo], axis=1))


def _lin_heads(n_heads, head_inputs, emit, st_ref, masks):
    cur = head_inputs(0)
    pre = _lin_prefix(cur[3], masks[0])
    for h in range(n_heads):
        nxt = pre_nxt = None
        if h + 1 < n_heads:
            nxt = head_inputs(h + 1)
            pre_nxt = _lin_prefix(nxt[3], masks[0])
        emit(h, _lin_tile(cur[0], cur[1], cur[2], pre, st_ref, h, masks))
        cur, pre = nxt, pre_nxt


def _lin_tile(q, k, v, acc, st_ref, h, masks):
    _, causal, own = masks
    R, K = q.shape
    acc = acc[:, :K] + (acc[:, K:2 * K] + acc[:, 2 * K:])
    b, b_chunk = acc[:R], acc[R:]
    q_dec = (q * jnp.exp(b)).astype(BF16)
    k_dec = (k * jnp.exp(-b)).astype(BF16)
    k_tail = (k * jnp.exp(b_chunk - b)).astype(BF16)
    vb = v.astype(BF16)
    attn = jnp.where(causal, _dot_nt(q_dec, k_dec), 0.0).astype(BF16)
    o = _dot(attn, vb)
    tile_lanes = lambda x: jnp.concatenate([x] * LIN_CHUNKS, axis=1)
    zero = jnp.zeros((), BF16)
    ds = _dot_tn(vb, jnp.where(own, tile_lanes(k_tail), zero))
    st = st_ref[h]
    states = []
    for c in range(LIN_CHUNKS):
        states.append(st.astype(BF16))
        decay = jnp.exp(b_chunk[c * LIN_CHUNK:c * LIN_CHUNK + 1, :])
        st = st * decay + ds[:, c * K:(c + 1) * K]
    st_ref[h] = st
    o = o + _dot_nt(jnp.where(own, tile_lanes(q_dec), zero), jnp.concatenate(states, axis=1))
    return o


def _head_norm_gate(o, gain, g):
    o = o * lax.rsqrt(jnp.mean(o * o, axis=-1, keepdims=True) + RMS_EPS) * gain
    return o * _silu(g)


def _gla_kernel(q_ref, k_ref, v_ref, g_ref, a_ref, wd_ref, bd_ref, gain_ref, o_ref, st_ref):
    @pl.when(pl.program_id(1) == 0)
    def _():
        st_ref[...] = jnp.zeros(st_ref.shape, F32)

    K, V = GLA_KEY_DIM, GLA_VAL_DIM
    masks = _lin_masks(K)
    z = _dot_split(a_ref[...], wd_ref[...]) + bd_ref[...]
    log_alpha = (jnp.minimum(z, 0.0) - jnp.log(1.0 + jnp.exp(-jnp.abs(z)))) * (1.0 / GLA_GATE_TEMP)
    def head_inputs(h):
        return (q_ref[:, h * K:(h + 1) * K] * (K ** -0.5), k_ref[:, h * K:(h + 1) * K],
                v_ref[:, h * V:(h + 1) * V], log_alpha[:, h * K:(h + 1) * K])

    def emit(h, o):
        o_ref[:, h * V:(h + 1) * V] = _head_norm_gate(o, gain_ref[...], g_ref[:, h * V:(h + 1) * V])

    _lin_heads(GLA_HEADS, head_inputs, emit, st_ref, masks)


def _gla_core(proj, wd_pad, bd, gain, batch, seq):
    n = proj.shape[0]
    nt = seq // LIN_ROWS
    dk = GLA_HEADS * GLA_KEY_DIM
    dv = GLA_HEADS * GLA_VAL_DIM
    rows = lambda c: (lambda b, t: (b * nt + t, c))
    const = lambda b, t: (0, 0)
    return pl.pallas_call(
        _gla_kernel,
        out_shape=jax.ShapeDtypeStruct((n, dv), F32),
        grid=(batch, nt),
        in_specs=[
            pl.BlockSpec((LIN_ROWS, dk), rows(0)),
            pl.BlockSpec((LIN_ROWS, dk), rows(1)),
            pl.BlockSpec((LIN_ROWS, dv), rows(1)),
            pl.BlockSpec((LIN_ROWS, dv), rows(2)),
            pl.BlockSpec((LIN_ROWS, LANES), rows((2 * dk + 2 * dv) // LANES)),
            pl.BlockSpec((LANES, dk), const),
            pl.BlockSpec((1, dk), const),
            pl.BlockSpec((1, GLA_VAL_DIM), const),
        ],
        out_specs=pl.BlockSpec((LIN_ROWS, dv), rows(0)),
        scratch_shapes=[pltpu.VMEM((GLA_HEADS, GLA_VAL_DIM, GLA_KEY_DIM), F32)],
        compiler_params=_params(("parallel", "arbitrary")),
        name="gla_core",
    )(proj, proj, proj, proj, proj, wd_pad, bd.reshape(1, dk), gain.reshape(1, GLA_VAL_DIM))


def _hgrn_kernel(q_ref, f_ref, i_ref, g_ref, lbl_ref, gain_ref, o_ref, st_ref, *, layer):
    @pl.when(pl.program_id(1) == 0)
    def _():
        st_ref[...] = jnp.zeros(st_ref.shape, F32)

    K, V = HGRN_KEY_DIM, HGRN_VAL_DIM
    masks = _lin_masks(K)
    logits = lbl_ref[...]
    e = jnp.exp(logits - jnp.max(logits, axis=0, keepdims=True))
    p = e / jnp.sum(e, axis=0, keepdims=True)
    lb = jnp.zeros((1, logits.shape[1]), F32)
    for l in range(1, layer + 1):
        lb = lb + p[l:l + 1, :]

    def head_inputs(h):
        cols = slice(h * K, (h + 1) * K)
        lbh = lb[:, cols]
        f = lbh + (1.0 - lbh) * _sigmoid(f_ref[:, cols])
        return _silu(q_ref[:, cols]) * (K ** -0.5), 1.0 - f, i_ref[:, cols], jnp.log(f)

    def emit(h, o):
        cols = slice(h * K, (h + 1) * K)
        o_ref[:, cols] = _head_norm_gate(o, gain_ref[...], g_ref[:, cols])

    _lin_heads(HGRN_HEADS, head_inputs, emit, st_ref, masks)


def _hgrn_core(proj, lb_logits, gain, layer, batch, seq):
    n = proj.shape[0]
    nt = seq // LIN_ROWS
    d = D_MODEL
    rows = lambda c: (lambda b, t: (b * nt + t, c))
    const = lambda b, t: (0, 0)
    return pl.pallas_call(
        functools.partial(_hgrn_kernel, layer=layer),
        out_shape=jax.ShapeDtypeStruct((n, d), F32),
        grid=(batch, nt),
        in_specs=[
            pl.BlockSpec((LIN_ROWS, d), rows(0)),
            pl.BlockSpec((LIN_ROWS, d), rows(1)),
            pl.BlockSpec((LIN_ROWS, d), rows(2)),
            pl.BlockSpec((LIN_ROWS, d), rows(3)),
            pl.BlockSpec(lb_logits.shape, const),
            pl.BlockSpec((1, HGRN_VAL_DIM), const),
        ],
        out_specs=pl.BlockSpec((LIN_ROWS, d), rows(0)),
        scratch_shapes=[pltpu.VMEM((HGRN_HEADS, HGRN_VAL_DIM, HGRN_KEY_DIM), F32)],
        compiler_params=_params(("parallel", "arbitrary")),
        name="hgrn_core",
    )(proj, proj, proj, proj, lb_logits, gain.reshape(1, HGRN_VAL_DIM))


def kernel(x, norm_mix, norm_ffn, swa_w_in, swa_sinks, swa_w_out, moba_w_in, moba_w_out, gla_w_in, gla_w_decay_up, gla_b_decay, gla_out_norm, gla_w_out, hgrn_w_in, hgrn_lb_logits, hgrn_out_norm, hgrn_w_out, ffn_w_gate_up, ffn_w_down, final_norm):
    batch, seq, d = x.shape
    depth = norm_mix.shape[0]
    xf = x.reshape(batch * seq, d)
    for i in range(depth):
        kind, j = i % N_MIXERS, i // N_MIXERS
        if kind == 0:
            proj = _norm_proj(xf, norm_mix[i], swa_w_in[j].astype(BF16), 768)
            o = _swa_core(proj, swa_sinks[j], batch, seq)
            w_out = swa_w_out[j]
        elif kind == 1:
            proj = _norm_proj(xf, norm_mix[i], _moba_slot_weights(moba_w_in[j]).astype(BF16), MOBA_WIDTH // 2)
            o = _moba_core(proj, batch, seq)
            w_out = moba_w_out[j]
        elif kind == 2:
            pad = LANES - GLA_GATE_RANK
            w_in = jnp.pad(gla_w_in[j], ((0, 0), (0, pad))).astype(BF16)
            wd_pad = jnp.pad(gla_w_decay_up[j], ((0, pad), (0, 0)))
            proj = _norm_proj(xf, norm_mix[i], w_in, 640)
            o = _gla_core(proj, wd_pad, gla_b_decay[j], gla_out_norm[j], batch, seq)
            w_out = gla_w_out[j]
        else:
            proj = _norm_proj(xf, norm_mix[i], hgrn_w_in[j].astype(BF16), 1024)
            o = _hgrn_core(proj, hgrn_lb_logits, hgrn_out_norm[j], i, batch, seq)
            w_out = hgrn_w_out[j]
        last = i == depth - 1
        xf = _ffn(xf, o, w_out.astype(BF16), norm_ffn[i], ffn_w_gate_up[i].astype(BF16),
                  ffn_w_down[i].astype(BF16), final_norm, last)
    return xf.reshape(batch, seq, d)
```

```python
import functools
import math

import jax
import jax.numpy as jnp
from jax import lax
from jax.experimental import pallas as pl
from jax.experimental.pallas import tpu as pltpu

F32 = jnp.float32
BF16 = jnp.bfloat16

D_MODEL = 1024
HEAD_DIM = 64
N_Q_HEADS = 16
N_KV_HEADS = 4
GQA_GROUP = 4
ATTN_Q_DIM = 1024
ATTN_KV_DIM = 256
SWA_BLOCK = 128
SWA_WINDOW = 128
MOBA_BLOCK = 256
MOBA_TOP_K = 3
GLA_HEADS = 4
GLA_KEY_DIM = 128
GLA_VAL_DIM = 256
GLA_GATE_RANK = 16
GLA_GATE_TEMP = 16.0
HGRN_HEADS = 8
HGRN_KEY_DIM = 128
HGRN_VAL_DIM = 128
LIN_CHUNK = 64
D_FF = 2816
RMS_EPS = 1e-6
N_MIXERS = 4

LANES = 128
VMEM_LIMIT = 56 * 1024 * 1024
FFN_CHUNK = 256
ROW_TILE = 512
PROJ_ROWS = 1024
PROJ_COLS_MAX = 1408
LIN_ROWS = 256

ALIBI_SLOPES = tuple(2.0 ** (-8.0 * (i + 1) / N_Q_HEADS) for i in range(N_Q_HEADS))
NEG_INF = float("-inf")


def _params(sem):
    return pltpu.CompilerParams(dimension_semantics=sem, vmem_limit_bytes=VMEM_LIMIT)


def _dot(a, b):
    return jnp.dot(a, b, preferred_element_type=F32)


def _dot_nt(a, b):
    return lax.dot_general(a, b, (((1,), (1,)), ((), ())), preferred_element_type=F32)


def _dot_tn(a, b):
    return lax.dot_general(a, b, (((0,), (0,)), ((), ())), preferred_element_type=F32)


def _split(x):
    hi = x.astype(BF16)
    lo = (x - hi.astype(F32)).astype(BF16)
    return hi, lo


def _dot_split(a, b):
    ah, al = _split(a)
    bh, bl = _split(b)
    return _dot(ah, bh) + (_dot(ah, bl) + _dot(al, bh))


def _rms(x, gain):
    return x * lax.rsqrt(jnp.mean(x * x, axis=-1, keepdims=True) + RMS_EPS) * gain


def _sigmoid(x):
    return 1.0 / (1.0 + jnp.exp(-x))


def _silu(x):
    return x * _sigmoid(x)


def _norm_proj_kernel(x_ref, g_ref, w_ref, o_ref, h_ref):
    @pl.when(pl.program_id(1) == 0)
    def _():
        h_ref[...] = _rms(x_ref[...], g_ref[...]).astype(BF16)

    o_ref[...] = _dot(h_ref[...], w_ref[...])


def _norm_proj(x, gain, w):
    n, d = x.shape
    dout = w.shape[1]
    tn = next(c * LANES for c in range(PROJ_COLS_MAX // LANES, 0, -1) if (dout // LANES) % c == 0)
    assert dout % LANES == 0 and dout % tn == 0
    return pl.pallas_call(
        _norm_proj_kernel,
        out_shape=jax.ShapeDtypeStruct((n, dout), F32),
        grid=(n // PROJ_ROWS, dout // tn),
        in_specs=[
            pl.BlockSpec((PROJ_ROWS, d), lambda i, j: (i, 0)),
            pl.BlockSpec((1, d), lambda i, j: (0, 0)),
            pl.BlockSpec((d, tn), lambda i, j: (0, j)),
        ],
        out_specs=pl.BlockSpec((PROJ_ROWS, tn), lambda i, j: (i, j)),
        scratch_shapes=[pltpu.VMEM((PROJ_ROWS, d), BF16)],
        compiler_params=_params(("parallel", "arbitrary")),
        name="norm_proj",
    )(x, gain.reshape(1, d), w)


def _ffn_kernel(x_ref, o_ref, wo_ref, g_ref, wgu_ref, wd_ref, fg_ref, out_ref,
                x1_ref, h_ref, acc_ref, *, final_norm):
    x1 = x_ref[...] + _dot(o_ref[...].astype(BF16), wo_ref[...])
    x1_ref[...] = x1
    h_ref[...] = _rms(x1, g_ref[...]).astype(BF16)
    for c in range(D_FF // FFN_CHUNK):
        h = h_ref[...]
        gate = _dot(h, wgu_ref[:, c * FFN_CHUNK:(c + 1) * FFN_CHUNK])
        up = _dot(h, wgu_ref[:, D_FF + c * FFN_CHUNK:D_FF + (c + 1) * FFN_CHUNK])
        act = (_silu(gate) * up).astype(BF16)
        part = _dot(act, wd_ref[c * FFN_CHUNK:(c + 1) * FFN_CHUNK, :])
        if c == 0:
            acc_ref[...] = part
        else:
            acc_ref[...] += part
    y = x1_ref[...] + acc_ref[...]
    if final_norm:
        y = _rms(y, fg_ref[...])
    out_ref[...] = y


def _ffn(x, o, wo, gain, wgu, wd, final_gain, final_norm):
    n, d = x.shape
    const = lambda i: (0, 0)
    row = lambda i: (i, 0)
    return pl.pallas_call(
        functools.partial(_ffn_kernel, final_norm=final_norm),
        out_shape=jax.ShapeDtypeStruct((n, d), F32),
        grid=(n // ROW_TILE,),
        in_specs=[
            pl.BlockSpec((ROW_TILE, d), row),
            pl.BlockSpec((ROW_TILE, d), row),
            pl.BlockSpec((d, d), const, pipeline_mode=pl.Buffered(1)),
            pl.BlockSpec((1, d), const),
            pl.BlockSpec((d, 2 * D_FF), const, pipeline_mode=pl.Buffered(1)),
            pl.BlockSpec((D_FF, d), const, pipeline_mode=pl.Buffered(1)),
            pl.BlockSpec((1, d), const),
        ],
        out_specs=pl.BlockSpec((ROW_TILE, d), row),
        scratch_shapes=[
            pltpu.VMEM((ROW_TILE, d), F32),
            pltpu.VMEM((ROW_TILE, d), BF16),
            pltpu.VMEM((ROW_TILE, d), F32),
        ],
        compiler_params=_params(("parallel",)),
        name="outproj_ffn",
    )(x, o, wo, gain.reshape(1, d), wgu, wd, final_gain.reshape(1, d))


def _swa_kernel(q_ref, kp_ref, ko_ref, vp_ref, vo_ref, sink_ref, tab_ref, o_ref, ot_ref):
    L = SWA_BLOCK
    n = pl.program_id(1)
    first = jnp.where(n == 0, 1, 0)
    qt = (q_ref[...] * (LOG2E * HEAD_DIM ** -0.5)).T.astype(BF16)
    k = jnp.concatenate([kp_ref[...], ko_ref[...]], axis=0).astype(BF16)
    vt = jnp.concatenate([vp_ref[...], vo_ref[...]], axis=0).T.astype(BF16)
    ones = jnp.ones((SUM_ROWS, 2 * L), BF16)
    sinks = sink_ref[...] * LOG2E
    def scores(kv):
        rows = [qt[hd * HEAD_DIM:(hd + 1) * HEAD_DIM, :] for hd in range(kv * GQA_GROUP, (kv + 1) * GQA_GROUP)]
        return _dot(k[:, kv * HEAD_DIM:(kv + 1) * HEAD_DIM], jnp.concatenate(rows, axis=1))

    s_next = scores(0)
    for kv in range(N_KV_HEADS):
        heads = [kv * GQA_GROUP + g for g in range(GQA_GROUP)]
        s_all = s_next
        if kv + 1 < N_KV_HEADS:
            s_next = scores(kv + 1)
        probs, extras = [], []
        for g, hd in enumerate(heads):
            s = s_all[:, g * L:(g + 1) * L] + tab_ref[first, hd]
            sink = sinks[:, hd:hd + 1]
            m = jnp.maximum(jnp.max(s, axis=0, keepdims=True), sink)
            probs.append(jnp.exp2(s - m).astype(BF16))
            extras.append(jnp.exp2(sink - m))
        vaug = jnp.concatenate([vt[kv * HEAD_DIM:(kv + 1) * HEAD_DIM, :], ones], axis=0)
        pv_all = _dot(vaug, jnp.concatenate(probs, axis=1))
        for g, hd in enumerate(heads):
            pv = pv_all[:, g * L:(g + 1) * L]
            denom = pv[HEAD_DIM:HEAD_DIM + 1, :] + extras[g]
            ot_ref[hd * HEAD_DIM:(hd + 1) * HEAD_DIM, :] = pv[:HEAD_DIM, :] / denom
    o_ref[...] = ot_ref[...].T


def _swa_core(proj, sinks, batch, seq):
    n = proj.shape[0]
    L = SWA_BLOCK
    nb = seq // L
    kcol = ATTN_Q_DIM // ATTN_KV_DIM
    vcol = kcol + 1
    own = lambda c: (lambda b, i: (b * nb + i, c))
    prev = lambda c: (lambda b, i: (b * nb + jnp.maximum(i - 1, 0), c))
    key = jnp.arange(2 * L)[:, None]
    qry = jnp.arange(L)[None, :]
    dist = (L + qry) - key
    band = (dist >= 0) & (dist < SWA_WINDOW)
    slopes = (LOG2E * jnp.asarray(ALIBI_SLOPES, F32))[:, None, None]
    bias = -slopes * dist.astype(F32)[None]
    tab = jnp.stack([jnp.where(band[None], bias, MASKED),
                     jnp.where((band & (key >= L))[None], bias, MASKED)])
    return pl.pallas_call(
        _swa_kernel,
        out_shape=jax.ShapeDtypeStruct((n, ATTN_Q_DIM), F32),
        grid=(batch, nb),
        in_specs=[
            pl.BlockSpec((L, ATTN_Q_DIM), own(0)),
            pl.BlockSpec((L, ATTN_KV_DIM), prev(kcol)),
            pl.BlockSpec((L, ATTN_KV_DIM), own(kcol)),
            pl.BlockSpec((L, ATTN_KV_DIM), prev(vcol)),
            pl.BlockSpec((L, ATTN_KV_DIM), own(vcol)),
            pl.BlockSpec((1, N_Q_HEADS), lambda b, i: (0, 0)),
            pl.BlockSpec((2, N_Q_HEADS, 2 * L, L), lambda b, i: (0, 0, 0, 0), pipeline_mode=pl.Buffered(1)),
        ],
        out_specs=pl.BlockSpec((L, ATTN_Q_DIM), own(0)),
        scratch_shapes=[pltpu.VMEM((ATTN_Q_DIM, L), F32)],
        compiler_params=_params(("parallel", "parallel")),
        name="swa_core",
    )(proj, proj, proj, proj, proj, sinks.reshape(1, N_Q_HEADS), tab)


MOBA_SLOT = 2 * HEAD_DIM
PEN_LANES = 32
SLOPE_LANES = (HEAD_DIM + PEN_LANES, HEAD_DIM + PEN_LANES + 1)
STRIDE_LANES = (HEAD_DIM + PEN_LANES + 2, HEAD_DIM + PEN_LANES + 3)
MOBA_Q_WIDTH = N_Q_HEADS * MOBA_SLOT
MOBA_K_WIDTH = N_KV_HEADS * MOBA_SLOT
MOBA_WIDTH = MOBA_Q_WIDTH + MOBA_K_WIDTH + ATTN_KV_DIM
MASKED = -1e30
LOG2E = math.log2(math.e)
SUM_ROWS = 16
MOBA_KEY_GROUP = 16
MOBA_UNROLL = 4
NORM_MARGIN = 1.02
BOUND_LIMIT = 60.0

_KMEAN_BLOCKS = 8


def _kmean_kernel(k_ref, mean_ref, sqmax_ref):
    k = k_ref[...].reshape(_KMEAN_BLOCKS, MOBA_BLOCK, MOBA_K_WIDTH)
    mean_ref[...] = jnp.sum(k, axis=1) * (1.0 / MOBA_BLOCK)
    sqmax_ref[...] = jnp.max(k * k, axis=1)


def _moba_kmean(proj):
    n = proj.shape[0]
    rows = _KMEAN_BLOCKS * MOBA_BLOCK
    out = jax.ShapeDtypeStruct((n // MOBA_BLOCK, MOBA_K_WIDTH), F32)
    spec = pl.BlockSpec((_KMEAN_BLOCKS, MOBA_K_WIDTH), lambda i: (i, 0))
    return pl.pallas_call(
        _kmean_kernel,
        out_shape=(out, out),
        grid=(n // rows,),
        in_specs=[pl.BlockSpec((rows, MOBA_K_WIDTH), lambda i: (i, MOBA_Q_WIDTH // MOBA_K_WIDTH))],
        out_specs=(spec, spec),
        compiler_params=_params(("parallel",)),
        name="moba_kmean",
    )(proj)


def _moba_select_kernel(q_ref, km_ref, expand_ref, slope_ref, slot_ref, k_ref, v_ref,
                        qa_ref, ks_ref, vt_ref, qn_ref, *, nblk):
    n = pl.program_id(1)
    Lb = MOBA_BLOCK
    qt = (q_ref[...] * (HEAD_DIM ** -0.5)).T
    gate = _dot_split(km_ref[0], qt)
    blk = lax.broadcasted_iota(jnp.int32, (nblk, Lb), 0)
    row = blk.astype(F32)
    unselected = []
    for h in range(N_KV_HEADS):
        g = jnp.where(blk < n, gate[h * nblk:(h + 1) * nblk, :], NEG_INF)
        sel = jnp.zeros((nblk, Lb), F32)
        for _ in range(MOBA_TOP_K):
            mx = jnp.max(g, axis=0, keepdims=True)
            is_max = (g == mx) & (mx > NEG_INF)
            first = jnp.min(jnp.where(is_max, row, float(nblk)), axis=0, keepdims=True)
            pick = row == first
            sel = jnp.where(pick, 1.0, sel)
            g = jnp.where(pick, NEG_INF, g)
        unselected.append(1.0 - sel)
    unselected = jnp.concatenate(unselected, axis=0).astype(BF16)
    qs = qt * LOG2E
    norm2 = _dot(slot_ref[...], (qs * qs).astype(BF16))
    qn_ref[...] = jnp.sqrt(norm2[:N_Q_HEADS, :]) * NORM_MARGIN
    q_feat = qs + slope_ref[...]
    for var in range(qa_ref.shape[0]):
        penalty = _dot(expand_ref[var], unselected)
        qa_ref[var] = (q_feat + penalty).astype(BF16)
    ks_ref[0] = k_ref[...].astype(BF16)
    vt_ref[0] = v_ref[...].T.astype(BF16)


def _moba_select(proj, km_mat, expand, slope_feat, batch, seq):
    n = proj.shape[0]
    nblk = seq // MOBA_BLOCK
    width = N_KV_HEADS * nblk
    nvar = expand.shape[0]
    vcol = (MOBA_Q_WIDTH + MOBA_K_WIDTH) // ATTN_KV_DIM
    slot = jnp.arange(MOBA_Q_WIDTH) // MOBA_SLOT
    lane = jnp.arange(MOBA_Q_WIDTH) % MOBA_SLOT
    slot_sum = ((jnp.arange(LANES)[:, None] == slot[None, :]) & (lane[None, :] < HEAD_DIM)).astype(BF16)
    return pl.pallas_call(
        functools.partial(_moba_select_kernel, nblk=nblk),
        out_shape=(jax.ShapeDtypeStruct((nvar, MOBA_Q_WIDTH, n), BF16),
                   jax.ShapeDtypeStruct((n // MOBA_BLOCK, MOBA_BLOCK, MOBA_K_WIDTH), BF16),
                   jax.ShapeDtypeStruct((n // MOBA_BLOCK, ATTN_KV_DIM, MOBA_BLOCK), BF16),
                   jax.ShapeDtypeStruct((N_Q_HEADS, n), F32)),
        grid=(batch, nblk),
        in_specs=[
            pl.BlockSpec((MOBA_BLOCK, MOBA_Q_WIDTH), lambda b, i: (b * nblk + i, 0)),
            pl.BlockSpec((1, width, MOBA_Q_WIDTH), lambda b, i: (b, 0, 0)),
            pl.BlockSpec((nvar, MOBA_Q_WIDTH, width), lambda b, i: (0, 0, 0)),
            pl.BlockSpec((MOBA_Q_WIDTH, MOBA_BLOCK), lambda b, i: (0, 0)),
            pl.BlockSpec((LANES, MOBA_Q_WIDTH), lambda b, i: (0, 0)),
            pl.BlockSpec((MOBA_BLOCK, MOBA_K_WIDTH), lambda b, i: (b * nblk + i, MOBA_Q_WIDTH // MOBA_K_WIDTH)),
            pl.BlockSpec((MOBA_BLOCK, ATTN_KV_DIM), lambda b, i: (b * nblk + i, vcol)),
        ],
        out_specs=(pl.BlockSpec((nvar, MOBA_Q_WIDTH, MOBA_BLOCK), lambda b, i: (0, 0, b * nblk + i)),
                   pl.BlockSpec((1, MOBA_BLOCK, MOBA_K_WIDTH), lambda b, i: (b * nblk + i, 0, 0)),
                   pl.BlockSpec((1, ATTN_KV_DIM, MOBA_BLOCK), lambda b, i: (b * nblk + i, 0, 0)),
                   pl.BlockSpec((N_Q_HEADS, MOBA_BLOCK), lambda b, i: (0, b * nblk + i))),
        compiler_params=_params(("parallel", "parallel")),
        name="moba_select",
    )(proj, km_mat, expand, slope_feat, slot_sum, proj, proj)


def _moba_scores(ka, qa_ref, var, kv):
    heads = [kv * GQA_GROUP + g for g in range(GQA_GROUP)]
    qt = jnp.concatenate([qa_ref[var, hd * MOBA_SLOT:(hd + 1) * MOBA_SLOT, :] for hd in heads], axis=1)
    return _dot(ka, qt)


def _moba_blocks_update(blocks, qa_ref, var, m_ref, l_ref, acc_ref, causal=None, qn_ref=None):
    keys = lambda kv: jnp.concatenate([blk[0](kv) for blk in blocks], axis=0)
    s_next = _moba_scores(keys(0), qa_ref, var, 0)
    for kv in range(N_KV_HEADS):
        s_all = s_next
        if kv + 1 < N_KV_HEADS:
            s_next = _moba_scores(keys(kv + 1), qa_ref, var, kv + 1)
        vt = jnp.concatenate([blk[1](kv) for blk in blocks], axis=1)
        vt = jnp.concatenate([vt, jnp.ones((SUM_ROWS, vt.shape[1]), BF16)], axis=0)
        k_norm = None
        if qn_ref is not None:
            k_norm = functools.reduce(jnp.maximum, [blk[2](kv) for blk in blocks])
        _moba_softmax_pv(s_all, vt, m_ref, l_ref, acc_ref, kv, causal, qn_ref, k_norm)


def _moba_softmax_pv(s_all, vt, m_ref, l_ref, acc_ref, kv, causal, qn_ref, k_norm):
    Lb = MOBA_BLOCK
    heads = [kv * GQA_GROUP + g for g in range(GQA_GROUP)]
    probs, alphas = [], []
    for g, hd in enumerate(heads):
        s = s_all[:, g * Lb:(g + 1) * Lb]
        if causal is not None:
            s = jnp.where(causal, s, MASKED)
        m_old = m_ref[hd:hd + 1, :]
        if k_norm is None:
            m_new = jnp.maximum(m_old, jnp.max(s, axis=0, keepdims=True))
        else:
            qry = lax.broadcasted_iota(jnp.int32, (1, Lb), 1).astype(F32)
            alibi_top = (qry - (Lb - 1)) * (LOG2E * ALIBI_SLOPES[hd])
            m_new = jnp.maximum(m_old, qn_ref[hd:hd + 1, :] * k_norm + alibi_top)
        alphas.append(jnp.exp2(m_old - m_new))
        probs.append(jnp.exp2(s - m_new).astype(BF16))
        m_ref[hd:hd + 1, :] = m_new
    pv_all = _dot(vt, jnp.concatenate(probs, axis=1))
    for g, hd in enumerate(heads):
        pv = pv_all[:, g * Lb:(g + 1) * Lb]
        rows = slice(hd * HEAD_DIM, (hd + 1) * HEAD_DIM)
        l_ref[hd:hd + 1, :] = alphas[g] * l_ref[hd:hd + 1, :] + pv[HEAD_DIM:HEAD_DIM + 1, :]
        acc_ref[rows, :] = alphas[g] * acc_ref[rows, :] + pv[:HEAD_DIM, :]


def _moba_attn_kernel(nq_ref, jg_ref, bounded_ref, qa_ref, k_ref, vt_ref, qn_ref, kn_ref, o_ref,
                      m_ref, l_ref, acc_ref, *, group, nblk):
    Lb = MOBA_BLOCK
    step = pl.program_id(1)
    n = nq_ref[step]
    first = jg_ref[step] * group
    var = first // PEN_LANES
    bounded = bounded_ref[pl.program_id(0) * nblk + n]

    @pl.when(first == 0)
    def _():
        m_ref[...] = jnp.full(m_ref.shape, NEG_INF, F32)
        l_ref[...] = jnp.zeros(l_ref.shape, F32)
        acc_ref[...] = jnp.zeros(acc_ref.shape, F32)

    lane = lax.broadcasted_iota(jnp.int32, (Lb, MOBA_SLOT), 1)
    key_off = (lax.broadcasted_iota(jnp.int32, (Lb, MOBA_SLOT), 0) - (Lb - 1)).astype(F32)
    slope_lanes = (lane == SLOPE_LANES[0]) | (lane == SLOPE_LANES[1])
    stride_lanes = (lane == STRIDE_LANES[0]) | (lane == STRIDE_LANES[1])
    key_feat = jnp.where(slope_lanes, key_off, 0.0)

    def past_block(i):
        j = first + i
        extra = jnp.where(stride_lanes, (j - n).astype(F32), key_feat)
        extra = jnp.where(lane == HEAD_DIM + (j - var * PEN_LANES), 1.0, extra).astype(BF16)
        return (lambda kv: k_ref[i, :, kv * MOBA_SLOT:(kv + 1) * MOBA_SLOT] + extra,
                lambda kv: vt_ref[i, kv * HEAD_DIM:(kv + 1) * HEAD_DIM, :],
                lambda kv: kn_ref[0, pl.ds(i * N_KV_HEADS + kv, 1), :])

    def past_blocks(count, shift_ref):
        def body(it, carry):
            blocks = [past_block(carry + u) for u in range(count)]
            _moba_blocks_update(blocks, qa_ref, var, m_ref, l_ref, acc_ref, None, shift_ref)
            return carry + count
        return body

    n_past = jnp.minimum(n - first, group)
    n_multi = n_past // MOBA_UNROLL

    def run_past(shift_ref):
        done = lax.fori_loop(0, n_multi, past_blocks(MOBA_UNROLL, shift_ref), 0)
        lax.fori_loop(0, n_past - done, past_blocks(1, shift_ref), done)

    @pl.when(bounded == 1)
    def _():
        run_past(qn_ref)

    @pl.when(bounded != 1)
    def _():
        run_past(None)

    @pl.when(n - first < group)
    def _():
        i = n - first
        key = lax.broadcasted_iota(jnp.int32, (Lb, Lb), 0)
        qry = lax.broadcasted_iota(jnp.int32, (Lb, Lb), 1)
        causal = key <= qry
        extra = key_feat.astype(BF16)
        own = (lambda kv: k_ref[i, :, kv * MOBA_SLOT:(kv + 1) * MOBA_SLOT] + extra,
               lambda kv: vt_ref[i, kv * HEAD_DIM:(kv + 1) * HEAD_DIM, :])
        _moba_blocks_update([own], qa_ref, var, m_ref, l_ref, acc_ref, causal)
        inv = 1.0 / l_ref[...]
        acc = acc_ref[...].reshape(N_Q_HEADS, HEAD_DIM, Lb) * inv[:, None, :]
        o_ref[...] = acc.reshape(N_Q_HEADS * HEAD_DIM, Lb).T


def _moba_attn(qaug, kslot, vt, qnorm, knorm, batch, seq):
    n = batch * seq
    nblk = seq // MOBA_BLOCK
    group = min(MOBA_KEY_GROUP, nblk)
    assert nblk % group == 0
    pairs = [(i, jg) for i in range(nblk) for jg in range(i // group + 1)]
    nq = jnp.asarray([p[0] for p in pairs], jnp.int32)
    jk = jnp.asarray([p[1] for p in pairs], jnp.int32)
    ngrp = nblk // group
    nvar = qaug.shape[0]
    assert PEN_LANES % group == 0, "a key group must not straddle two query variants"
    q_top = jnp.max(qnorm.reshape(N_Q_HEADS, batch, nblk, MOBA_BLOCK), axis=(0, 3))
    bounded = (q_top * jnp.max(knorm, axis=(1, 2))[:, None] <= BOUND_LIMIT).astype(jnp.int32).reshape(-1)
    kn_rows = jnp.broadcast_to(knorm.reshape(batch * ngrp, group * N_KV_HEADS, 1),
                               (batch * ngrp, group * N_KV_HEADS, MOBA_BLOCK))
    qmap = lambda b, s, nq_r, jk_r, bd_r: (b * nblk + nq_r[s], 0)
    kmap = lambda b, s, nq_r, jk_r, bd_r: (b * ngrp + jk_r[s], 0, 0)
    grid_spec = pltpu.PrefetchScalarGridSpec(
        num_scalar_prefetch=3,
        grid=(batch, len(pairs)),
        in_specs=[
            pl.BlockSpec((nvar, MOBA_Q_WIDTH, MOBA_BLOCK), lambda b, s, nq_r, jk_r, bd_r: (0, 0, b * nblk + nq_r[s])),
            pl.BlockSpec((group, MOBA_BLOCK, MOBA_K_WIDTH), kmap),
            pl.BlockSpec((group, ATTN_KV_DIM, MOBA_BLOCK), kmap),
            pl.BlockSpec((N_Q_HEADS, MOBA_BLOCK), lambda b, s, nq_r, jk_r, bd_r: (0, b * nblk + nq_r[s])),
            pl.BlockSpec((1, group * N_KV_HEADS, MOBA_BLOCK), kmap),
        ],
        out_specs=pl.BlockSpec((MOBA_BLOCK, ATTN_Q_DIM), qmap),
        scratch_shapes=[
            pltpu.VMEM((N_Q_HEADS, MOBA_BLOCK), F32),
            pltpu.VMEM((N_Q_HEADS, MOBA_BLOCK), F32),
            pltpu.VMEM((N_Q_HEADS * HEAD_DIM, MOBA_BLOCK), F32),
        ],
    )
    return pl.pallas_call(
        functools.partial(_moba_attn_kernel, group=group, nblk=nblk),
        out_shape=jax.ShapeDtypeStruct((n, ATTN_Q_DIM), F32),
        grid_spec=grid_spec,
        compiler_params=_params(("parallel", "arbitrary")),
        name="moba_attn",
    )(nq, jk, bounded, qaug, kslot, vt, qnorm, kn_rows)


def _moba_slot_weights(w_in):
    d = w_in.shape[0]
    fill = MOBA_SLOT - HEAD_DIM
    wq = w_in[:, :ATTN_Q_DIM].reshape(d, N_Q_HEADS, HEAD_DIM)
    wk = w_in[:, ATTN_Q_DIM:ATTN_Q_DIM + ATTN_KV_DIM].reshape(d, N_KV_HEADS, HEAD_DIM)
    wq = jnp.pad(wq, ((0, 0), (0, 0), (0, fill))).reshape(d, MOBA_Q_WIDTH)
    wk = jnp.pad(wk, ((0, 0), (0, 0), (0, fill))).reshape(d, MOBA_K_WIDTH)
    return jnp.concatenate([wq, wk, w_in[:, ATTN_Q_DIM + ATTN_KV_DIM:]], axis=1)


def _moba_core(proj, batch, seq):
    nblk = seq // MOBA_BLOCK
    nvar = -(-nblk // PEN_LANES)
    width = N_KV_HEADS * nblk
    kmean, ksqmax = _moba_kmean(proj)
    knorm = jnp.sqrt(jnp.sum(ksqmax.reshape(batch, nblk, N_KV_HEADS, MOBA_SLOT), axis=-1))
    km = kmean.reshape(batch, nblk, N_KV_HEADS, MOBA_SLOT).transpose(0, 2, 3, 1)
    eye = jnp.eye(N_KV_HEADS, dtype=F32)
    km_mat = jnp.einsum('bhej,hH->bheHj', km, eye)
    km_mat = jnp.broadcast_to(km_mat[:, :, None], (batch, N_KV_HEADS, GQA_GROUP, MOBA_SLOT, N_KV_HEADS, nblk))
    km_mat = km_mat.reshape(batch, MOBA_Q_WIDTH, width).transpose(0, 2, 1)
    h_of = jnp.arange(width) // nblk
    c_of = jnp.arange(width) % nblk
    slot = jnp.arange(MOBA_Q_WIDTH) // MOBA_SLOT
    lane = jnp.arange(MOBA_Q_WIDTH) % MOBA_SLOT
    hit = (h_of[:, None] == (slot // GQA_GROUP)[None, :]) & (lane[None, :] == HEAD_DIM + (c_of % PEN_LANES)[:, None])
    in_var = (c_of // PEN_LANES)[None, :, None] == jnp.arange(nvar)[:, None, None]
    expand = jnp.where(hit[None] & in_var, MASKED, 0.0).astype(BF16).transpose(0, 2, 1)
    sigma = LOG2E * jnp.asarray(ALIBI_SLOPES, F32)
    slope_feat = jnp.zeros((MOBA_Q_WIDTH,), F32)
    for lanes, value in ((SLOPE_LANES, sigma), (STRIDE_LANES, sigma * MOBA_BLOCK)):
        hi = value.astype(BF16).astype(F32)
        lo = (value - hi).astype(BF16).astype(F32)
        slope_feat = slope_feat + jnp.where(lane == lanes[0], hi[slot], 0.0) + jnp.where(lane == lanes[1], lo[slot], 0.0)
    slope_feat = jnp.broadcast_to(slope_feat[:, None], (MOBA_Q_WIDTH, MOBA_BLOCK))
    qaug, kslot, vt, qnorm = _moba_select(proj, km_mat, expand, slope_feat, batch, seq)
    return _moba_attn(qaug, kslot, vt, qnorm, knorm, batch, seq)


LIN_CHUNKS = LIN_ROWS // LIN_CHUNK


def _lin_masks(key_dim):
    R = LIN_ROWS
    chunk_bits = int(math.log2(LIN_CHUNK))
    key_bits = int(math.log2(key_dim))
    assert (1 << chunk_bits) == LIN_CHUNK and (1 << key_bits) == key_dim
    r = lax.broadcasted_iota(jnp.int32, (R, R), 0)
    c = lax.broadcasted_iota(jnp.int32, (R, R), 1)
    same = jnp.right_shift(r, chunk_bits) == jnp.right_shift(c, chunk_bits)
    causal = same & (r >= c)
    sums = jnp.concatenate([jnp.where(causal, 1.0, 0.0), jnp.where(same, 1.0, 0.0)], axis=0).astype(BF16)
    rr = lax.broadcasted_iota(jnp.int32, (R, LIN_CHUNKS * key_dim), 0)
    cc = lax.broadcasted_iota(jnp.int32, (R, LIN_CHUNKS * key_dim), 1)
    own = jnp.right_shift(rr, chunk_bits) == jnp.right_shift(cc, key_bits)
    return sums, causal, own


def _lin_prefix(log_g, sums):
    hi = log_g.astype(BF16)
    r1 = log_g - hi.astype(F32)
    mid = r1.astype(BF16)
    lo = (r1 - mid.astype(F32)).astype(BF16)
    return _dot(sums, jnp.concatenate([hi, mid, lo], axis=1))


def _lin_heads(n_heads, head_inputs, emit, st_ref, masks):
    cur = head_inputs(0)
    pre = _lin_prefix(cur[3], masks[0])
    for h in range(n_heads):
        nxt = pre_nxt = None
        if h + 1 < n_heads:
            nxt = head_inputs(h + 1)
            pre_nxt = _lin_prefix(nxt[3], masks[0])
        emit(h, _lin_tile(cur[0], cur[1], cur[2], pre, st_ref, h, masks))
        cur, pre = nxt, pre_nxt


def _lin_tile(q, k, v, acc, st_ref, h, masks):
    _, causal, own = masks
    R, K = q.shape
    acc = acc[:, :K] + (acc[:, K:2 * K] + acc[:, 2 * K:])
    b, b_chunk = acc[:R], acc[R:]
    q_dec = (q * jnp.exp(b)).astype(BF16)
    k_dec = (k * jnp.exp(-b)).astype(BF16)
    k_tail = (k * jnp.exp(b_chunk - b)).astype(BF16)
    vb = v.astype(BF16)
    attn = jnp.where(causal, _dot_nt(q_dec, k_dec), 0.0).astype(BF16)
    o = _dot(attn, vb)
    tile_lanes = lambda x: jnp.concatenate([x] * LIN_CHUNKS, axis=1)
    zero = jnp.zeros((), BF16)
    ds = _dot_tn(vb, jnp.where(own, tile_lanes(k_tail), zero))
    st = st_ref[h]
    states = []
    for c in range(LIN_CHUNKS):
        states.append(st.astype(BF16))
        decay = jnp.exp(b_chunk[c * LIN_CHUNK:c * LIN_CHUNK + 1, :])
        st = st * decay + ds[:, c * K:(c + 1) * K]
    st_ref[h] = st
    o = o + _dot_nt(jnp.where(own, tile_lanes(q_dec), zero), jnp.concatenate(states, axis=1))
    return o


def _head_norm_gate(o, gain, g):
    o = o * lax.rsqrt(jnp.mean(o * o, axis=-1, keepdims=True) + RMS_EPS) * gain
    return o * _silu(g)


def _gla_kernel(q_ref, k_ref, v_ref, g_ref, a_ref, wd_ref, bd_ref, gain_ref, o_ref, st_ref):
    @pl.when(pl.program_id(1) == 0)
    def _():
        st_ref[...] = jnp.zeros(st_ref.shape, F32)

    K, V = GLA_KEY_DIM, GLA_VAL_DIM
    masks = _lin_masks(K)
    z = _dot_split(a_ref[...], wd_ref[...]) + bd_ref[...]
    log_alpha = (jnp.minimum(z, 0.0) - jnp.log(1.0 + jnp.exp(-jnp.abs(z)))) * (1.0 / GLA_GATE_TEMP)
    def head_inputs(h):
        return (q_ref[:, h * K:(h + 1) * K] * (K ** -0.5), k_ref[:, h * K:(h + 1) * K],
                v_ref[:, h * V:(h + 1) * V], log_alpha[:, h * K:(h + 1) * K])

    def emit(h, o):
        o_ref[:, h * V:(h + 1) * V] = _head_norm_gate(o, gain_ref[...], g_ref[:, h * V:(h + 1) * V])

    _lin_heads(GLA_HEADS, head_inputs, emit, st_ref, masks)


def _gla_core(proj, wd_pad, bd, gain, batch, seq):
    n = proj.shape[0]
    nt = seq // LIN_ROWS
    dk = GLA_HEADS * GLA_KEY_DIM
    dv = GLA_HEADS * GLA_VAL_DIM
    rows = lambda c: (lambda b, t: (b * nt + t, c))
    const = lambda b, t: (0, 0)
    return pl.pallas_call(
        _gla_kernel,
        out_shape=jax.ShapeDtypeStruct((n, dv), F32),
        grid=(batch, nt),
        in_specs=[
            pl.BlockSpec((LIN_ROWS, dk), rows(0)),
            pl.BlockSpec((LIN_ROWS, dk), rows(1)),
            pl.BlockSpec((LIN_ROWS, dv), rows(1)),
            pl.BlockSpec((LIN_ROWS, dv), rows(2)),
            pl.BlockSpec((LIN_ROWS, LANES), rows((2 * dk + 2 * dv) // LANES)),
            pl.BlockSpec((LANES, dk), const),
            pl.BlockSpec((1, dk), const),
            pl.BlockSpec((1, GLA_VAL_DIM), const),
        ],
        out_specs=pl.BlockSpec((LIN_ROWS, dv), rows(0)),
        scratch_shapes=[pltpu.VMEM((GLA_HEADS, GLA_VAL_DIM, GLA_KEY_DIM), F32)],
        compiler_params=_params(("parallel", "arbitrary")),
        name="gla_core",
    )(proj, proj, proj, proj, proj, wd_pad, bd.reshape(1, dk), gain.reshape(1, GLA_VAL_DIM))


def _hgrn_kernel(q_ref, f_ref, i_ref, g_ref, lbl_ref, gain_ref, o_ref, st_ref, *, layer):
    @pl.when(pl.program_id(1) == 0)
    def _():
        st_ref[...] = jnp.zeros(st_ref.shape, F32)

    K, V = HGRN_KEY_DIM, HGRN_VAL_DIM
    masks = _lin_masks(K)
    logits = lbl_ref[...]
    e = jnp.exp(logits - jnp.max(logits, axis=0, keepdims=True))
    p = e / jnp.sum(e, axis=0, keepdims=True)
    lb = jnp.zeros((1, logits.shape[1]), F32)
    for l in range(1, layer + 1):
        lb = lb + p[l:l + 1, :]

    def head_inputs(h):
        cols = slice(h * K, (h + 1) * K)
        lbh = lb[:, cols]
        f = lbh + (1.0 - lbh) * _sigmoid(f_ref[:, cols])
        return _silu(q_ref[:, cols]) * (K ** -0.5), 1.0 - f, i_ref[:, cols], jnp.log(f)

    def emit(h, o):
        cols = slice(h * K, (h + 1) * K)
        o_ref[:, cols] = _head_norm_gate(o, gain_ref[...], g_ref[:, cols])

    _lin_heads(HGRN_HEADS, head_inputs, emit, st_ref, masks)


def _hgrn_core(proj, lb_logits, gain, layer, batch, seq):
    n = proj.shape[0]
    nt = seq // LIN_ROWS
    d = D_MODEL
    rows = lambda c: (lambda b, t: (b * nt + t, c))
    const = lambda b, t: (0, 0)
    return pl.pallas_call(
        functools.partial(_hgrn_kernel, layer=layer),
        out_shape=jax.ShapeDtypeStruct((n, d), F32),
        grid=(batch, nt),
        in_specs=[
            pl.BlockSpec((LIN_ROWS, d), rows(0)),
            pl.BlockSpec((LIN_ROWS, d), rows(1)),
            pl.BlockSpec((LIN_ROWS, d), rows(2)),
            pl.BlockSpec((LIN_ROWS, d), rows(3)),
            pl.BlockSpec(lb_logits.shape, const),
            pl.BlockSpec((1, HGRN_VAL_DIM), const),
        ],
        out_specs=pl.BlockSpec((LIN_ROWS, d), rows(0)),
        scratch_shapes=[pltpu.VMEM((HGRN_HEADS, HGRN_VAL_DIM, HGRN_KEY_DIM), F32)],
        compiler_params=_params(("parallel", "arbitrary")),
        name="hgrn_core",
    )(proj, proj, proj, proj, lb_logits, gain.reshape(1, HGRN_VAL_DIM))


def kernel(x, norm_mix, norm_ffn, swa_w_in, swa_sinks, swa_w_out, moba_w_in, moba_w_out, gla_w_in, gla_w_decay_up, gla_b_decay, gla_out_norm, gla_w_out, hgrn_w_in, hgrn_lb_logits, hgrn_out_norm, hgrn_w_out, ffn_w_gate_up, ffn_w_down, final_norm):
    batch, seq, d = x.shape
    depth = norm_mix.shape[0]
    xf = x.reshape(batch * seq, d)
    for i in range(depth):
        kind, j = i % N_MIXERS, i // N_MIXERS
        if kind == 0:
            proj = _norm_proj(xf, norm_mix[i], swa_w_in[j].astype(BF16))
            o = _swa_core(proj, swa_sinks[j], batch, seq)
            w_out = swa_w_out[j]
        elif kind == 1:
            proj = _norm_proj(xf, norm_mix[i], _moba_slot_weights(moba_w_in[j]).astype(BF16))
            o = _moba_core(proj, batch, seq)
            w_out = moba_w_out[j]
        elif kind == 2:
            pad = LANES - GLA_GATE_RANK
            w_in = jnp.pad(gla_w_in[j], ((0, 0), (0, pad))).astype(BF16)
            wd_pad = jnp.pad(gla_w_decay_up[j], ((0, pad), (0, 0)))
            proj = _norm_proj(xf, norm_mix[i], w_in)
            o = _gla_core(proj, wd_pad, gla_b_decay[j], gla_out_norm[j], batch, seq)
            w_out = gla_w_out[j]
        else:
            proj = _norm_proj(xf, norm_mix[i], hgrn_w_in[j].astype(BF16))
            o = _hgrn_core(proj, hgrn_lb_logits, hgrn_out_norm[j], i, batch, seq)
            w_out = hgrn_w_out[j]
        last = i == depth - 1
        xf = _ffn(xf, o, w_out.astype(BF16), norm_ffn[i], ffn_w_gate_up[i].astype(BF16),
                  ffn_w_down[i].astype(BF16), final_norm, last)
    return xf.reshape(batch, seq, d)
```

```python
import functools
import math

import jax
import jax.numpy as jnp
from jax import lax
from jax.experimental import pallas as pl
from jax.experimental.pallas import tpu as pltpu

F32 = jnp.float32
BF16 = jnp.bfloat16

D_MODEL = 1024
HEAD_DIM = 64
N_Q_HEADS = 16
N_KV_HEADS = 4
GQA_GROUP = 4
ATTN_Q_DIM = 1024
ATTN_KV_DIM = 256
SWA_BLOCK = 128
SWA_WINDOW = 128
MOBA_BLOCK = 256
MOBA_TOP_K = 3
GLA_HEADS = 4
GLA_KEY_DIM = 128
GLA_VAL_DIM = 256
GLA_GATE_RANK = 16
GLA_GATE_TEMP = 16.0
HGRN_HEADS = 8
HGRN_KEY_DIM = 128
HGRN_VAL_DIM = 128
LIN_CHUNK = 64
D_FF = 2816
RMS_EPS = 1e-6
N_MIXERS = 4

LANES = 128
VMEM_LIMIT = 56 * 1024 * 1024
FFN_CHUNK = 256
ROW_TILE = 512
PROJ_ROWS = 1024
PROJ_COLS_MAX = 1408
LIN_ROWS = 256

ALIBI_SLOPES = tuple(2.0 ** (-8.0 * (i + 1) / N_Q_HEADS) for i in range(N_Q_HEADS))
NEG_INF = float("-inf")


def _params(sem):
    return pltpu.CompilerParams(dimension_semantics=sem, vmem_limit_bytes=VMEM_LIMIT)


def _dot(a, b):
    return jnp.dot(a, b, preferred_element_type=F32)


def _dot_nt(a, b):
    return lax.dot_general(a, b, (((1,), (1,)), ((), ())), preferred_element_type=F32)


def _dot_tn(a, b):
    return lax.dot_general(a, b, (((0,), (0,)), ((), ())), preferred_element_type=F32)


def _split(x):
    hi = x.astype(BF16)
    lo = (x - hi.astype(F32)).astype(BF16)
    return hi, lo


def _dot_split(a, b):
    ah, al = _split(a)
    bh, bl = _split(b)
    return _dot(ah, bh) + (_dot(ah, bl) + _dot(al, bh))


def _rms(x, gain):
    return x * lax.rsqrt(jnp.mean(x * x, axis=-1, keepdims=True) + RMS_EPS) * gain


def _sigmoid(x):
    return 1.0 / (1.0 + jnp.exp(-x))


def _silu(x):
    return x * _sigmoid(x)


def _norm_proj_kernel(x_ref, g_ref, w_ref, o_ref, h_ref):
    @pl.when(pl.program_id(1) == 0)
    def _():
        h_ref[...] = _rms(x_ref[...], g_ref[...]).astype(BF16)

    o_ref[...] = _dot(h_ref[...], w_ref[...])


def _norm_proj(x, gain, w):
    n, d = x.shape
    dout = w.shape[1]
    tn = next(c * LANES for c in range(PROJ_COLS_MAX // LANES, 0, -1) if (dout // LANES) % c == 0)
    assert dout % LANES == 0 and dout % tn == 0
    return pl.pallas_call(
        _norm_proj_kernel,
        out_shape=jax.ShapeDtypeStruct((n, dout), F32),
        grid=(n // PROJ_ROWS, dout // tn),
        in_specs=[
            pl.BlockSpec((PROJ_ROWS, d), lambda i, j: (i, 0)),
            pl.BlockSpec((1, d), lambda i, j: (0, 0)),
            pl.BlockSpec((d, tn), lambda i, j: (0, j)),
        ],
        out_specs=pl.BlockSpec((PROJ_ROWS, tn), lambda i, j: (i, j)),
        scratch_shapes=[pltpu.VMEM((PROJ_ROWS, d), BF16)],
        compiler_params=_params(("parallel", "arbitrary")),
        name="norm_proj",
    )(x, gain.reshape(1, d), w)


def _ffn_kernel(x_ref, o_ref, wo_ref, g_ref, wgu_ref, wd_ref, fg_ref, out_ref,
                x1_ref, h_ref, acc_ref, *, final_norm):
    x1 = x_ref[...] + _dot(o_ref[...].astype(BF16), wo_ref[...])
    x1_ref[...] = x1
    h_ref[...] = _rms(x1, g_ref[...]).astype(BF16)
    for c in range(D_FF // FFN_CHUNK):
        h = h_ref[...]
        gate = _dot(h, wgu_ref[:, c * FFN_CHUNK:(c + 1) * FFN_CHUNK])
        up = _dot(h, wgu_ref[:, D_FF + c * FFN_CHUNK:D_FF + (c + 1) * FFN_CHUNK])
        act = (_silu(gate) * up).astype(BF16)
        part = _dot(act, wd_ref[c * FFN_CHUNK:(c + 1) * FFN_CHUNK, :])
        if c == 0:
            acc_ref[...] = part
        else:
            acc_ref[...] += part
    y = x1_ref[...] + acc_ref[...]
    if final_norm:
        y = _rms(y, fg_ref[...])
    out_ref[...] = y


def _ffn(x, o, wo, gain, wgu, wd, final_gain, final_norm):
    n, d = x.shape
    const = lambda i: (0, 0)
    row = lambda i: (i, 0)
    return pl.pallas_call(
        functools.partial(_ffn_kernel, final_norm=final_norm),
        out_shape=jax.ShapeDtypeStruct((n, d), F32),
        grid=(n // ROW_TILE,),
        in_specs=[
            pl.BlockSpec((ROW_TILE, d), row),
            pl.BlockSpec((ROW_TILE, d), row),
            pl.BlockSpec((d, d), const, pipeline_mode=pl.Buffered(1)),
            pl.BlockSpec((1, d), const),
            pl.BlockSpec((d, 2 * D_FF), const, pipeline_mode=pl.Buffered(1)),
            pl.BlockSpec((D_FF, d), const, pipeline_mode=pl.Buffered(1)),
            pl.BlockSpec((1, d), const),
        ],
        out_specs=pl.BlockSpec((ROW_TILE, d), row),
        scratch_shapes=[
            pltpu.VMEM((ROW_TILE, d), F32),
            pltpu.VMEM((ROW_TILE, d), BF16),
            pltpu.VMEM((ROW_TILE, d), F32),
        ],
        compiler_params=_params(("parallel",)),
        name="outproj_ffn",
    )(x, o, wo, gain.reshape(1, d), wgu, wd, final_gain.reshape(1, d))


def _swa_kernel(q_ref, kp_ref, ko_ref, vp_ref, vo_ref, sink_ref, tab_ref, o_ref, ot_ref):
    L = SWA_BLOCK
    n = pl.program_id(1)
    first = jnp.where(n == 0, 1, 0)
    qt = (q_ref[...] * (LOG2E * HEAD_DIM ** -0.5)).T.astype(BF16)
    k = jnp.concatenate([kp_ref[...], ko_ref[...]], axis=0).astype(BF16)
    vt = jnp.concatenate([vp_ref[...], vo_ref[...]], axis=0).T.astype(BF16)
    ones = jnp.ones((SUM_ROWS, 2 * L), BF16)
    sinks = sink_ref[...] * LOG2E
    def scores(kv):
        rows = [qt[hd * HEAD_DIM:(hd + 1) * HEAD_DIM, :] for hd in range(kv * GQA_GROUP, (kv + 1) * GQA_GROUP)]
        return _dot(k[:, kv * HEAD_DIM:(kv + 1) * HEAD_DIM], jnp.concatenate(rows, axis=1))

    s_next = scores(0)
    for kv in range(N_KV_HEADS):
        heads = [kv * GQA_GROUP + g for g in range(GQA_GROUP)]
        s_all = s_next
        if kv + 1 < N_KV_HEADS:
            s_next = scores(kv + 1)
        probs, extras = [], []
        for g, hd in enumerate(heads):
            s = s_all[:, g * L:(g + 1) * L] + tab_ref[first, hd]
            sink = sinks[:, hd:hd + 1]
            m = jnp.maximum(jnp.max(s, axis=0, keepdims=True), sink)
            probs.append(jnp.exp2(s - m).astype(BF16))
            extras.append(jnp.exp2(sink - m))
        vaug = jnp.concatenate([vt[kv * HEAD_DIM:(kv + 1) * HEAD_DIM, :], ones], axis=0)
        pv_all = _dot(vaug, jnp.concatenate(probs, axis=1))
        for g, hd in enumerate(heads):
            pv = pv_all[:, g * L:(g + 1) * L]
            denom = pv[HEAD_DIM:HEAD_DIM + 1, :] + extras[g]
            ot_ref[hd * HEAD_DIM:(hd + 1) * HEAD_DIM, :] = pv[:HEAD_DIM, :] / denom
    o_ref[...] = ot_ref[...].T


def _swa_core(proj, sinks, batch, seq):
    n = proj.shape[0]
    L = SWA_BLOCK
    nb = seq // L
    kcol = ATTN_Q_DIM // ATTN_KV_DIM
    vcol = kcol + 1
    own = lambda c: (lambda b, i: (b * nb + i, c))
    prev = lambda c: (lambda b, i: (b * nb + jnp.maximum(i - 1, 0), c))
    key = jnp.arange(2 * L)[:, None]
    qry = jnp.arange(L)[None, :]
    dist = (L + qry) - key
    band = (dist >= 0) & (dist < SWA_WINDOW)
    slopes = (LOG2E * jnp.asarray(ALIBI_SLOPES, F32))[:, None, None]
    bias = -slopes * dist.astype(F32)[None]
    tab = jnp.stack([jnp.where(band[None], bias, MASKED),
                     jnp.where((band & (key >= L))[None], bias, MASKED)])
    return pl.pallas_call(
        _swa_kernel,
        out_shape=jax.ShapeDtypeStruct((n, ATTN_Q_DIM), F32),
        grid=(batch, nb),
        in_specs=[
            pl.BlockSpec((L, ATTN_Q_DIM), own(0)),
            pl.BlockSpec((L, ATTN_KV_DIM), prev(kcol)),
            pl.BlockSpec((L, ATTN_KV_DIM), own(kcol)),
            pl.BlockSpec((L, ATTN_KV_DIM), prev(vcol)),
            pl.BlockSpec((L, ATTN_KV_DIM), own(vcol)),
            pl.BlockSpec((1, N_Q_HEADS), lambda b, i: (0, 0)),
            pl.BlockSpec((2, N_Q_HEADS, 2 * L, L), lambda b, i: (0, 0, 0, 0), pipeline_mode=pl.Buffered(1)),
        ],
        out_specs=pl.BlockSpec((L, ATTN_Q_DIM), own(0)),
        scratch_shapes=[pltpu.VMEM((ATTN_Q_DIM, L), F32)],
        compiler_params=_params(("parallel", "parallel")),
        name="swa_core",
    )(proj, proj, proj, proj, proj, sinks.reshape(1, N_Q_HEADS), tab)


MOBA_SLOT = 2 * HEAD_DIM
PEN_LANES = 32
SLOPE_LANES = (HEAD_DIM + PEN_LANES, HEAD_DIM + PEN_LANES + 1)
STRIDE_LANES = (HEAD_DIM + PEN_LANES + 2, HEAD_DIM + PEN_LANES + 3)
MOBA_Q_WIDTH = N_Q_HEADS * MOBA_SLOT
MOBA_K_WIDTH = N_KV_HEADS * MOBA_SLOT
MOBA_WIDTH = MOBA_Q_WIDTH + MOBA_K_WIDTH + ATTN_KV_DIM
MASKED = -1e30
LOG2E = math.log2(math.e)
SUM_ROWS = 16
MOBA_KEY_GROUP = 16
MOBA_UNROLL = 4
NORM_MARGIN = 1.02
BOUND_LIMIT = 60.0

_KMEAN_BLOCKS = 8


def _kmean_kernel(k_ref, mean_ref, sqmax_ref):
    k = k_ref[...].reshape(_KMEAN_BLOCKS, MOBA_BLOCK, ATTN_KV_DIM)
    mean_ref[...] = jnp.sum(k, axis=1) * (1.0 / MOBA_BLOCK)
    sqmax_ref[...] = jnp.max(k * k, axis=1)


def _moba_kmean(proj):
    n = proj.shape[0]
    rows = _KMEAN_BLOCKS * MOBA_BLOCK
    out = jax.ShapeDtypeStruct((n // MOBA_BLOCK, ATTN_KV_DIM), F32)
    spec = pl.BlockSpec((_KMEAN_BLOCKS, ATTN_KV_DIM), lambda i: (i, 0))
    return pl.pallas_call(
        _kmean_kernel,
        out_shape=(out, out),
        grid=(n // rows,),
        in_specs=[pl.BlockSpec((rows, ATTN_KV_DIM), lambda i: (i, ATTN_Q_DIM // ATTN_KV_DIM))],
        out_specs=(spec, spec),
        compiler_params=_params(("parallel",)),
        name="moba_kmean",
    )(proj)


def _moba_select_kernel(q_ref, km_ref, expand_ref, slope_ref, slot_ref, kslot_ref, k_ref, v_ref,
                        qa_ref, ks_ref, vt_ref, qn_ref, *, nblk):
    n = pl.program_id(1)
    Lb = MOBA_BLOCK
    qc = (q_ref[...] * (HEAD_DIM ** -0.5)).T
    gate = _dot_split(km_ref[0], qc)
    fill = jnp.zeros((MOBA_SLOT - HEAD_DIM, Lb), F32)
    qt = jnp.concatenate([piece for hd in range(N_Q_HEADS)
                          for piece in (qc[hd * HEAD_DIM:(hd + 1) * HEAD_DIM, :], fill)], axis=0)
    blk = lax.broadcasted_iota(jnp.int32, (nblk, Lb), 0)
    row = blk.astype(F32)
    unselected = []
    for h in range(N_KV_HEADS):
        g = jnp.where(blk < n, gate[h * nblk:(h + 1) * nblk, :], NEG_INF)
        sel = jnp.zeros((nblk, Lb), F32)
        for _ in range(MOBA_TOP_K):
            mx = jnp.max(g, axis=0, keepdims=True)
            is_max = (g == mx) & (mx > NEG_INF)
            first = jnp.min(jnp.where(is_max, row, float(nblk)), axis=0, keepdims=True)
            pick = row == first
            sel = jnp.where(pick, 1.0, sel)
            g = jnp.where(pick, NEG_INF, g)
        unselected.append(1.0 - sel)
    unselected = jnp.concatenate(unselected, axis=0).astype(BF16)
    qs = qt * LOG2E
    norm2 = _dot(slot_ref[...], (qs * qs).astype(BF16))
    qn_ref[...] = jnp.sqrt(norm2[:N_Q_HEADS, :]) * NORM_MARGIN
    q_feat = qs + slope_ref[...]
    for var in range(qa_ref.shape[0]):
        penalty = _dot(expand_ref[var], unselected)
        qa_ref[var] = (q_feat + penalty).astype(BF16)
    ks_ref[0] = _dot(k_ref[...].astype(BF16), kslot_ref[...]).astype(BF16)
    vt_ref[0] = v_ref[...].T.astype(BF16)


def _moba_select(proj, km_mat, expand, slope_feat, batch, seq):
    n = proj.shape[0]
    nblk = seq // MOBA_BLOCK
    width = N_KV_HEADS * nblk
    nvar = expand.shape[0]
    kcol = ATTN_Q_DIM // ATTN_KV_DIM
    slot = jnp.arange(MOBA_Q_WIDTH) // MOBA_SLOT
    lane = jnp.arange(MOBA_Q_WIDTH) % MOBA_SLOT
    slot_sum = ((jnp.arange(LANES)[:, None] == slot[None, :]) & (lane[None, :] < HEAD_DIM)).astype(BF16)
    src = jnp.arange(ATTN_KV_DIM)
    dst = jnp.arange(MOBA_K_WIDTH)
    k_slot = ((src // HEAD_DIM)[:, None] == (dst // MOBA_SLOT)[None, :]) & \
             ((src % HEAD_DIM)[:, None] == (dst % MOBA_SLOT)[None, :])
    k_slot = k_slot.astype(BF16)
    return pl.pallas_call(
        functools.partial(_moba_select_kernel, nblk=nblk),
        out_shape=(jax.ShapeDtypeStruct((nvar, MOBA_Q_WIDTH, n), BF16),
                   jax.ShapeDtypeStruct((n // MOBA_BLOCK, MOBA_BLOCK, MOBA_K_WIDTH), BF16),
                   jax.ShapeDtypeStruct((n // MOBA_BLOCK, ATTN_KV_DIM, MOBA_BLOCK), BF16),
                   jax.ShapeDtypeStruct((N_Q_HEADS, n), F32)),
        grid=(batch, nblk),
        in_specs=[
            pl.BlockSpec((MOBA_BLOCK, ATTN_Q_DIM), lambda b, i: (b * nblk + i, 0)),
            pl.BlockSpec((1, width, ATTN_Q_DIM), lambda b, i: (b, 0, 0)),
            pl.BlockSpec((nvar, MOBA_Q_WIDTH, width), lambda b, i: (0, 0, 0)),
            pl.BlockSpec((MOBA_Q_WIDTH, MOBA_BLOCK), lambda b, i: (0, 0)),
            pl.BlockSpec((LANES, MOBA_Q_WIDTH), lambda b, i: (0, 0)),
            pl.BlockSpec((ATTN_KV_DIM, MOBA_K_WIDTH), lambda b, i: (0, 0)),
            pl.BlockSpec((MOBA_BLOCK, ATTN_KV_DIM), lambda b, i: (b * nblk + i, kcol)),
            pl.BlockSpec((MOBA_BLOCK, ATTN_KV_DIM), lambda b, i: (b * nblk + i, kcol + 1)),
        ],
        out_specs=(pl.BlockSpec((nvar, MOBA_Q_WIDTH, MOBA_BLOCK), lambda b, i: (0, 0, b * nblk + i)),
                   pl.BlockSpec((1, MOBA_BLOCK, MOBA_K_WIDTH), lambda b, i: (b * nblk + i, 0, 0)),
                   pl.BlockSpec((1, ATTN_KV_DIM, MOBA_BLOCK), lambda b, i: (b * nblk + i, 0, 0)),
                   pl.BlockSpec((N_Q_HEADS, MOBA_BLOCK), lambda b, i: (0, b * nblk + i))),
        compiler_params=_params(("parallel", "parallel")),
        name="moba_select",
    )(proj, km_mat, expand, slope_feat, slot_sum, k_slot, proj, proj)


def _moba_scores(ka, qa_ref, var, kv):
    heads = [kv * GQA_GROUP + g for g in range(GQA_GROUP)]
    qt = jnp.concatenate([qa_ref[var, hd * MOBA_SLOT:(hd + 1) * MOBA_SLOT, :] for hd in heads], axis=1)
    return _dot(ka, qt)


def _moba_blocks_update(blocks, qa_ref, var, m_ref, l_ref, acc_ref, causal=None, qn_ref=None):
    keys = lambda kv: jnp.concatenate([blk[0](kv) for blk in blocks], axis=0)
    s_next = _moba_scores(keys(0), qa_ref, var, 0)
    for kv in range(N_KV_HEADS):
        s_all = s_next
        if kv + 1 < N_KV_HEADS:
            s_next = _moba_scores(keys(kv + 1), qa_ref, var, kv + 1)
        vt = jnp.concatenate([blk[1](kv) for blk in blocks], axis=1)
        vt = jnp.concatenate([vt, jnp.ones((SUM_ROWS, vt.shape[1]), BF16)], axis=0)
        k_norm = None
        if qn_ref is not None:
            k_norm = functools.reduce(jnp.maximum, [blk[2](kv) for blk in blocks])
        _moba_softmax_pv(s_all, vt, m_ref, l_ref, acc_ref, kv, causal, qn_ref, k_norm)


def _moba_softmax_pv(s_all, vt, m_ref, l_ref, acc_ref, kv, causal, qn_ref, k_norm):
    Lb = MOBA_BLOCK
    heads = [kv * GQA_GROUP + g for g in range(GQA_GROUP)]
    probs, alphas = [], []
    for g, hd in enumerate(heads):
        s = s_all[:, g * Lb:(g + 1) * Lb]
        if causal is not None:
            s = jnp.where(causal, s, MASKED)
        m_old = m_ref[hd:hd + 1, :]
        if k_norm is None:
            m_new = jnp.maximum(m_old, jnp.max(s, axis=0, keepdims=True))
        else:
            qry = lax.broadcasted_iota(jnp.int32, (1, Lb), 1).astype(F32)
            alibi_top = (qry - (Lb - 1)) * (LOG2E * ALIBI_SLOPES[hd])
            m_new = jnp.maximum(m_old, qn_ref[hd:hd + 1, :] * k_norm + alibi_top)
        alphas.append(jnp.exp2(m_old - m_new))
        probs.append(jnp.exp2(s - m_new).astype(BF16))
        m_ref[hd:hd + 1, :] = m_new
    pv_all = _dot(vt, jnp.concatenate(probs, axis=1))
    for g, hd in enumerate(heads):
        pv = pv_all[:, g * Lb:(g + 1) * Lb]
        rows = slice(hd * HEAD_DIM, (hd + 1) * HEAD_DIM)
        l_ref[hd:hd + 1, :] = alphas[g] * l_ref[hd:hd + 1, :] + pv[HEAD_DIM:HEAD_DIM + 1, :]
        acc_ref[rows, :] = alphas[g] * acc_ref[rows, :] + pv[:HEAD_DIM, :]


def _moba_attn_kernel(nq_ref, jg_ref, bounded_ref, qa_ref, k_ref, vt_ref, qn_ref, kn_ref, o_ref,
                      m_ref, l_ref, acc_ref, *, group, nblk):
    Lb = MOBA_BLOCK
    step = pl.program_id(1)
    n = nq_ref[step]
    first = jg_ref[step] * group
    var = first // PEN_LANES
    bounded = bounded_ref[pl.program_id(0) * nblk + n]

    @pl.when(first == 0)
    def _():
        m_ref[...] = jnp.full(m_ref.shape, NEG_INF, F32)
        l_ref[...] = jnp.zeros(l_ref.shape, F32)
        acc_ref[...] = jnp.zeros(acc_ref.shape, F32)

    lane = lax.broadcasted_iota(jnp.int32, (Lb, MOBA_SLOT), 1)
    key_off = (lax.broadcasted_iota(jnp.int32, (Lb, MOBA_SLOT), 0) - (Lb - 1)).astype(F32)
    slope_lanes = (lane == SLOPE_LANES[0]) | (lane == SLOPE_LANES[1])
    stride_lanes = (lane == STRIDE_LANES[0]) | (lane == STRIDE_LANES[1])
    key_feat = jnp.where(slope_lanes, key_off, 0.0)

    def past_block(i):
        j = first + i
        extra = jnp.where(stride_lanes, (j - n).astype(F32), key_feat)
        extra = jnp.where(lane == HEAD_DIM + (j - var * PEN_LANES), 1.0, extra).astype(BF16)
        return (lambda kv: k_ref[i, :, kv * MOBA_SLOT:(kv + 1) * MOBA_SLOT] + extra,
                lambda kv: vt_ref[i, kv * HEAD_DIM:(kv + 1) * HEAD_DIM, :],
                lambda kv: kn_ref[0, pl.ds(i * N_KV_HEADS + kv, 1), :])

    def past_blocks(count, shift_ref):
        def body(it, carry):
            blocks = [past_block(carry + u) for u in range(count)]
            _moba_blocks_update(blocks, qa_ref, var, m_ref, l_ref, acc_ref, None, shift_ref)
            return carry + count
        return body

    n_past = jnp.minimum(n - first, group)
    n_multi = n_past // MOBA_UNROLL

    def run_past(shift_ref):
        done = lax.fori_loop(0, n_multi, past_blocks(MOBA_UNROLL, shift_ref), 0)
        lax.fori_loop(0, n_past - done, past_blocks(1, shift_ref), done)

    @pl.when(bounded == 1)
    def _():
        run_past(qn_ref)

    @pl.when(bounded != 1)
    def _():
        run_past(None)

    @pl.when(n - first < group)
    def _():
        i = n - first
        key = lax.broadcasted_iota(jnp.int32, (Lb, Lb), 0)
        qry = lax.broadcasted_iota(jnp.int32, (Lb, Lb), 1)
        causal = key <= qry
        extra = key_feat.astype(BF16)
        own = (lambda kv: k_ref[i, :, kv * MOBA_SLOT:(kv + 1) * MOBA_SLOT] + extra,
               lambda kv: vt_ref[i, kv * HEAD_DIM:(kv + 1) * HEAD_DIM, :])
        _moba_blocks_update([own], qa_ref, var, m_ref, l_ref, acc_ref, causal)
        inv = 1.0 / l_ref[...]
        acc = acc_ref[...].reshape(N_Q_HEADS, HEAD_DIM, Lb) * inv[:, None, :]
        o_ref[...] = acc.reshape(N_Q_HEADS * HEAD_DIM, Lb).T


def _moba_attn(qaug, kslot, vt, qnorm, knorm, batch, seq):
    n = batch * seq
    nblk = seq // MOBA_BLOCK
    group = min(MOBA_KEY_GROUP, nblk)
    assert nblk % group == 0
    pairs = [(i, jg) for i in range(nblk) for jg in range(i // group + 1)]
    nq = jnp.asarray([p[0] for p in pairs], jnp.int32)
    jk = jnp.asarray([p[1] for p in pairs], jnp.int32)
    ngrp = nblk // group
    nvar = qaug.shape[0]
    assert PEN_LANES % group == 0, "a key group must not straddle two query variants"
    q_top = jnp.max(qnorm.reshape(N_Q_HEADS, batch, nblk, MOBA_BLOCK), axis=(0, 3))
    bounded = (q_top * jnp.max(knorm, axis=(1, 2))[:, None] <= BOUND_LIMIT).astype(jnp.int32).reshape(-1)
    kn_rows = jnp.broadcast_to(knorm.reshape(batch * ngrp, group * N_KV_HEADS, 1),
                               (batch * ngrp, group * N_KV_HEADS, MOBA_BLOCK))
    qmap = lambda b, s, nq_r, jk_r, bd_r: (b * nblk + nq_r[s], 0)
    kmap = lambda b, s, nq_r, jk_r, bd_r: (b * ngrp + jk_r[s], 0, 0)
    grid_spec = pltpu.PrefetchScalarGridSpec(
        num_scalar_prefetch=3,
        grid=(batch, len(pairs)),
        in_specs=[
            pl.BlockSpec((nvar, MOBA_Q_WIDTH, MOBA_BLOCK), lambda b, s, nq_r, jk_r, bd_r: (0, 0, b * nblk + nq_r[s])),
            pl.BlockSpec((group, MOBA_BLOCK, MOBA_K_WIDTH), kmap),
            pl.BlockSpec((group, ATTN_KV_DIM, MOBA_BLOCK), kmap),
            pl.BlockSpec((N_Q_HEADS, MOBA_BLOCK), lambda b, s, nq_r, jk_r, bd_r: (0, b * nblk + nq_r[s])),
            pl.BlockSpec((1, group * N_KV_HEADS, MOBA_BLOCK), kmap),
        ],
        out_specs=pl.BlockSpec((MOBA_BLOCK, ATTN_Q_DIM), qmap),
        scratch_shapes=[
            pltpu.VMEM((N_Q_HEADS, MOBA_BLOCK), F32),
            pltpu.VMEM((N_Q_HEADS, MOBA_BLOCK), F32),
            pltpu.VMEM((N_Q_HEADS * HEAD_DIM, MOBA_BLOCK), F32),
        ],
    )
    return pl.pallas_call(
        functools.partial(_moba_attn_kernel, group=group, nblk=nblk),
        out_shape=jax.ShapeDtypeStruct((n, ATTN_Q_DIM), F32),
        grid_spec=grid_spec,
        compiler_params=_params(("parallel", "arbitrary")),
        name="moba_attn",
    )(nq, jk, bounded, qaug, kslot, vt, qnorm, kn_rows)


def _moba_slot_weights(w_in):
    d = w_in.shape[0]
    fill = MOBA_SLOT - HEAD_DIM
    wq = w_in[:, :ATTN_Q_DIM].reshape(d, N_Q_HEADS, HEAD_DIM)
    wk = w_in[:, ATTN_Q_DIM:ATTN_Q_DIM + ATTN_KV_DIM].reshape(d, N_KV_HEADS, HEAD_DIM)
    wq = jnp.pad(wq, ((0, 0), (0, 0), (0, fill))).reshape(d, MOBA_Q_WIDTH)
    wk = jnp.pad(wk, ((0, 0), (0, 0), (0, fill))).reshape(d, MOBA_K_WIDTH)
    return jnp.concatenate([wq, wk, w_in[:, ATTN_Q_DIM + ATTN_KV_DIM:]], axis=1)


def _moba_core(proj, batch, seq):
    nblk = seq // MOBA_BLOCK
    nvar = -(-nblk // PEN_LANES)
    width = N_KV_HEADS * nblk
    kmean, ksqmax = _moba_kmean(proj)
    knorm = jnp.sqrt(jnp.sum(ksqmax.reshape(batch, nblk, N_KV_HEADS, HEAD_DIM), axis=-1))
    km = kmean.reshape(batch, nblk, N_KV_HEADS, HEAD_DIM).transpose(0, 2, 3, 1)
    eye = jnp.eye(N_KV_HEADS, dtype=F32)
    km_mat = jnp.einsum('bhej,hH->bheHj', km, eye)
    km_mat = jnp.broadcast_to(km_mat[:, :, None], (batch, N_KV_HEADS, GQA_GROUP, HEAD_DIM, N_KV_HEADS, nblk))
    km_mat = km_mat.reshape(batch, ATTN_Q_DIM, width).transpose(0, 2, 1)
    h_of = jnp.arange(width) // nblk
    c_of = jnp.arange(width) % nblk
    slot = jnp.arange(MOBA_Q_WIDTH) // MOBA_SLOT
    lane = jnp.arange(MOBA_Q_WIDTH) % MOBA_SLOT
    hit = (h_of[:, None] == (slot // GQA_GROUP)[None, :]) & (lane[None, :] == HEAD_DIM + (c_of % PEN_LANES)[:, None])
    in_var = (c_of // PEN_LANES)[None, :, None] == jnp.arange(nvar)[:, None, None]
    expand = jnp.where(hit[None] & in_var, MASKED, 0.0).astype(BF16).transpose(0, 2, 1)
    sigma = LOG2E * jnp.asarray(ALIBI_SLOPES, F32)
    slope_feat = jnp.zeros((MOBA_Q_WIDTH,), F32)
    for lanes, value in ((SLOPE_LANES, sigma), (STRIDE_LANES, sigma * MOBA_BLOCK)):
        hi = value.astype(BF16).astype(F32)
        lo = (value - hi).astype(BF16).astype(F32)
        slope_feat = slope_feat + jnp.where(lane == lanes[0], hi[slot], 0.0) + jnp.where(lane == lanes[1], lo[slot], 0.0)
    slope_feat = jnp.broadcast_to(slope_feat[:, None], (MOBA_Q_WIDTH, MOBA_BLOCK))
    qaug, kslot, vt, qnorm = _moba_select(proj, km_mat, expand, slope_feat, batch, seq)
    return _moba_attn(qaug, kslot, vt, qnorm, knorm, batch, seq)


LIN_CHUNKS = LIN_ROWS // LIN_CHUNK


def _lin_masks(key_dim):
    R = LIN_ROWS
    chunk_bits = int(math.log2(LIN_CHUNK))
    key_bits = int(math.log2(key_dim))
    assert (1 << chunk_bits) == LIN_CHUNK and (1 << key_bits) == key_dim
    r = lax.broadcasted_iota(jnp.int32, (R, R), 0)
    c = lax.broadcasted_iota(jnp.int32, (R, R), 1)
    same = jnp.right_shift(r, chunk_bits) == jnp.right_shift(c, chunk_bits)
    causal = same & (r >= c)
    sums = jnp.concatenate([jnp.where(causal, 1.0, 0.0), jnp.where(same, 1.0, 0.0)], axis=0).astype(BF16)
    rr = lax.broadcasted_iota(jnp.int32, (R, LIN_CHUNKS * key_dim), 0)
    cc = lax.broadcasted_iota(jnp.int32, (R, LIN_CHUNKS * key_dim), 1)
    own = jnp.right_shift(rr, chunk_bits) == jnp.right_shift(cc, key_bits)
    return sums, causal, own


def _lin_prefix(log_g, sums):
    hi = log_g.astype(BF16)
    r1 = log_g - hi.astype(F32)
    mid = r1.astype(BF16)
    lo = (r1 - mid.astype(F32)).astype(BF16)
    return _dot(sums, jnp.concatenate([hi, mid, lo], axis=1))


def _lin_heads(n_heads, head_inputs, emit, st_ref, masks):
    cur = head_inputs(0)
    pre = _lin_prefix(cur[3], masks[0])
    for h in range(n_heads):
        nxt = pre_nxt = None
        if h + 1 < n_heads:
            nxt = head_inputs(h + 1)
            pre_nxt = _lin_prefix(nxt[3], masks[0])
        emit(h, _lin_tile(cur[0], cur[1], cur[2], pre, st_ref, h, masks))
        cur, pre = nxt, pre_nxt


def _lin_tile(q, k, v, acc, st_ref, h, masks):
    _, causal, own = masks
    R, K = q.shape
    acc = acc[:, :K] + (acc[:, K:2 * K] + acc[:, 2 * K:])
    b, b_chunk = acc[:R], acc[R:]
    q_dec = (q * jnp.exp(b)).astype(BF16)
    k_dec = (k * jnp.exp(-b)).astype(BF16)
    k_tail = (k * jnp.exp(b_chunk - b)).astype(BF16)
    vb = v.astype(BF16)
    attn = jnp.where(causal, _dot_nt(q_dec, k_dec), 0.0).astype(BF16)
    o = _dot(attn, vb)
    tile_lanes = lambda x: jnp.concatenate([x] * LIN_CHUNKS, axis=1)
    zero = jnp.zeros((), BF16)
    ds = _dot_tn(vb, jnp.where(own, tile_lanes(k_tail), zero))
    st = st_ref[h]
    states = []
    for c in range(LIN_CHUNKS):
        states.append(st.astype(BF16))
        decay = jnp.exp(b_chunk[c * LIN_CHUNK:c * LIN_CHUNK + 1, :])
        st = st * decay + ds[:, c * K:(c + 1) * K]
    st_ref[h] = st
    o = o + _dot_nt(jnp.where(own, tile_lanes(q_dec), zero), jnp.concatenate(states, axis=1))
    return o


def _head_norm_gate(o, gain, g):
    o = o * lax.rsqrt(jnp.mean(o * o, axis=-1, keepdims=True) + RMS_EPS) * gain
    return o * _silu(g)


def _gla_kernel(q_ref, k_ref, v_ref, g_ref, a_ref, wd_ref, bd_ref, gain_ref, o_ref, st_ref):
    @pl.when(pl.program_id(1) == 0)
    def _():
        st_ref[...] = jnp.zeros(st_ref.shape, F32)

    K, V = GLA_KEY_DIM, GLA_VAL_DIM
    masks = _lin_masks(K)
    z = _dot_split(a_ref[...], wd_ref[...]) + bd_ref[...]
    log_alpha = (jnp.minimum(z, 0.0) - jnp.log(1.0 + jnp.exp(-jnp.abs(z)))) * (1.0 / GLA_GATE_TEMP)
    def head_inputs(h):
        return (q_ref[:, h * K:(h + 1) * K] * (K ** -0.5), k_ref[:, h * K:(h + 1) * K],
                v_ref[:, h * V:(h + 1) * V], log_alpha[:, h * K:(h + 1) * K])

    def emit(h, o):
        o_ref[:, h * V:(h + 1) * V] = _head_norm_gate(o, gain_ref[...], g_ref[:, h * V:(h + 1) * V])

    _lin_heads(GLA_HEADS, head_inputs, emit, st_ref, masks)


def _gla_core(proj, wd_pad, bd, gain, batch, seq):
    n = proj.shape[0]
    nt = seq // LIN_ROWS
    dk = GLA_HEADS * GLA_KEY_DIM
    dv = GLA_HEADS * GLA_VAL_DIM
    rows = lambda c: (lambda b, t: (b * nt + t, c))
    const = lambda b, t: (0, 0)
    return pl.pallas_call(
        _gla_kernel,
        out_shape=jax.ShapeDtypeStruct((n, dv), F32),
        grid=(batch, nt),
        in_specs=[
            pl.BlockSpec((LIN_ROWS, dk), rows(0)),
            pl.BlockSpec((LIN_ROWS, dk), rows(1)),
            pl.BlockSpec((LIN_ROWS, dv), rows(1)),
            pl.BlockSpec((LIN_ROWS, dv), rows(2)),
            pl.BlockSpec((LIN_ROWS, LANES), rows((2 * dk + 2 * dv) // LANES)),
            pl.BlockSpec((LANES, dk), const),
            pl.BlockSpec((1, dk), const),
            pl.BlockSpec((1, GLA_VAL_DIM), const),
        ],
        out_specs=pl.BlockSpec((LIN_ROWS, dv), rows(0)),
        scratch_shapes=[pltpu.VMEM((GLA_HEADS, GLA_VAL_DIM, GLA_KEY_DIM), F32)],
        compiler_params=_params(("parallel", "arbitrary")),
        name="gla_core",
    )(proj, proj, proj, proj, proj, wd_pad, bd.reshape(1, dk), gain.reshape(1, GLA_VAL_DIM))


def _hgrn_kernel(q_ref, f_ref, i_ref, g_ref, lbl_ref, gain_ref, o_ref, st_ref, *, layer):
    @pl.when(pl.program_id(1) == 0)
    def _():
        st_ref[...] = jnp.zeros(st_ref.shape, F32)

    K, V = HGRN_KEY_DIM, HGRN_VAL_DIM
    masks = _lin_masks(K)
    logits = lbl_ref[...]
    e = jnp.exp(logits - jnp.max(logits, axis=0, keepdims=True))
    p = e / jnp.sum(e, axis=0, keepdims=True)
    lb = jnp.zeros((1, logits.shape[1]), F32)
    for l in range(1, layer + 1):
        lb = lb + p[l:l + 1, :]

    def head_inputs(h):
        cols = slice(h * K, (h + 1) * K)
        lbh = lb[:, cols]
        f = lbh + (1.0 - lbh) * _sigmoid(f_ref[:, cols])
        return _silu(q_ref[:, cols]) * (K ** -0.5), 1.0 - f, i_ref[:, cols], jnp.log(f)

    def emit(h, o):
        cols = slice(h * K, (h + 1) * K)
        o_ref[:, cols] = _head_norm_gate(o, gain_ref[...], g_ref[:, cols])

    _lin_heads(HGRN_HEADS, head_inputs, emit, st_ref, masks)


def _hgrn_core(proj, lb_logits, gain, layer, batch, seq):
    n = proj.shape[0]
    nt = seq // LIN_ROWS
    d = D_MODEL
    rows = lambda c: (lambda b, t: (b * nt + t, c))
    const = lambda b, t: (0, 0)
    return pl.pallas_call(
        functools.partial(_hgrn_kernel, layer=layer),
        out_shape=jax.ShapeDtypeStruct((n, d), F32),
        grid=(batch, nt),
        in_specs=[
            pl.BlockSpec((LIN_ROWS, d), rows(0)),
            pl.BlockSpec((LIN_ROWS, d), rows(1)),
            pl.BlockSpec((LIN_ROWS, d), rows(2)),
            pl.BlockSpec((LIN_ROWS, d), rows(3)),
            pl.BlockSpec(lb_logits.shape, const),
            pl.BlockSpec((1, HGRN_VAL_DIM), const),
        ],
        out_specs=pl.BlockSpec((LIN_ROWS, d), rows(0)),
        scratch_shapes=[pltpu.VMEM((HGRN_HEADS, HGRN_VAL_DIM, HGRN_KEY_DIM), F32)],
        compiler_params=_params(("parallel", "arbitrary")),
        name="hgrn_core",
    )(proj, proj, proj, proj, lb_logits, gain.reshape(1, HGRN_VAL_DIM))


def kernel(x, norm_mix, norm_ffn, swa_w_in, swa_sinks, swa_w_out, moba_w_in, moba_w_out, gla_w_in, gla_w_decay_up, gla_b_decay, gla_out_norm, gla_w_out, hgrn_w_in, hgrn_lb_logits, hgrn_out_norm, hgrn_w_out, ffn_w_gate_up, ffn_w_down, final_norm):
    batch, seq, d = x.shape
    depth = norm_mix.shape[0]
    xf = x.reshape(batch * seq, d)
    for i in range(depth):
        kind, j = i % N_MIXERS, i // N_MIXERS
        if kind == 0:
            proj = _norm_proj(xf, norm_mix[i], swa_w_in[j].astype(BF16))
            o = _swa_core(proj, swa_sinks[j], batch, seq)
            w_out = swa_w_out[j]
        elif kind == 1:
            proj = _norm_proj(xf, norm_mix[i], moba_w_in[j].astype(BF16))
            o = _moba_core(proj, batch, seq)
            w_out = moba_w_out[j]
        elif kind == 2:
            pad = LANES - GLA_GATE_RANK
            w_in = jnp.pad(gla_w_in[j], ((0, 0), (0, pad))).astype(BF16)
            wd_pad = jnp.pad(gla_w_decay_up[j], ((0, pad), (0, 0)))
            proj = _norm_proj(xf, norm_mix[i], w_in)
            o = _gla_core(proj, wd_pad, gla_b_decay[j], gla_out_norm[j], batch, seq)
            w_out = gla_w_out[j]
        else:
            proj = _norm_proj(xf, norm_mix[i], hgrn_w_in[j].astype(BF16))
            o = _hgrn_core(proj, hgrn_lb_logits, hgrn_out_norm[j], i, batch, seq)
            w_out = hgrn_w_out[j]
        last = i == depth - 1
        xf = _ffn(xf, o, w_out.astype(BF16), norm_ffn[i], ffn_w_gate_up[i].astype(BF16),
                  ffn_w_down[i].astype(BF16), final_norm, last)
    return xf.reshape(batch, seq, d)
```

```python
import functools
import math

import jax
import jax.numpy as jnp
from jax import lax
from jax.experimental import pallas as pl
from jax.experimental.pallas import tpu as pltpu

F32 = jnp.float32
BF16 = jnp.bfloat16

D_MODEL = 1024
HEAD_DIM = 64
N_Q_HEADS = 16
N_KV_HEADS = 4
GQA_GROUP = 4
ATTN_Q_DIM = 1024
ATTN_KV_DIM = 256
SWA_BLOCK = 128
SWA_WINDOW = 128
MOBA_BLOCK = 256
MOBA_TOP_K = 3
GLA_HEADS = 4
GLA_KEY_DIM = 128
GLA_VAL_DIM = 256
GLA_GATE_RANK = 16
GLA_GATE_TEMP = 16.0
HGRN_HEADS = 8
HGRN_KEY_DIM = 128
HGRN_VAL_DIM = 128
LIN_CHUNK = 64
D_FF = 2816
RMS_EPS = 1e-6
N_MIXERS = 4

LANES = 128
VMEM_LIMIT = 56 * 1024 * 1024
FFN_CHUNK = 256
ROW_TILE = 512
PROJ_ROWS = 1024
PROJ_COLS_MAX = 1408
LIN_ROWS = 256

ALIBI_SLOPES = tuple(2.0 ** (-8.0 * (i + 1) / N_Q_HEADS) for i in range(N_Q_HEADS))
NEG_INF = float("-inf")


def _params(sem):
    return pltpu.CompilerParams(dimension_semantics=sem, vmem_limit_bytes=VMEM_LIMIT)


def _dot(a, b):
    return jnp.dot(a, b, preferred_element_type=F32)


def _dot_nt(a, b):
    return lax.dot_general(a, b, (((1,), (1,)), ((), ())), preferred_element_type=F32)


def _dot_tn(a, b):
    return lax.dot_general(a, b, (((0,), (0,)), ((), ())), preferred_element_type=F32)


def _split(x):
    hi = x.astype(BF16)
    lo = (x - hi.astype(F32)).astype(BF16)
    return hi, lo


def _dot_split(a, b):
    ah, al = _split(a)
    bh, bl = _split(b)
    return _dot(ah, bh) + (_dot(ah, bl) + _dot(al, bh))


def _rms(x, gain):
    return x * lax.rsqrt(jnp.mean(x * x, axis=-1, keepdims=True) + RMS_EPS) * gain


def _sigmoid(x):
    return 1.0 / (1.0 + jnp.exp(-x))


def _silu(x):
    return x * _sigmoid(x)


def _norm_proj_kernel(x_ref, g_ref, w_ref, o_ref, h_ref):
    @pl.when(pl.program_id(1) == 0)
    def _():
        h_ref[...] = _rms(x_ref[...], g_ref[...]).astype(BF16)

    o_ref[...] = _dot(h_ref[...], w_ref[...])


def _norm_proj(x, gain, w):
    n, d = x.shape
    dout = w.shape[1]
    tn = next(c * LANES for c in range(PROJ_COLS_MAX // LANES, 0, -1) if (dout // LANES) % c == 0)
    assert dout % LANES == 0 and dout % tn == 0
    return pl.pallas_call(
        _norm_proj_kernel,
        out_shape=jax.ShapeDtypeStruct((n, dout), F32),
        grid=(n // PROJ_ROWS, dout // tn),
        in_specs=[
            pl.BlockSpec((PROJ_ROWS, d), lambda i, j: (i, 0)),
            pl.BlockSpec((1, d), lambda i, j: (0, 0)),
            pl.BlockSpec((d, tn), lambda i, j: (0, j)),
        ],
        out_specs=pl.BlockSpec((PROJ_ROWS, tn), lambda i, j: (i, j)),
        scratch_shapes=[pltpu.VMEM((PROJ_ROWS, d), BF16)],
        compiler_params=_params(("parallel", "arbitrary")),
        name="norm_proj",
    )(x, gain.reshape(1, d), w)


def _ffn_kernel(x_ref, o_ref, wo_ref, g_ref, wgu_ref, wd_ref, fg_ref, out_ref,
                x1_ref, h_ref, acc_ref, *, final_norm):
    x1 = x_ref[...] + _dot(o_ref[...].astype(BF16), wo_ref[...])
    x1_ref[...] = x1
    h_ref[...] = _rms(x1, g_ref[...]).astype(BF16)
    def gate_up(c):
        h = h_ref[...]
        return (_dot(h, wgu_ref[:, c * FFN_CHUNK:(c + 1) * FFN_CHUNK]),
                _dot(h, wgu_ref[:, D_FF + c * FFN_CHUNK:D_FF + (c + 1) * FFN_CHUNK]))

    n_chunks = D_FF // FFN_CHUNK
    ahead = gate_up(0)
    for c in range(n_chunks):
        gate, up = ahead
        if c + 1 < n_chunks:
            ahead = gate_up(c + 1)
        act = (_silu(gate) * up).astype(BF16)
        part = _dot(act, wd_ref[c * FFN_CHUNK:(c + 1) * FFN_CHUNK, :])
        if c == 0:
            acc_ref[...] = part
        else:
            acc_ref[...] += part
    y = x1_ref[...] + acc_ref[...]
    if final_norm:
        y = _rms(y, fg_ref[...])
    out_ref[...] = y


def _ffn(x, o, wo, gain, wgu, wd, final_gain, final_norm):
    n, d = x.shape
    const = lambda i: (0, 0)
    row = lambda i: (i, 0)
    return pl.pallas_call(
        functools.partial(_ffn_kernel, final_norm=final_norm),
        out_shape=jax.ShapeDtypeStruct((n, d), F32),
        grid=(n // ROW_TILE,),
        in_specs=[
            pl.BlockSpec((ROW_TILE, d), row),
            pl.BlockSpec((ROW_TILE, d), row),
            pl.BlockSpec((d, d), const, pipeline_mode=pl.Buffered(1)),
            pl.BlockSpec((1, d), const),
            pl.BlockSpec((d, 2 * D_FF), const, pipeline_mode=pl.Buffered(1)),
            pl.BlockSpec((D_FF, d), const, pipeline_mode=pl.Buffered(1)),
            pl.BlockSpec((1, d), const),
        ],
        out_specs=pl.BlockSpec((ROW_TILE, d), row),
        scratch_shapes=[
            pltpu.VMEM((ROW_TILE, d), F32),
            pltpu.VMEM((ROW_TILE, d), BF16),
            pltpu.VMEM((ROW_TILE, d), F32),
        ],
        compiler_params=_params(("parallel",)),
        name="outproj_ffn",
    )(x, o, wo, gain.reshape(1, d), wgu, wd, final_gain.reshape(1, d))


def _swa_kernel(q_ref, kp_ref, ko_ref, vp_ref, vo_ref, sink_ref, tab_ref, o_ref, ot_ref):
    L = SWA_BLOCK
    n = pl.program_id(1)
    first = jnp.where(n == 0, 1, 0)
    qt = (q_ref[...] * (LOG2E * HEAD_DIM ** -0.5)).T.astype(BF16)
    k = jnp.concatenate([kp_ref[...], ko_ref[...]], axis=0).astype(BF16)
    vt = jnp.concatenate([vp_ref[...], vo_ref[...]], axis=0).T.astype(BF16)
    ones = jnp.ones((SUM_ROWS, 2 * L), BF16)
    sinks = sink_ref[...] * LOG2E
    def scores(kv):
        rows = [qt[hd * HEAD_DIM:(hd + 1) * HEAD_DIM, :] for hd in range(kv * GQA_GROUP, (kv + 1) * GQA_GROUP)]
        return _dot(k[:, kv * HEAD_DIM:(kv + 1) * HEAD_DIM], jnp.concatenate(rows, axis=1))

    s_next = scores(0)
    for kv in range(N_KV_HEADS):
        heads = [kv * GQA_GROUP + g for g in range(GQA_GROUP)]
        s_all = s_next
        if kv + 1 < N_KV_HEADS:
            s_next = scores(kv + 1)
        probs, extras = [], []
        for g, hd in enumerate(heads):
            s = s_all[:, g * L:(g + 1) * L] + tab_ref[first, hd]
            sink = sinks[:, hd:hd + 1]
            m = jnp.maximum(jnp.max(s, axis=0, keepdims=True), sink)
            probs.append(jnp.exp2(s - m).astype(BF16))
            extras.append(jnp.exp2(sink - m))
        vaug = jnp.concatenate([vt[kv * HEAD_DIM:(kv + 1) * HEAD_DIM, :], ones], axis=0)
        pv_all = _dot(vaug, jnp.concatenate(probs, axis=1))
        for g, hd in enumerate(heads):
            pv = pv_all[:, g * L:(g + 1) * L]
            denom = pv[HEAD_DIM:HEAD_DIM + 1, :] + extras[g]
            ot_ref[hd * HEAD_DIM:(hd + 1) * HEAD_DIM, :] = pv[:HEAD_DIM, :] / denom
    o_ref[...] = ot_ref[...].T


def _swa_core(proj, sinks, batch, seq):
    n = proj.shape[0]
    L = SWA_BLOCK
    nb = seq // L
    kcol = ATTN_Q_DIM // ATTN_KV_DIM
    vcol = kcol + 1
    own = lambda c: (lambda b, i: (b * nb + i, c))
    prev = lambda c: (lambda b, i: (b * nb + jnp.maximum(i - 1, 0), c))
    key = jnp.arange(2 * L)[:, None]
    qry = jnp.arange(L)[None, :]
    dist = (L + qry) - key
    band = (dist >= 0) & (dist < SWA_WINDOW)
    slopes = (LOG2E * jnp.asarray(ALIBI_SLOPES, F32))[:, None, None]
    bias = -slopes * dist.astype(F32)[None]
    tab = jnp.stack([jnp.where(band[None], bias, MASKED),
                     jnp.where((band & (key >= L))[None], bias, MASKED)])
    return pl.pallas_call(
        _swa_kernel,
        out_shape=jax.ShapeDtypeStruct((n, ATTN_Q_DIM), F32),
        grid=(batch, nb),
        in_specs=[
            pl.BlockSpec((L, ATTN_Q_DIM), own(0)),
            pl.BlockSpec((L, ATTN_KV_DIM), prev(kcol)),
            pl.BlockSpec((L, ATTN_KV_DIM), own(kcol)),
            pl.BlockSpec((L, ATTN_KV_DIM), prev(vcol)),
            pl.BlockSpec((L, ATTN_KV_DIM), own(vcol)),
            pl.BlockSpec((1, N_Q_HEADS), lambda b, i: (0, 0)),
            pl.BlockSpec((2, N_Q_HEADS, 2 * L, L), lambda b, i: (0, 0, 0, 0), pipeline_mode=pl.Buffered(1)),
        ],
        out_specs=pl.BlockSpec((L, ATTN_Q_DIM), own(0)),
        scratch_shapes=[pltpu.VMEM((ATTN_Q_DIM, L), F32)],
        compiler_params=_params(("parallel", "parallel")),
        name="swa_core",
    )(proj, proj, proj, proj, proj, sinks.reshape(1, N_Q_HEADS), tab)


MOBA_SLOT = 2 * HEAD_DIM
PEN_LANES = 32
SLOPE_LANES = (HEAD_DIM + PEN_LANES, HEAD_DIM + PEN_LANES + 1)
STRIDE_LANES = (HEAD_DIM + PEN_LANES + 2, HEAD_DIM + PEN_LANES + 3)
MOBA_Q_WIDTH = N_Q_HEADS * MOBA_SLOT
MOBA_K_WIDTH = N_KV_HEADS * MOBA_SLOT
MOBA_WIDTH = MOBA_Q_WIDTH + MOBA_K_WIDTH + ATTN_KV_DIM
MASKED = -1e30
LOG2E = math.log2(math.e)
SUM_ROWS = 16
MOBA_KEY_GROUP = 16
MOBA_UNROLL = 4
NORM_MARGIN = 1.02
BOUND_LIMIT = 60.0

_KMEAN_BLOCKS = 8


def _kmean_kernel(k_ref, mean_ref, sqmax_ref):
    k = k_ref[...].reshape(_KMEAN_BLOCKS, MOBA_BLOCK, ATTN_KV_DIM)
    mean_ref[...] = jnp.sum(k, axis=1) * (1.0 / MOBA_BLOCK)
    sqmax_ref[...] = jnp.max(k * k, axis=1)


def _moba_kmean(proj):
    n = proj.shape[0]
    rows = _KMEAN_BLOCKS * MOBA_BLOCK
    out = jax.ShapeDtypeStruct((n // MOBA_BLOCK, ATTN_KV_DIM), F32)
    spec = pl.BlockSpec((_KMEAN_BLOCKS, ATTN_KV_DIM), lambda i: (i, 0))
    return pl.pallas_call(
        _kmean_kernel,
        out_shape=(out, out),
        grid=(n // rows,),
        in_specs=[pl.BlockSpec((rows, ATTN_KV_DIM), lambda i: (i, ATTN_Q_DIM // ATTN_KV_DIM))],
        out_specs=(spec, spec),
        compiler_params=_params(("parallel",)),
        name="moba_kmean",
    )(proj)


def _moba_select_kernel(q_ref, km_ref, expand_ref, slope_ref, slot_ref, kslot_ref, k_ref, v_ref,
                        qa_ref, ks_ref, vt_ref, qn_ref, *, nblk):
    n = pl.program_id(1)
    Lb = MOBA_BLOCK
    qc = (q_ref[...] * (HEAD_DIM ** -0.5)).T
    gate = _dot_split(km_ref[0], qc)
    fill = jnp.zeros((MOBA_SLOT - HEAD_DIM, Lb), F32)
    qt = jnp.concatenate([piece for hd in range(N_Q_HEADS)
                          for piece in (qc[hd * HEAD_DIM:(hd + 1) * HEAD_DIM, :], fill)], axis=0)
    blk = lax.broadcasted_iota(jnp.int32, (nblk, Lb), 0)
    row = blk.astype(F32)
    unselected = []
    for h in range(N_KV_HEADS):
        g = jnp.where(blk < n, gate[h * nblk:(h + 1) * nblk, :], NEG_INF)
        sel = jnp.zeros((nblk, Lb), F32)
        for _ in range(MOBA_TOP_K):
            mx = jnp.max(g, axis=0, keepdims=True)
            is_max = (g == mx) & (mx > NEG_INF)
            first = jnp.min(jnp.where(is_max, row, float(nblk)), axis=0, keepdims=True)
            pick = row == first
            sel = jnp.where(pick, 1.0, sel)
            g = jnp.where(pick, NEG_INF, g)
        unselected.append(1.0 - sel)
    unselected = jnp.concatenate(unselected, axis=0).astype(BF16)
    qs = qt * LOG2E
    norm2 = _dot(slot_ref[...], (qs * qs).astype(BF16))
    qn_ref[...] = jnp.sqrt(norm2[:N_Q_HEADS, :]) * NORM_MARGIN
    q_feat = qs + slope_ref[...]
    for var in range(qa_ref.shape[0]):
        penalty = _dot(expand_ref[var], unselected)
        qa_ref[var] = (q_feat + penalty).astype(BF16)
    ks_ref[0] = _dot(k_ref[...].astype(BF16), kslot_ref[...]).astype(BF16)
    vt_ref[0] = v_ref[...].T.astype(BF16)


def _moba_select(proj, km_mat, expand, slope_feat, batch, seq):
    n = proj.shape[0]
    nblk = seq // MOBA_BLOCK
    width = N_KV_HEADS * nblk
    nvar = expand.shape[0]
    kcol = ATTN_Q_DIM // ATTN_KV_DIM
    slot = jnp.arange(MOBA_Q_WIDTH) // MOBA_SLOT
    lane = jnp.arange(MOBA_Q_WIDTH) % MOBA_SLOT
    slot_sum = ((jnp.arange(LANES)[:, None] == slot[None, :]) & (lane[None, :] < HEAD_DIM)).astype(BF16)
    src = jnp.arange(ATTN_KV_DIM)
    dst = jnp.arange(MOBA_K_WIDTH)
    k_slot = ((src // HEAD_DIM)[:, None] == (dst // MOBA_SLOT)[None, :]) & \
             ((src % HEAD_DIM)[:, None] == (dst % MOBA_SLOT)[None, :])
    k_slot = k_slot.astype(BF16)
    return pl.pallas_call(
        functools.partial(_moba_select_kernel, nblk=nblk),
        out_shape=(jax.ShapeDtypeStruct((nvar, MOBA_Q_WIDTH, n), BF16),
                   jax.ShapeDtypeStruct((n // MOBA_BLOCK, MOBA_BLOCK, MOBA_K_WIDTH), BF16),
                   jax.ShapeDtypeStruct((n // MOBA_BLOCK, ATTN_KV_DIM, MOBA_BLOCK), BF16),
                   jax.ShapeDtypeStruct((N_Q_HEADS, n), F32)),
        grid=(batch, nblk),
        in_specs=[
            pl.BlockSpec((MOBA_BLOCK, ATTN_Q_DIM), lambda b, i: (b * nblk + i, 0)),
            pl.BlockSpec((1, width, ATTN_Q_DIM), lambda b, i: (b, 0, 0)),
            pl.BlockSpec((nvar, MOBA_Q_WIDTH, width), lambda b, i: (0, 0, 0)),
            pl.BlockSpec((MOBA_Q_WIDTH, MOBA_BLOCK), lambda b, i: (0, 0)),
            pl.BlockSpec((LANES, MOBA_Q_WIDTH), lambda b, i: (0, 0)),
            pl.BlockSpec((ATTN_KV_DIM, MOBA_K_WIDTH), lambda b, i: (0, 0)),
            pl.BlockSpec((MOBA_BLOCK, ATTN_KV_DIM), lambda b, i: (b * nblk + i, kcol)),
            pl.BlockSpec((MOBA_BLOCK, ATTN_KV_DIM), lambda b, i: (b * nblk + i, kcol + 1)),
        ],
        out_specs=(pl.BlockSpec((nvar, MOBA_Q_WIDTH, MOBA_BLOCK), lambda b, i: (0, 0, b * nblk + i)),
                   pl.BlockSpec((1, MOBA_BLOCK, MOBA_K_WIDTH), lambda b, i: (b * nblk + i, 0, 0)),
                   pl.BlockSpec((1, ATTN_KV_DIM, MOBA_BLOCK), lambda b, i: (b * nblk + i, 0, 0)),
                   pl.BlockSpec((N_Q_HEADS, MOBA_BLOCK), lambda b, i: (0, b * nblk + i))),
        compiler_params=_params(("parallel", "parallel")),
        name="moba_select",
    )(proj, km_mat, expand, slope_feat, slot_sum, k_slot, proj, proj)


def _moba_scores(ka, qa_ref, var, kv):
    heads = [kv * GQA_GROUP + g for g in range(GQA_GROUP)]
    qt = jnp.concatenate([qa_ref[var, hd * MOBA_SLOT:(hd + 1) * MOBA_SLOT, :] for hd in heads], axis=1)
    return _dot(ka, qt)


def _moba_blocks_update(blocks, qa_ref, var, m_ref, l_ref, acc_ref, causal=None, qn_ref=None):
    keys = lambda kv: jnp.concatenate([blk[0](kv) for blk in blocks], axis=0)
    s_next = _moba_scores(keys(0), qa_ref, var, 0)
    for kv in range(N_KV_HEADS):
        s_all = s_next
        if kv + 1 < N_KV_HEADS:
            s_next = _moba_scores(keys(kv + 1), qa_ref, var, kv + 1)
        vt = jnp.concatenate([blk[1](kv) for blk in blocks], axis=1)
        vt = jnp.concatenate([vt, jnp.ones((SUM_ROWS, vt.shape[1]), BF16)], axis=0)
        k_norm = None
        if qn_ref is not None:
            k_norm = functools.reduce(jnp.maximum, [blk[2](kv) for blk in blocks])
        _moba_softmax_pv(s_all, vt, m_ref, l_ref, acc_ref, kv, causal, qn_ref, k_norm)


def _moba_softmax_pv(s_all, vt, m_ref, l_ref, acc_ref, kv, causal, qn_ref, k_norm):
    Lb = MOBA_BLOCK
    heads = [kv * GQA_GROUP + g for g in range(GQA_GROUP)]
    probs, alphas = [], []
    for g, hd in enumerate(heads):
        s = s_all[:, g * Lb:(g + 1) * Lb]
        if causal is not None:
            s = jnp.where(causal, s, MASKED)
        m_old = m_ref[hd:hd + 1, :]
        if k_norm is None:
            m_new = jnp.maximum(m_old, jnp.max(s, axis=0, keepdims=True))
        else:
            qry = lax.broadcasted_iota(jnp.int32, (1, Lb), 1).astype(F32)
            alibi_top = (qry - (Lb - 1)) * (LOG2E * ALIBI_SLOPES[hd])
            m_new = jnp.maximum(m_old, qn_ref[hd:hd + 1, :] * k_norm + alibi_top)
        alphas.append(jnp.exp2(m_old - m_new))
        probs.append(jnp.exp2(s - m_new).astype(BF16))
        m_ref[hd:hd + 1, :] = m_new
    pv_all = _dot(vt, jnp.concatenate(probs, axis=1))
    for g, hd in enumerate(heads):
        pv = pv_all[:, g * Lb:(g + 1) * Lb]
        rows = slice(hd * HEAD_DIM, (hd + 1) * HEAD_DIM)
        l_ref[hd:hd + 1, :] = alphas[g] * l_ref[hd:hd + 1, :] + pv[HEAD_DIM:HEAD_DIM + 1, :]
        acc_ref[rows, :] = alphas[g] * acc_ref[rows, :] + pv[:HEAD_DIM, :]


def _moba_attn_kernel(nq_ref, jg_ref, bounded_ref, qa_ref, k_ref, vt_ref, qn_ref, kn_ref, o_ref,
                      m_ref, l_ref, acc_ref, *, group, nblk):
    Lb = MOBA_BLOCK
    step = pl.program_id(1)
    n = nq_ref[step]
    first = jg_ref[step] * group
    var = first // PEN_LANES
    bounded = bounded_ref[pl.program_id(0) * nblk + n]

    @pl.when(first == 0)
    def _():
        m_ref[...] = jnp.full(m_ref.shape, NEG_INF, F32)
        l_ref[...] = jnp.zeros(l_ref.shape, F32)
        acc_ref[...] = jnp.zeros(acc_ref.shape, F32)

    lane = lax.broadcasted_iota(jnp.int32, (Lb, MOBA_SLOT), 1)
    key_off = (lax.broadcasted_iota(jnp.int32, (Lb, MOBA_SLOT), 0) - (Lb - 1)).astype(F32)
    slope_lanes = (lane == SLOPE_LANES[0]) | (lane == SLOPE_LANES[1])
    stride_lanes = (lane == STRIDE_LANES[0]) | (lane == STRIDE_LANES[1])
    key_feat = jnp.where(slope_lanes, key_off, 0.0)

    def past_block(i):
        j = first + i
        extra = jnp.where(stride_lanes, (j - n).astype(F32), key_feat)
        extra = jnp.where(lane == HEAD_DIM + (j - var * PEN_LANES), 1.0, extra).astype(BF16)
        return (lambda kv: k_ref[i, :, kv * MOBA_SLOT:(kv + 1) * MOBA_SLOT] + extra,
                lambda kv: vt_ref[i, kv * HEAD_DIM:(kv + 1) * HEAD_DIM, :],
                lambda kv: kn_ref[0, pl.ds(i * N_KV_HEADS + kv, 1), :])

    def past_blocks(count, shift_ref):
        def body(it, carry):
            blocks = [past_block(carry + u) for u in range(count)]
            _moba_blocks_update(blocks, qa_ref, var, m_ref, l_ref, acc_ref, None, shift_ref)
            return carry + count
        return body

    n_past = jnp.minimum(n - first, group)
    n_multi = n_past // MOBA_UNROLL

    def run_past(shift_ref):
        done = lax.fori_loop(0, n_multi, past_blocks(MOBA_UNROLL, shift_ref), 0)
        lax.fori_loop(0, n_past - done, past_blocks(1, shift_ref), done)

    @pl.when(bounded == 1)
    def _():
        run_past(qn_ref)

    @pl.when(bounded != 1)
    def _():
        run_past(None)

    @pl.when(n - first < group)
    def _():
        i = n - first
        key = lax.broadcasted_iota(jnp.int32, (Lb, Lb), 0)
        qry = lax.broadcasted_iota(jnp.int32, (Lb, Lb), 1)
        causal = key <= qry
        extra = key_feat.astype(BF16)
        own = (lambda kv: k_ref[i, :, kv * MOBA_SLOT:(kv + 1) * MOBA_SLOT] + extra,
               lambda kv: vt_ref[i, kv * HEAD_DIM:(kv + 1) * HEAD_DIM, :])
        _moba_blocks_update([own], qa_ref, var, m_ref, l_ref, acc_ref, causal)
        inv = 1.0 / l_ref[...]
        acc = acc_ref[...].reshape(N_Q_HEADS, HEAD_DIM, Lb) * inv[:, None, :]
        o_ref[...] = acc.reshape(N_Q_HEADS * HEAD_DIM, Lb).T


def _moba_attn(qaug, kslot, vt, qnorm, knorm, batch, seq):
    n = batch * seq
    nblk = seq // MOBA_BLOCK
    group = min(MOBA_KEY_GROUP, nblk)
    assert nblk % group == 0
    pairs = [(i, jg) for i in range(nblk) for jg in range(i // group + 1)]
    nq = jnp.asarray([p[0] for p in pairs], jnp.int32)
    jk = jnp.asarray([p[1] for p in pairs], jnp.int32)
    ngrp = nblk // group
    nvar = qaug.shape[0]
    assert PEN_LANES % group == 0, "a key group must not straddle two query variants"
    q_top = jnp.max(qnorm.reshape(N_Q_HEADS, batch, nblk, MOBA_BLOCK), axis=(0, 3))
    bounded = (q_top * jnp.max(knorm, axis=(1, 2))[:, None] <= BOUND_LIMIT).astype(jnp.int32).reshape(-1)
    kn_rows = jnp.broadcast_to(knorm.reshape(batch * ngrp, group * N_KV_HEADS, 1),
                               (batch * ngrp, group * N_KV_HEADS, MOBA_BLOCK))
    qmap = lambda b, s, nq_r, jk_r, bd_r: (b * nblk + nq_r[s], 0)
    kmap = lambda b, s, nq_r, jk_r, bd_r: (b * ngrp + jk_r[s], 0, 0)
    grid_spec = pltpu.PrefetchScalarGridSpec(
        num_scalar_prefetch=3,
        grid=(batch, len(pairs)),
        in_specs=[
            pl.BlockSpec((nvar, MOBA_Q_WIDTH, MOBA_BLOCK), lambda b, s, nq_r, jk_r, bd_r: (0, 0, b * nblk + nq_r[s])),
            pl.BlockSpec((group, MOBA_BLOCK, MOBA_K_WIDTH), kmap),
            pl.BlockSpec((group, ATTN_KV_DIM, MOBA_BLOCK), kmap),
            pl.BlockSpec((N_Q_HEADS, MOBA_BLOCK), lambda b, s, nq_r, jk_r, bd_r: (0, b * nblk + nq_r[s])),
            pl.BlockSpec((1, group * N_KV_HEADS, MOBA_BLOCK), kmap),
        ],
        out_specs=pl.BlockSpec((MOBA_BLOCK, ATTN_Q_DIM), qmap),
        scratch_shapes=[
            pltpu.VMEM((N_Q_HEADS, MOBA_BLOCK), F32),
            pltpu.VMEM((N_Q_HEADS, MOBA_BLOCK), F32),
            pltpu.VMEM((N_Q_HEADS * HEAD_DIM, MOBA_BLOCK), F32),
        ],
    )
    return pl.pallas_call(
        functools.partial(_moba_attn_kernel, group=group, nblk=nblk),
        out_shape=jax.ShapeDtypeStruct((n, ATTN_Q_DIM), F32),
        grid_spec=grid_spec,
        compiler_params=_params(("parallel", "arbitrary")),
        name="moba_attn",
    )(nq, jk, bounded, qaug, kslot, vt, qnorm, kn_rows)


def _moba_slot_weights(w_in):
    d = w_in.shape[0]
    fill = MOBA_SLOT - HEAD_DIM
    wq = w_in[:, :ATTN_Q_DIM].reshape(d, N_Q_HEADS, HEAD_DIM)
    wk = w_in[:, ATTN_Q_DIM:ATTN_Q_DIM + ATTN_KV_DIM].reshape(d, N_KV_HEADS, HEAD_DIM)
    wq = jnp.pad(wq, ((0, 0), (0, 0), (0, fill))).reshape(d, MOBA_Q_WIDTH)
    wk = jnp.pad(wk, ((0, 0), (0, 0), (0, fill))).reshape(d, MOBA_K_WIDTH)
    return jnp.concatenate([wq, wk, w_in[:, ATTN_Q_DIM + ATTN_KV_DIM:]], axis=1)


def _moba_core(proj, batch, seq):
    nblk = seq // MOBA_BLOCK
    nvar = -(-nblk // PEN_LANES)
    width = N_KV_HEADS * nblk
    kmean, ksqmax = _moba_kmean(proj)
    knorm = jnp.sqrt(jnp.sum(ksqmax.reshape(batch, nblk, N_KV_HEADS, HEAD_DIM), axis=-1))
    km = kmean.reshape(batch, nblk, N_KV_HEADS, HEAD_DIM).transpose(0, 2, 3, 1)
    eye = jnp.eye(N_KV_HEADS, dtype=F32)
    km_mat = jnp.einsum('bhej,hH->bheHj', km, eye)
    km_mat = jnp.broadcast_to(km_mat[:, :, None], (batch, N_KV_HEADS, GQA_GROUP, HEAD_DIM, N_KV_HEADS, nblk))
    km_mat = km_mat.reshape(batch, ATTN_Q_DIM, width).transpose(0, 2, 1)
    h_of = jnp.arange(width) // nblk
    c_of = jnp.arange(width) % nblk
    slot = jnp.arange(MOBA_Q_WIDTH) // MOBA_SLOT
    lane = jnp.arange(MOBA_Q_WIDTH) % MOBA_SLOT
    hit = (h_of[:, None] == (slot // GQA_GROUP)[None, :]) & (lane[None, :] == HEAD_DIM + (c_of % PEN_LANES)[:, None])
    in_var = (c_of // PEN_LANES)[None, :, None] == jnp.arange(nvar)[:, None, None]
    expand = jnp.where(hit[None] & in_var, MASKED, 0.0).astype(BF16).transpose(0, 2, 1)
    sigma = LOG2E * jnp.asarray(ALIBI_SLOPES, F32)
    slope_feat = jnp.zeros((MOBA_Q_WIDTH,), F32)
    for lanes, value in ((SLOPE_LANES, sigma), (STRIDE_LANES, sigma * MOBA_BLOCK)):
        hi = value.astype(BF16).astype(F32)
        lo = (value - hi).astype(BF16).astype(F32)
        slope_feat = slope_feat + jnp.where(lane == lanes[0], hi[slot], 0.0) + jnp.where(lane == lanes[1], lo[slot], 0.0)
    slope_feat = jnp.broadcast_to(slope_feat[:, None], (MOBA_Q_WIDTH, MOBA_BLOCK))
    qaug, kslot, vt, qnorm = _moba_select(proj, km_mat, expand, slope_feat, batch, seq)
    return _moba_attn(qaug, kslot, vt, qnorm, knorm, batch, seq)


LIN_CHUNKS = LIN_ROWS // LIN_CHUNK


def _lin_masks(key_dim):
    R = LIN_ROWS
    chunk_bits = int(math.log2(LIN_CHUNK))
    key_bits = int(math.log2(key_dim))
    assert (1 << chunk_bits) == LIN_CHUNK and (1 << key_bits) == key_dim
    r = lax.broadcasted_iota(jnp.int32, (R, R), 0)
    c = lax.broadcasted_iota(jnp.int32, (R, R), 1)
    same = jnp.right_shift(r, chunk_bits) == jnp.right_shift(c, chunk_bits)
    causal = same & (r >= c)
    sums = jnp.concatenate([jnp.where(causal, 1.0, 0.0), jnp.where(same, 1.0, 0.0)], axis=0).astype(BF16)
    rr = lax.broadcasted_iota(jnp.int32, (R, LIN_CHUNKS * key_dim), 0)
    cc = lax.broadcasted_iota(jnp.int32, (R, LIN_CHUNKS * key_dim), 1)
    own = jnp.right_shift(rr, chunk_bits) == jnp.right_shift(cc, key_bits)
    return sums, causal, own


def _lin_prefix(log_g, sums):
    hi = log_g.astype(BF16)
    r1 = log_g - hi.astype(F32)
    mid = r1.astype(BF16)
    lo = (r1 - mid.astype(F32)).astype(BF16)
    return _dot(sums, jnp.concatenate([hi, mid, lo], axis=1))


def _lin_heads(n_heads, head_inputs, emit, st_ref, masks):
    cur = head_inputs(0)
    pre = _lin_prefix(cur[3], masks[0])
    for h in range(n_heads):
        nxt = pre_nxt = None
        if h + 1 < n_heads:
            nxt = head_inputs(h + 1)
            pre_nxt = _lin_prefix(nxt[3], masks[0])
        emit(h, _lin_tile(cur[0], cur[1], cur[2], pre, st_ref, h, masks))
        cur, pre = nxt, pre_nxt


def _lin_tile(q, k, v, acc, st_ref, h, masks):
    _, causal, own = masks
    R, K = q.shape
    acc = acc[:, :K] + (acc[:, K:2 * K] + acc[:, 2 * K:])
    b, b_chunk = acc[:R], acc[R:]
    q_dec = (q * jnp.exp(b)).astype(BF16)
    k_dec = (k * jnp.exp(-b)).astype(BF16)
    k_tail = (k * jnp.exp(b_chunk - b)).astype(BF16)
    vb = v.astype(BF16)
    attn = jnp.where(causal, _dot_nt(q_dec, k_dec), 0.0).astype(BF16)
    o = _dot(attn, vb)
    tile_lanes = lambda x: jnp.concatenate([x] * LIN_CHUNKS, axis=1)
    zero = jnp.zeros((), BF16)
    ds = _dot_tn(vb, jnp.where(own, tile_lanes(k_tail), zero))
    st = st_ref[h]
    states = []
    for c in range(LIN_CHUNKS):
        states.append(st.astype(BF16))
        decay = jnp.exp(b_chunk[c * LIN_CHUNK:c * LIN_CHUNK + 1, :])
        st = st * decay + ds[:, c * K:(c + 1) * K]
    st_ref[h] = st
    o = o + _dot_nt(jnp.where(own, tile_lanes(q_dec), zero), jnp.concatenate(states, axis=1))
    return o


def _head_norm_gate(o, gain, g):
    o = o * lax.rsqrt(jnp.mean(o * o, axis=-1, keepdims=True) + RMS_EPS) * gain
    return o * _silu(g)


def _gla_kernel(q_ref, k_ref, v_ref, g_ref, a_ref, wd_ref, bd_ref, gain_ref, o_ref, st_ref):
    @pl.when(pl.program_id(1) == 0)
    def _():
        st_ref[...] = jnp.zeros(st_ref.shape, F32)

    K, V = GLA_KEY_DIM, GLA_VAL_DIM
    masks = _lin_masks(K)
    z = _dot_split(a_ref[...], wd_ref[...]) + bd_ref[...]
    log_alpha = (jnp.minimum(z, 0.0) - jnp.log(1.0 + jnp.exp(-jnp.abs(z)))) * (1.0 / GLA_GATE_TEMP)
    def head_inputs(h):
        return (q_ref[:, h * K:(h + 1) * K] * (K ** -0.5), k_ref[:, h * K:(h + 1) * K],
                v_ref[:, h * V:(h + 1) * V], log_alpha[:, h * K:(h + 1) * K])

    def emit(h, o):
        o_ref[:, h * V:(h + 1) * V] = _head_norm_gate(o, gain_ref[...], g_ref[:, h * V:(h + 1) * V])

    _lin_heads(GLA_HEADS, head_inputs, emit, st_ref, masks)


def _gla_core(proj, wd_pad, bd, gain, batch, seq):
    n = proj.shape[0]
    nt = seq // LIN_ROWS
    dk = GLA_HEADS * GLA_KEY_DIM
    dv = GLA_HEADS * GLA_VAL_DIM
    rows = lambda c: (lambda b, t: (b * nt + t, c))
    const = lambda b, t: (0, 0)
    return pl.pallas_call(
        _gla_kernel,
        out_shape=jax.ShapeDtypeStruct((n, dv), F32),
        grid=(batch, nt),
        in_specs=[
            pl.BlockSpec((LIN_ROWS, dk), rows(0)),
            pl.BlockSpec((LIN_ROWS, dk), rows(1)),
            pl.BlockSpec((LIN_ROWS, dv), rows(1)),
            pl.BlockSpec((LIN_ROWS, dv), rows(2)),
            pl.BlockSpec((LIN_ROWS, LANES), rows((2 * dk + 2 * dv) // LANES)),
            pl.BlockSpec((LANES, dk), const),
            pl.BlockSpec((1, dk), const),
            pl.BlockSpec((1, GLA_VAL_DIM), const),
        ],
        out_specs=pl.BlockSpec((LIN_ROWS, dv), rows(0)),
        scratch_shapes=[pltpu.VMEM((GLA_HEADS, GLA_VAL_DIM, GLA_KEY_DIM), F32)],
        compiler_params=_params(("parallel", "arbitrary")),
        name="gla_core",
    )(proj, proj, proj, proj, proj, wd_pad, bd.reshape(1, dk), gain.reshape(1, GLA_VAL_DIM))


def _hgrn_kernel(q_ref, f_ref, i_ref, g_ref, lbl_ref, gain_ref, o_ref, st_ref, *, layer):
    @pl.when(pl.program_id(1) == 0)
    def _():
        st_ref[...] = jnp.zeros(st_ref.shape, F32)

    K, V = HGRN_KEY_DIM, HGRN_VAL_DIM
    masks = _lin_masks(K)
    logits = lbl_ref[...]
    e = jnp.exp(logits - jnp.max(logits, axis=0, keepdims=True))
    p = e / jnp.sum(e, axis=0, keepdims=True)
    lb = jnp.zeros((1, logits.shape[1]), F32)
    for l in range(1, layer + 1):
        lb = lb + p[l:l + 1, :]

    def head_inputs(h):
        cols = slice(h * K, (h + 1) * K)
        lbh = lb[:, cols]
        f = lbh + (1.0 - lbh) * _sigmoid(f_ref[:, cols])
        return _silu(q_ref[:, cols]) * (K ** -0.5), 1.0 - f, i_ref[:, cols], jnp.log(f)

    def emit(h, o):
        cols = slice(h * K, (h + 1) * K)
        o_ref[:, cols] = _head_norm_gate(o, gain_ref[...], g_ref[:, cols])

    _lin_heads(HGRN_HEADS, head_inputs, emit, st_ref, masks)


def _hgrn_core(proj, lb_logits, gain, layer, batch, seq):
    n = proj.shape[0]
    nt = seq // LIN_ROWS
    d = D_MODEL
    rows = lambda c: (lambda b, t: (b * nt + t, c))
    const = lambda b, t: (0, 0)
    return pl.pallas_call(
        functools.partial(_hgrn_kernel, layer=layer),
        out_shape=jax.ShapeDtypeStruct((n, d), F32),
        grid=(batch, nt),
        in_specs=[
            pl.BlockSpec((LIN_ROWS, d), rows(0)),
            pl.BlockSpec((LIN_ROWS, d), rows(1)),
            pl.BlockSpec((LIN_ROWS, d), rows(2)),
            pl.BlockSpec((LIN_ROWS, d), rows(3)),
            pl.BlockSpec(lb_logits.shape, const),
            pl.BlockSpec((1, HGRN_VAL_DIM), const),
        ],
        out_specs=pl.BlockSpec((LIN_ROWS, d), rows(0)),
        scratch_shapes=[pltpu.VMEM((HGRN_HEADS, HGRN_VAL_DIM, HGRN_KEY_DIM), F32)],
        compiler_params=_params(("parallel", "arbitrary")),
        name="hgrn_core",
    )(proj, proj, proj, proj, lb_logits, gain.reshape(1, HGRN_VAL_DIM))


def kernel(x, norm_mix, norm_ffn, swa_w_in, swa_sinks, swa_w_out, moba_w_in, moba_w_out, gla_w_in, gla_w_decay_up, gla_b_decay, gla_out_norm, gla_w_out, hgrn_w_in, hgrn_lb_logits, hgrn_out_norm, hgrn_w_out, ffn_w_gate_up, ffn_w_down, final_norm):
    batch, seq, d = x.shape
    depth = norm_mix.shape[0]
    xf = x.reshape(batch * seq, d)
    for i in range(depth):
        kind, j = i % N_MIXERS, i // N_MIXERS
        if kind == 0:
            proj = _norm_proj(xf, norm_mix[i], swa_w_in[j].astype(BF16))
            o = _swa_core(proj, swa_sinks[j], batch, seq)
            w_out = swa_w_out[j]
        elif kind == 1:
            proj = _norm_proj(xf, norm_mix[i], moba_w_in[j].astype(BF16))
            o = _moba_core(proj, batch, seq)
            w_out = moba_w_out[j]
        elif kind == 2:
            pad = LANES - GLA_GATE_RANK
            w_in = jnp.pad(gla_w_in[j], ((0, 0), (0, pad))).astype(BF16)
            wd_pad = jnp.pad(gla_w_decay_up[j], ((0, pad), (0, 0)))
            proj = _norm_proj(xf, norm_mix[i], w_in)
            o = _gla_core(proj, wd_pad, gla_b_decay[j], gla_out_norm[j], batch, seq)
            w_out = gla_w_out[j]
        else:
            proj = _norm_proj(xf, norm_mix[i], hgrn_w_in[j].astype(BF16))
            o = _hgrn_core(proj, hgrn_lb_logits, hgrn_out_norm[j], i, batch, seq)
            w_out = hgrn_w_out[j]
        last = i == depth - 1
        xf = _ffn(xf, o, w_out.astype(BF16), norm_ffn[i], ffn_w_gate_up[i].astype(BF16),
                  ffn_w_down[i].astype(BF16), final_norm, last)
    return xf.reshape(batch, seq, d)
```
